```python
import functools
import jax, jax.numpy as jnp
from jax import lax
import numpy as np

D_MODEL = 2048
BATCH = 2
SEQ = 4096
DEPTH = 2
DEC_BATCH = 128
DEC_SEQ = 8
PAST_LEN = 16384
PAGE_SIZE = 128

MLA_HEADS = 8
MLA_NOPE = 128
MLA_ROPE = 64
MLA_V = 128
KV_RANK = 256
ROPE_THETA = 10000.0
Q_BLOCK = 128
MLA_SCALE = (MLA_NOPE + MLA_ROPE) ** -0.5
RWKV_HEAD = 64
RWKV_WIDTH = 1024
RWKV_HEADS = RWKV_WIDTH // RWKV_HEAD
W_LORA = 64
A_LORA = 64
G_LORA = 128
RWKV_GN_EPS = 64e-5
MLSTM_HEADS = 4
MLSTM_DH = 256
MLSTM_WIDTH = MLSTM_HEADS * MLSTM_DH
MLSTM_CHUNK = 64
N_BRANCH = 3
D_FF = 4 * D_MODEL
NORM_EPS = 1e-6

MLA_Q_COLS = MLA_HEADS * (MLA_NOPE + MLA_ROPE)
MLA_KV_COLS = KV_RANK + MLA_ROPE
RWKV_COLS = 3 * RWKV_WIDTH + W_LORA + A_LORA + G_LORA
MLSTM_COLS = 4 * MLSTM_WIDTH + 2 * MLSTM_HEADS
GATE_COLS = N_BRANCH * D_MODEL
N_IN = MLA_Q_COLS + MLA_KV_COLS + RWKV_COLS + MLSTM_COLS + GATE_COLS

kernel_name = 'mla_rwkv7_mlstm_gated_hybrid_step'

f32 = jnp.float32


def rmsnorm(x, g):
    xf = x.astype(f32)
    y = xf * lax.rsqrt(jnp.mean(xf * xf, axis=-1, keepdims=True) + NORM_EPS)
    return (y * g.astype(f32)).astype(x.dtype)


def rope_tables(pos):
    half = MLA_ROPE // 2
    inv = ROPE_THETA ** (-jnp.arange(half, dtype=f32) / half)
    ang = pos.astype(f32)[:, None] * inv[None, :]
    return jnp.cos(ang), jnp.sin(ang)


def apply_rope(x, cos, sin):
    x1, x2 = jnp.split(x.astype(f32), 2, axis=-1)
    return jnp.concatenate([x1 * cos - x2 * sin, x1 * sin + x2 * cos], axis=-1).astype(x.dtype)


def mla_inputs(u_q, u_kv, g_ckv, w_uk, cos, sin):
    B, T, _ = u_q.shape
    q = u_q.reshape(B, T, MLA_HEADS, MLA_NOPE + MLA_ROPE)
    q_pe = apply_rope(q[..., MLA_NOPE:], cos[:, None, :], sin[:, None, :])
    q_lat = jnp.einsum('bthd,chd->bthc', q[..., :MLA_NOPE], w_uk)
    c_kv = rmsnorm(u_kv[..., :KV_RANK], g_ckv)
    k_pe = apply_rope(u_kv[..., KV_RANK:], cos, sin)
    return q_lat, q_pe, c_kv, k_pe


def mla_attend_prompt(q_lat, q_pe, c_kv, k_pe):
    B, T, H, R = q_lat.shape
    key_pos = jnp.arange(T)

    def one_block(start):
        ql = lax.dynamic_slice_in_dim(q_lat, start, Q_BLOCK, axis=1)
        qp = lax.dynamic_slice_in_dim(q_pe, start, Q_BLOCK, axis=1)
        s = (jnp.einsum('bqhc,bkc->bhqk', ql, c_kv)
             + jnp.einsum('bqhr,bkr->bhqk', qp, k_pe)).astype(f32) * MLA_SCALE
        q_pos = start + jnp.arange(Q_BLOCK)
        s = jnp.where(key_pos[None, :] <= q_pos[:, None], s, -jnp.inf)
        p = jax.nn.softmax(s, axis=-1).astype(c_kv.dtype)
        return jnp.einsum('bhqk,bkc->bqhc', p, c_kv)

    o = lax.map(one_block, jnp.arange(T // Q_BLOCK) * Q_BLOCK)
    return jnp.moveaxis(o, 0, 1).reshape(B, T, H, R)


def mla_attend_sample(q_lat, q_pe, c_new, kpe_new, *, cache_c, cache_pe, page_table, layer):
    T = q_lat.shape[1]
    causal = jnp.tril(jnp.ones((T, T), dtype=bool))

    def one_seq(args):
        ql, qp, cn, pn, pages = args
        cp = cache_c[layer, pages].reshape(-1, KV_RANK)
        pp = cache_pe[layer, pages].reshape(-1, MLA_ROPE)
        s_past = (jnp.einsum('qhc,kc->hqk', ql, cp) + jnp.einsum('qhr,kr->hqk', qp, pp)).astype(f32) * MLA_SCALE
        s_new = (jnp.einsum('qhc,kc->hqk', ql, cn) + jnp.einsum('qhr,kr->hqk', qp, pn)).astype(f32) * MLA_SCALE
        s_new = jnp.where(causal[None], s_new, -jnp.inf)
        mx = jnp.maximum(jnp.max(s_past, axis=-1), jnp.max(s_new, axis=-1))[..., None]
        p_past = jnp.exp(s_past - mx)
        p_new = jnp.exp(s_new - mx)
        den = jnp.sum(p_past, axis=-1) + jnp.sum(p_new, axis=-1)
        o = (jnp.einsum('hqk,kc->qhc', p_past.astype(cp.dtype), cp).astype(f32)
             + jnp.einsum('hqk,kc->qhc', p_new.astype(cn.dtype), cn).astype(f32))
        return (o / den.T[..., None]).astype(cn.dtype)

    return lax.map(one_seq, (q_lat, q_pe, c_new, kpe_new, page_table))


def rwkv7_mix(u, shift_prev, S0, lp):
    B, T, _ = u.shape
    dt = u.dtype
    W = RWKV_WIDTH
    prev = jnp.concatenate([shift_prev[:, None, :].astype(dt), u[:, :-1]], axis=1)
    z = u + (prev - u) * lp['rwkv_mu']
    r, k, v = z[..., :W], z[..., W:2 * W], z[..., 2 * W:3 * W]
    o = 3 * W
    wd = z[..., o:o + W_LORA]
    ad = z[..., o + W_LORA:o + W_LORA + A_LORA]
    gd = z[..., o + W_LORA + A_LORA:]
    w_log = -jax.nn.softplus(-(lp['rwkv_w0'] + jnp.tanh(wd) @ lp['rwkv_w2'])) - 0.5
    decay = jnp.exp(-jnp.exp(w_log.astype(f32)))
    a = jax.nn.sigmoid(lp['rwkv_a0'] + ad @ lp['rwkv_a2'])
    g = jax.nn.sigmoid(gd) @ lp['rwkv_g2']

    def heads(t):
        return t.reshape(B, T, RWKV_HEADS, RWKV_HEAD).astype(f32)

    kk = heads(k * lp['rwkv_kk'])
    kk = kk / jnp.maximum(jnp.sqrt(jnp.sum(kk * kk, axis=-1, keepdims=True)), 1e-12)
    k = k * (1.0 + (a - 1.0) * lp['rwkv_ka'])
    rh, kh, vh, ah, dh = heads(r), heads(k), heads(v), heads(a), heads(decay)

    def step(S, inp):
        r_t, k_t, v_t, kk_t, a_t, d_t = inp
        S = (S * d_t[:, :, None, :]
             + jnp.einsum('bhij,bhj->bhi', S, -kk_t)[..., None] * (kk_t * a_t)[:, :, None, :]
             + v_t[..., None] * k_t[:, :, None, :])
        return S, jnp.einsum('bhij,bhj->bhi', S, r_t)

    xs = tuple(jnp.moveaxis(t, 1, 0) for t in (rh, kh, vh, kk, ah, dh))
    S_T, y = lax.scan(step, S0.astype(f32), xs)
    y = jnp.moveaxis(y, 0, 1)
    mean = jnp.mean(y, axis=-1, keepdims=True)
    var = jnp.mean(jnp.square(y - mean), axis=-1, keepdims=True)
    y = ((y - mean) * lax.rsqrt(var + RWKV_GN_EPS)).reshape(B, T, W)
    y = y * lp['rwkv_ln_w'].astype(f32) + lp['rwkv_ln_b'].astype(f32)
    bonus = (jnp.sum(rh * kh * lp['rwkv_rk'].astype(f32), axis=-1, keepdims=True) * vh).reshape(B, T, W)
    y = (y + bonus) * g.astype(f32)
    return y.astype(dt), u[:, -1], S_T.astype(dt)


def mlstm_chunk(carry, inp):
    C, n, m = carry
    q, k, v, ig, logf = inp
    L = q.shape[2]
    F = jnp.cumsum(logf, axis=-1)
    causal = jnp.tril(jnp.ones((L, L), dtype=bool))
    D_ts = jnp.where(causal, F[..., :, None] - F[..., None, :] + ig[..., None, :], -jnp.inf)
    inter = m[..., None] + F
    m_t = jnp.maximum(inter, jnp.max(D_ts, axis=-1))
    w_inter = jnp.exp(inter - m_t)
    A = jnp.exp(D_ts - m_t[..., None]) * jnp.einsum('bhtd,bhsd->bhts', q, k)
    num = w_inter[..., None] * jnp.einsum('bhvk,bhtk->bhtv', C, q) + jnp.einsum('bhts,bhsv->bhtv', A, v)
    den = w_inter * jnp.einsum('bhk,bhtk->bht', n, q) + jnp.sum(A, axis=-1)
    h = num / jnp.maximum(jnp.abs(den), jnp.exp(-m_t))[..., None]
    m_new = m_t[..., -1]
    carry_decay = jnp.exp(m + F[..., -1] - m_new)
    w_write = jnp.exp(F[..., -1:] - F + ig - m_new[..., None])
    C_new = carry_decay[..., None, None] * C + jnp.einsum('bhs,bhsv,bhsk->bhvk', w_write, v, k)
    n_new = carry_decay[..., None] * n + jnp.einsum('bhs,bhsk->bhk', w_write, k)
    return (C_new, n_new, m_new), h


def mlstm_mix(u, C0, n0, m0, gate_b, norm_g, chunk):
    B, T, _ = u.shape
    dt = u.dtype
    W = MLSTM_WIDTH

    def heads(t):
        return t.reshape(B, T, MLSTM_HEADS, MLSTM_DH).transpose(0, 2, 1, 3).astype(f32)

    q = heads(u[..., :W])
    k = heads(u[..., W:2 * W]) * (MLSTM_DH ** -0.5)
    v = heads(u[..., 2 * W:3 * W])
    o_gate = jax.nn.sigmoid(u[..., 3 * W:4 * W].astype(f32))
    gates = u[..., 4 * W:].astype(f32).reshape(B, T, 2, MLSTM_HEADS) + gate_b.astype(f32)
    ig = gates[:, :, 0].transpose(0, 2, 1)
    logf = jax.nn.log_sigmoid(gates[:, :, 1]).transpose(0, 2, 1)
    nc = T // chunk

    def chunks(t):
        return jnp.moveaxis(t.reshape(t.shape[:2] + (nc, chunk) + t.shape[3:]), 2, 0)

    carry0 = (C0.astype(f32), n0.astype(f32), m0.astype(f32))
    (C, n, m), h = lax.scan(mlstm_chunk, carry0, tuple(chunks(t) for t in (q, k, v, ig, logf)))
    h = jnp.moveaxis(h, 0, 2).reshape(B, MLSTM_HEADS, T, MLSTM_DH).transpose(0, 2, 1, 3)
    h = (h * lax.rsqrt(jnp.mean(h * h, axis=-1, keepdims=True) + NORM_EPS)).reshape(B, T, W)
    y = o_gate * h * norm_g.astype(f32)
    return y.astype(dt), C.astype(dt), n.astype(dt), m.astype(dt)


def trunk_layer(x, cos, sin, attend, shift0, S0, C0, n0, m0, chunk, lp):
    B, T, D = x.shape
    gains = lp['norm_gains']
    h = rmsnorm(x, gains[0])
    u = h @ lp['w_in']
    o0 = MLA_Q_COLS
    o1 = o0 + MLA_KV_COLS
    o2 = o1 + RWKV_COLS
    o3 = o2 + MLSTM_COLS
    q_lat, q_pe, c_kv, k_pe = mla_inputs(u[..., :o0], u[..., o0:o1], lp['g_ckv'], lp['w_uk'], cos, sin)
    o_lat = attend(q_lat, q_pe, c_kv, k_pe)
    y_mla = jnp.einsum('bthc,chd->bthd', o_lat, lp['w_uv']).reshape(B, T, MLA_HEADS * MLA_V)
    y_rwkv, shift_new, S_new = rwkv7_mix(u[..., o1:o2], shift0, S0, lp)
    y_mlstm, C_new, n_new, m_new = mlstm_mix(u[..., o2:o3], C0, n0, m0, lp['mlstm_gate_b'], lp['mlstm_norm'], chunk)
    gates = jax.nn.sigmoid((u[..., o3:] + lp['gate_b'].reshape(-1)).astype(f32)).reshape(B, T, N_BRANCH, D).astype(x.dtype)
    merged = (gates[:, :, 0] * (y_mla @ lp['w_br_mla'])
              + gates[:, :, 1] * (y_rwkv @ lp['w_br_rwkv'])
              + gates[:, :, 2] * (y_mlstm @ lp['w_br_mlstm']))
    x = x + rmsnorm(merged @ lp['w_out'], gains[1])
    hf = rmsnorm(x, gains[2])
    ff = jnp.square(jax.nn.relu(hf @ lp['w_up'])) @ lp['w_down']
    x = x + rmsnorm(ff, gains[3])
    return x, (c_kv, k_pe, shift_new, S_new, C_new, n_new, m_new)


def setup_inputs(seed: int = 0) -> dict:
    key = jax.random.key(seed)
    keys = iter(jax.random.split(key, 64))

    def nrm(shape, scale=1.0):
        return scale * jax.random.normal(next(keys), shape, jnp.float32)

    def gain(shape):
        return 1.0 + nrm(shape, 0.02)

    L, D, W = DEPTH, D_MODEL, RWKV_WIDTH
    n_pages = PAST_LEN // PAGE_SIZE
    n_used = DEC_BATCH * n_pages
    n_pool = n_used + max(1, n_used // 4)
    page_table = jax.random.permutation(next(keys), n_pool)[:n_used].reshape(DEC_BATCH, n_pages).astype(jnp.int32)
    mlstm_gate_b = jnp.concatenate([nrm((L, 1, MLSTM_HEADS), 0.1),
                                    3.0 + nrm((L, 1, MLSTM_HEADS), 0.5)], axis=1)
    return {
        'x_prompt': nrm((BATCH, SEQ, D)),
        'x_sample': nrm((DEC_BATCH, DEC_SEQ, D)),
        'cache_ckv': nrm((L, n_pool, PAGE_SIZE, KV_RANK)),
        'cache_kpe': nrm((L, n_pool, PAGE_SIZE, MLA_ROPE)),
        'page_table': page_table,
        'state_rwkv_shift': nrm((L, DEC_BATCH, RWKV_COLS)),
        'state_rwkv_S': nrm((L, DEC_BATCH, RWKV_HEADS, RWKV_HEAD, RWKV_HEAD), 0.1),
        'state_mlstm_C': nrm((L, DEC_BATCH, MLSTM_HEADS, MLSTM_DH, MLSTM_DH), 0.1),
        'state_mlstm_n': nrm((L, DEC_BATCH, MLSTM_HEADS, MLSTM_DH), 0.1),
        'state_mlstm_m': nrm((L, DEC_BATCH, MLSTM_HEADS)),
        'norm_gains': gain((L, 4, D)),
        'w_in': nrm((L, D, N_IN), D ** -0.5),
        'g_ckv': gain((L, KV_RANK)),
        'w_uk': nrm((L, KV_RANK, MLA_HEADS, MLA_NOPE), KV_RANK ** -0.5),
        'w_uv': nrm((L, KV_RANK, MLA_HEADS, MLA_V), KV_RANK ** -0.5),
        'rwkv_mu': jax.random.uniform(next(keys), (L, RWKV_COLS), jnp.float32),
        'rwkv_w0': jax.random.uniform(next(keys), (L, W), jnp.float32, -4.0, 1.0),
        'rwkv_w2': nrm((L, W_LORA, W), 0.5 * W_LORA ** -0.5),
        'rwkv_a0': nrm((L, W), 0.1),
        'rwkv_a2': nrm((L, A_LORA, W), 0.5 * A_LORA ** -0.5),
        'rwkv_g2': nrm((L, G_LORA, W), G_LORA ** -0.5),
        'rwkv_kk': 0.85 + nrm((L, W), 0.02),
        'rwkv_ka': 1.0 + nrm((L, W), 0.02),
        'rwkv_rk': nrm((L, RWKV_HEADS, RWKV_HEAD), 0.1),
        'rwkv_ln_w': gain((L, W)),
        'rwkv_ln_b': nrm((L, W), 0.01),
        'mlstm_gate_b': mlstm_gate_b,
        'mlstm_norm': gain((L, MLSTM_WIDTH)),
        'gate_b': nrm((L, N_BRANCH, D), 0.01),
        'w_br_mla': nrm((L, MLA_HEADS * MLA_V, D), (MLA_HEADS * MLA_V) ** -0.5),
        'w_br_rwkv': nrm((L, RWKV_WIDTH, D), RWKV_WIDTH ** -0.5),
        'w_br_mlstm': nrm((L, MLSTM_WIDTH, D), MLSTM_WIDTH ** -0.5),
        'w_out': nrm((L, D, D), D ** -0.5),
        'w_up': nrm((L, D, D_FF), D ** -0.5),
        'w_down': nrm((L, D_FF, D), D_FF ** -0.5),
    }


def reference(x_prompt, x_sample, cache_ckv, cache_kpe, page_table, state_rwkv_shift, state_rwkv_S,
              state_mlstm_C, state_mlstm_n, state_mlstm_m, norm_gains, w_in, g_ckv, w_uk, w_uv,
              rwkv_mu, rwkv_w0, rwkv_w2, rwkv_a0, rwkv_a2, rwkv_g2, rwkv_kk, rwkv_ka, rwkv_rk,
              rwkv_ln_w, rwkv_ln_b, mlstm_gate_b, mlstm_norm, gate_b, w_br_mla, w_br_rwkv,
              w_br_mlstm, w_out, w_up, w_down):
    B, T = x_prompt.shape[:2]
    DB, TS = x_sample.shape[:2]
    dt = x_prompt.dtype
    past_len = page_table.shape[1] * cache_ckv.shape[2]
    cos_p, sin_p = rope_tables(jnp.arange(T))
    cos_s, sin_s = rope_tables(past_len + jnp.arange(TS))
    xp, xs = x_prompt, x_sample
    sp, ss = [], []
    for l in range(DEPTH):
        lp = {
            'norm_gains': norm_gains[l], 'w_in': w_in[l], 'g_ckv': g_ckv[l], 'w_uk': w_uk[l], 'w_uv': w_uv[l],
            'rwkv_mu': rwkv_mu[l], 'rwkv_w0': rwkv_w0[l], 'rwkv_w2': rwkv_w2[l], 'rwkv_a0': rwkv_a0[l],
            'rwkv_a2': rwkv_a2[l], 'rwkv_g2': rwkv_g2[l], 'rwkv_kk': rwkv_kk[l], 'rwkv_ka': rwkv_ka[l],
            'rwkv_rk': rwkv_rk[l], 'rwkv_ln_w': rwkv_ln_w[l], 'rwkv_ln_b': rwkv_ln_b[l],
            'mlstm_gate_b': mlstm_gate_b[l], 'mlstm_norm': mlstm_norm[l], 'gate_b': gate_b[l],
            'w_br_mla': w_br_mla[l], 'w_br_rwkv': w_br_rwkv[l], 'w_br_mlstm': w_br_mlstm[l],
            'w_out': w_out[l], 'w_up': w_up[l], 'w_down': w_down[l],
        }
        xp, st_p = trunk_layer(
            xp, cos_p, sin_p, mla_attend_prompt,
            jnp.zeros((B, RWKV_COLS), dt),
            jnp.zeros((B, RWKV_HEADS, RWKV_HEAD, RWKV_HEAD), dt),
            jnp.zeros((B, MLSTM_HEADS, MLSTM_DH, MLSTM_DH), dt),
            jnp.zeros((B, MLSTM_HEADS, MLSTM_DH), dt),
            jnp.zeros((B, MLSTM_HEADS), dt),
            MLSTM_CHUNK, lp)
        attend_s = functools.partial(mla_attend_sample, cache_c=cache_ckv, cache_pe=cache_kpe,
                                     page_table=page_table, layer=l)
        xs, st_s = trunk_layer(
            xs, cos_s, sin_s, attend_s,
            state_rwkv_shift[l], state_rwkv_S[l],
            state_mlstm_C[l], state_mlstm_n[l], state_mlstm_m[l],
            TS, lp)
        sp.append(st_p)
        ss.append(st_s)
    p_ckv = jnp.stack([s[0] for s in sp])
    p_kpe = jnp.stack([s[1] for s in sp])
    p_shift = jnp.stack([s[2] for s in sp])
    p_S = jnp.stack([s[3] for s in sp])
    p_C = jnp.stack([s[4] for s in sp])
    p_n = jnp.stack([s[5] for s in sp])
    p_m = jnp.stack([s[6] for s in sp])
    s_ckv = jnp.stack([s[0] for s in ss])
    s_kpe = jnp.stack([s[1] for s in ss])
    s_shift = jnp.stack([s[2] for s in ss])
    s_S = jnp.stack([s[3] for s in ss])
    s_C = jnp.stack([s[4] for s in ss])
    s_n = jnp.stack([s[5] for s in ss])
    s_m = jnp.stack([s[6] for s in ss])
    return (xp, xs, p_ckv, p_kpe, p_shift, p_S, p_C, p_n, p_m, s_ckv, s_kpe, s_shift, s_S, s_C, s_n, s_m)
```

```python
import functools

import jax
import jax.numpy as jnp
import numpy as np
from jax import lax
from jax.experimental import pallas as pl
from jax.experimental.pallas import tpu as pltpu

D_MODEL = 2048
BATCH = 2
SEQ = 4096
DEPTH = 2
DEC_BATCH = 128
DEC_SEQ = 8
PAST_LEN = 16384
PAGE_SIZE = 128

MLA_HEADS = 8
MLA_NOPE = 128
MLA_ROPE = 64
MLA_V = 128
KV_RANK = 256
ROPE_THETA = 10000.0
MLA_SCALE = (MLA_NOPE + MLA_ROPE) ** -0.5
RWKV_HEAD = 64
RWKV_WIDTH = 1024
RWKV_HEADS = RWKV_WIDTH // RWKV_HEAD
W_LORA = 64
A_LORA = 64
G_LORA = 128
RWKV_GN_EPS = 64e-5
MLSTM_HEADS = 4
MLSTM_DH = 256
MLSTM_WIDTH = MLSTM_HEADS * MLSTM_DH
MLSTM_CHUNK = 64
N_BRANCH = 3
D_FF = 4 * D_MODEL
NORM_EPS = 1e-6

MLA_Q_COLS = MLA_HEADS * (MLA_NOPE + MLA_ROPE)
MLA_KV_COLS = KV_RANK + MLA_ROPE
RWKV_COLS = 3 * RWKV_WIDTH + W_LORA + A_LORA + G_LORA
MLSTM_COLS = 4 * MLSTM_WIDTH + 2 * MLSTM_HEADS
GATE_COLS = N_BRANCH * D_MODEL

LANES = 128
ROPE_HALF = MLA_ROPE // 2
QL_COLS = MLA_HEADS * KV_RANK
QP_COLS = MLA_HEADS * LANES
UQ_COLS = MLA_HEADS * MLA_NOPE + QP_COLS
UKV_COLS = KV_RANK + LANES
N_PAGES = PAST_LEN // PAGE_SIZE
PAGES_PER_STEP = 16
RWKV_CHUNK = 64
M_PROMPT = BATCH * SEQ
M_SAMPLE = DEC_BATCH * DEC_SEQ
M_TOK = M_PROMPT + M_SAMPLE
VMEM_LIMIT = 56 * 1024 * 1024

f32 = jnp.float32
bf16 = jnp.bfloat16


def _cparams(*sem):
    return pltpu.CompilerParams(dimension_semantics=sem, vmem_limit_bytes=VMEM_LIMIT)


def _pick(n, cands):
    for c in cands:
        if n % c == 0:
            return c
    raise ValueError(f"no tile for {n} in {cands}")


def _dot(a, b):
    return jnp.dot(a, b, preferred_element_type=f32)


def _dot_nt(a, b):
    return lax.dot_general(a, b, (((1,), (1,)), ((), ())), preferred_element_type=f32)


def _dot_tn(a, b):
    return lax.dot_general(a, b, (((0,), (0,)), ((), ())), preferred_element_type=f32)


def _split3(x):
    hi = x.astype(bf16)
    r1 = x - hi.astype(f32)
    mid = r1.astype(bf16)
    lo = (r1 - mid.astype(f32)).astype(bf16)
    return hi, mid, lo


def _dot_exact_rhs(sel, x):
    hi, mid, lo = _split3(x)
    return _dot(sel, hi) + _dot(sel, mid) + _dot(sel, lo)


def _dot_exact_lhs(x, sel):
    hi, mid, lo = _split3(x)
    return _dot(hi, sel) + _dot(mid, sel) + _dot(lo, sel)


def _rms(x, g):
    return x * lax.rsqrt(jnp.mean(x * x, axis=-1, keepdims=True) + NORM_EPS) * g


def _emit_norm(hn, h_ref, lo_ref):
    hb = hn.astype(bf16)
    h_ref[...] = hb
    if lo_ref is not None:
        lo_ref[...] = (hn - hb.astype(f32)).astype(bf16)


def _norm_kernel(x_ref, g_ref, h_ref, lo_ref):
    _emit_norm(_rms(x_ref[...], g_ref[...]), h_ref, lo_ref)


def rmsnorm_bf16(x, g):
    m, d = x.shape
    tm = _pick(m, (512, 256))
    row = pl.BlockSpec((tm, d), lambda i: (i, 0))
    return pl.pallas_call(
        _norm_kernel,
        grid=(m // tm,),
        in_specs=[row, pl.BlockSpec((1, d), lambda i: (0, 0))],
        out_specs=[row, row],
        out_shape=[jax.ShapeDtypeStruct((m, d), bf16)] * 2,
        compiler_params=_cparams("parallel"),
        name="rmsnorm",
    )(x, g.reshape(1, d))


def _resnorm_kernel(x_ref, y_ref, gp_ref, gn_ref, xo_ref, h_ref, *lo_ref):
    xn = x_ref[...] + _rms(y_ref[...], gp_ref[...])
    xo_ref[...] = xn
    _emit_norm(_rms(xn, gn_ref[...]), h_ref, lo_ref[0] if lo_ref else None)


def resnorm(x, y, g_post, g_next, with_lo):
    m, d = x.shape
    tm = _pick(m, (512, 256))
    row = pl.BlockSpec((tm, d), lambda i: (i, 0))
    gain = pl.BlockSpec((1, d), lambda i: (0, 0))
    n_h = 2 if with_lo else 1
    return pl.pallas_call(
        _resnorm_kernel,
        grid=(m // tm,),
        in_specs=[row, row, gain, gain],
        out_specs=[row] * (1 + n_h),
        out_shape=[jax.ShapeDtypeStruct((m, d), f32)] + [jax.ShapeDtypeStruct((m, d), bf16)] * n_h,
        compiler_params=_cparams("parallel"),
        name="resnorm",
    )(x, y, g_post.reshape(1, d), g_next.reshape(1, d))


def _mm_kernel(a_ref, b_ref, o_ref, *, act):
    acc = _dot(a_ref[...], b_ref[...])
    if act == "relu2":
        acc = jnp.square(jnp.maximum(acc, 0.0))
    o_ref[...] = acc.astype(o_ref.dtype)


def _mm_bias_sigmoid_kernel(a_ref, b_ref, bias_ref, o_ref):
    acc = _dot(a_ref[...], b_ref[...]) + bias_ref[...]
    o_ref[...] = jax.nn.sigmoid(acc).astype(o_ref.dtype)


def _mm_tiles(m, n):
    tm = _pick(m, (1024, 512, 256))
    tn = _pick(n, (1024, 512, 384, 256, 128)) if n % 1664 else 1664
    return tm, tn


def matmul(a, b, out_dtype=f32, act=None, bias=None):
    m, k = a.shape
    n = b.shape[1]
    tm, tn = _mm_tiles(m, n)
    in_specs = [pl.BlockSpec((tm, k), lambda i, j: (i, 0)), pl.BlockSpec((k, tn), lambda i, j: (0, j))]
    args = [a, b]
    if bias is not None:
        kern = _mm_bias_sigmoid_kernel
        in_specs.append(pl.BlockSpec((1, tn), lambda i, j: (0, j)))
        args.append(bias.reshape(1, n))
    else:
        kern = functools.partial(_mm_kernel, act=act)
    return pl.pallas_call(
        kern,
        grid=(m // tm, n // tn),
        in_specs=in_specs,
        out_specs=pl.BlockSpec((tm, tn), lambda i, j: (i, j)),
        out_shape=jax.ShapeDtypeStruct((m, n), out_dtype),
        compiler_params=_cparams("parallel", "parallel"),
        name="matmul",
    )(*args)


def _mm3_kernel(a_ref, alo_ref, b_ref, blo_ref, o_ref):
    a = a_ref[...]
    b = b_ref[...]
    o_ref[...] = _dot(a, b) + _dot(alo_ref[...], b) + _dot(a, blo_ref[...])


def matmul_split(a, a_lo, b):
    m, k = a.shape
    n = b.shape[1]
    b_hi = b.astype(bf16)
    b_lo = (b - b_hi.astype(f32)).astype(bf16)
    tm = _pick(m, (1024, 512, 256))
    row = pl.BlockSpec((tm, k), lambda i: (i, 0))
    col = pl.BlockSpec((k, n), lambda i: (0, 0))
    return pl.pallas_call(
        _mm3_kernel,
        grid=(m // tm,),
        in_specs=[row, row, col, col],
        out_specs=pl.BlockSpec((tm, n), lambda i: (i, 0)),
        out_shape=jax.ShapeDtypeStruct((m, n), f32),
        compiler_params=_cparams("parallel"),
        name="matmul_split",
    )(a, a_lo, b_hi, b_lo)


def _mm_acc_kernel(a_ref, b_ref, o_ref, acc_ref):
    kk = pl.program_id(2)

    @pl.when(kk == 0)
    def _():
        acc_ref[...] = jnp.zeros_like(acc_ref)

    acc_ref[...] += _dot(a_ref[...], b_ref[...])

    @pl.when(kk == pl.num_programs(2) - 1)
    def _():
        o_ref[...] = acc_ref[...]


def matmul_ksplit(a, b):
    m, k = a.shape
    n = b.shape[1]
    tm = _pick(m, (512, 256))
    tn = _pick(n, (1024, 512, 256))
    tk = _pick(k, (2048, 1024, 512, 256))
    return pl.pallas_call(
        _mm_acc_kernel,
        grid=(m // tm, n // tn, k // tk),
        in_specs=[pl.BlockSpec((tm, tk), lambda i, j, kk: (i, kk)), pl.BlockSpec((tk, tn), lambda i, j, kk: (kk, j))],
        out_specs=pl.BlockSpec((tm, tn), lambda i, j, kk: (i, j)),
        out_shape=jax.ShapeDtypeStruct((m, n), f32),
        scratch_shapes=[pltpu.VMEM((tm, tn), f32)],
        compiler_params=_cparams("parallel", "parallel", "arbitrary"),
        name="matmul_ksplit",
    )(a, b)


def _merge_kernel(ya_ref, yr_ref, ym_ref, wa_ref, wr_ref, wm_ref, ga_ref, gr_ref, gm_ref, o_ref):
    acc = ga_ref[...].astype(f32) * _dot(ya_ref[...], wa_ref[...])
    acc += gr_ref[...].astype(f32) * _dot(yr_ref[...], wr_ref[...])
    acc += gm_ref[...].astype(f32) * _dot(ym_ref[...], wm_ref[...])
    o_ref[...] = acc.astype(o_ref.dtype)


def merge_branches(y_mla, y_rwkv, y_mlstm, w_mla, w_rwkv, w_mlstm, gates):
    m = y_mla.shape[0]
    d = w_mla.shape[1]
    tm = _pick(m, (1024, 512, 256))
    tn = _pick(d, (512, 256))
    nj = d // tn
    ys = [pl.BlockSpec((tm, y.shape[1]), lambda i, j: (i, 0)) for y in (y_mla, y_rwkv, y_mlstm)]
    ws = [pl.BlockSpec((w.shape[0], tn), lambda i, j: (0, j)) for w in (w_mla, w_rwkv, w_mlstm)]
    gs = [pl.BlockSpec((tm, tn), functools.partial(lambda i, j, b: (i, j + b * nj), b=b)) for b in range(N_BRANCH)]
    return pl.pallas_call(
        _merge_kernel,
        grid=(m // tm, nj),
        in_specs=ys + ws + gs,
        out_specs=pl.BlockSpec((tm, tn), lambda i, j: (i, j)),
        out_shape=jax.ShapeDtypeStruct((m, d), bf16),
        compiler_params=_cparams("parallel", "parallel"),
        name="merge_branches",
    )(y_mla, y_rwkv, y_mlstm, w_mla, w_rwkv, w_mlstm, gates, gates, gates)


def _rope_group(p, c4, s4):
    return p * c4 + pltpu.roll(p, 2 * ROPE_HALF, axis=1) * s4


def _mla_prep_kernel(uq_ref, ukv_ref, c4_ref, s4_ref, wuk_ref, g_ref, ql_ref, qp_ref, kc_ref, kpe_ref, ckv_ref, kpef_ref):
    c4 = c4_ref[...]
    s4 = s4_ref[...]
    nope = MLA_HEADS * MLA_NOPE
    for h in range(MLA_HEADS):
        qn = uq_ref[:, h * MLA_NOPE:(h + 1) * MLA_NOPE].astype(bf16)
        ql = _dot(qn, wuk_ref[h]) * MLA_SCALE
        ql_ref[:, h * KV_RANK:(h + 1) * KV_RANK] = ql.astype(ql_ref.dtype)
        qp = _rope_group(uq_ref[:, nope + h * LANES:nope + (h + 1) * LANES], c4, s4) * MLA_SCALE
        qp_ref[:, h * LANES:(h + 1) * LANES] = qp.astype(qp_ref.dtype)
    ckv = _rms(ukv_ref[:, :KV_RANK], g_ref[...])
    ckv_ref[...] = ckv
    kc_ref[...] = ckv.astype(bf16)
    kpe = _rope_group(ukv_ref[:, KV_RANK:], c4, s4)
    kpe_ref[...] = kpe.astype(bf16)
    kpef_ref[...] = kpe[:, :MLA_ROPE]


def mla_prep(u_q, u_kv, c4, s4, w_uk_t, g_ckv, row0, rows, q_dtype):
    tb = _pick(rows, (256, 128))
    assert row0 % tb == 0
    off = row0 // tb

    def rowspec(cols):
        return pl.BlockSpec((tb, cols), lambda i: (i + off, 0))

    def outspec(cols):
        return pl.BlockSpec((tb, cols), lambda i: (i, 0))

    return pl.pallas_call(
        _mla_prep_kernel,
        grid=(rows // tb,),
        in_specs=[rowspec(UQ_COLS), rowspec(UKV_COLS), rowspec(LANES), rowspec(LANES),
                  pl.BlockSpec((MLA_HEADS, MLA_NOPE, KV_RANK), lambda i: (0, 0, 0)),
                  pl.BlockSpec((1, KV_RANK), lambda i: (0, 0))],
        out_specs=[outspec(QL_COLS), outspec(QP_COLS), outspec(KV_RANK), outspec(LANES), outspec(KV_RANK), outspec(MLA_ROPE)],
        out_shape=[jax.ShapeDtypeStruct((rows, QL_COLS), q_dtype), jax.ShapeDtypeStruct((rows, QP_COLS), q_dtype),
                   jax.ShapeDtypeStruct((rows, KV_RANK), bf16), jax.ShapeDtypeStruct((rows, LANES), bf16),
                   jax.ShapeDtypeStruct((rows, KV_RANK), f32), jax.ShapeDtypeStruct((rows, MLA_ROPE), f32)],
        compiler_params=_cparams("parallel"),
        name="mla_prep",
    )(u_q, u_kv, c4, s4, w_uk_t, g_ckv.reshape(1, KV_RANK))


def _attn_prompt_kernel(ql_ref, qp_ref, kc_ref, kpe_ref, wuv_ref, o_ref, m_sc, l_sc, acc_sc, *, tq, tk):
    qi = pl.program_id(1)
    ki = pl.program_id(2)

    @pl.when(ki == 0)
    def _():
        m_sc[...] = jnp.full_like(m_sc, -jnp.inf)
        l_sc[...] = jnp.zeros_like(l_sc)
        acc_sc[...] = jnp.zeros_like(acc_sc)

    @pl.when(ki <= qi)
    def _():
        kc = kc_ref[...]
        kpe = kpe_ref[...]
        rel = lax.broadcasted_iota(jnp.int32, (tq, tk), 1) - lax.broadcasted_iota(jnp.int32, (tq, tk), 0)
        visible = rel <= (qi * tq - ki * tk)
        for h in range(MLA_HEADS):
            s = _dot_nt(ql_ref[:, h * KV_RANK:(h + 1) * KV_RANK], kc)
            s += _dot_nt(qp_ref[:, h * LANES:(h + 1) * LANES], kpe)
            s = jnp.where(visible, s, -jnp.inf)
            m_prev = m_sc[h]
            m_new = jnp.maximum(m_prev, jnp.max(s, axis=1, keepdims=True))
            alpha = jnp.exp(m_prev - m_new)
            p = jnp.exp(s - m_new[:, :1])
            l_sc[h] = alpha * l_sc[h] + jnp.sum(p, axis=1, keepdims=True)
            acc_sc[h] = acc_sc[h] * alpha[:, :1] + _dot(p.astype(bf16), kc)
            m_sc[h] = m_new

    @pl.when(ki == qi)
    def _():
        for h in range(MLA_HEADS):
            o = acc_sc[h] / l_sc[h][:, :1]
            o_ref[:, h * MLA_V:(h + 1) * MLA_V] = _dot(o.astype(bf16), wuv_ref[h]).astype(o_ref.dtype)


def attn_prompt(ql, qp, kc, kpe, w_uv_h):
    tq = tk = _pick(SEQ, (512, 256))
    nq = SEQ // tq

    def qmap(b, qi, ki):
        return (b * nq + qi, 0)

    def kmap(b, qi, ki):
        return (b * nq + jnp.minimum(ki, qi), 0)

    return pl.pallas_call(
        functools.partial(_attn_prompt_kernel, tq=tq, tk=tk),
        grid=(BATCH, nq, nq),
        in_specs=[pl.BlockSpec((tq, QL_COLS), qmap), pl.BlockSpec((tq, QP_COLS), qmap),
                  pl.BlockSpec((tk, KV_RANK), kmap), pl.BlockSpec((tk, LANES), kmap),
                  pl.BlockSpec((MLA_HEADS, KV_RANK, MLA_V), lambda b, qi, ki: (0, 0, 0))],
        out_specs=pl.BlockSpec((tq, MLA_HEADS * MLA_V), qmap),
        out_shape=jax.ShapeDtypeStruct((M_PROMPT, MLA_HEADS * MLA_V), bf16),
        scratch_shapes=[pltpu.VMEM((MLA_HEADS, tq, LANES), f32), pltpu.VMEM((MLA_HEADS, tq, LANES), f32),
                        pltpu.VMEM((MLA_HEADS, tq, KV_RANK), f32)],
        compiler_params=_cparams("parallel", "parallel", "arbitrary"),
        name="attn_prompt",
    )(ql, qp, kc, kpe, w_uv_h)


def _attn_sample_kernel(pt_ref, ql_ref, qp_ref, cn_ref, pn_ref, wuv_ref, *rest):
    del pt_ref
    pps = PAGES_PER_STEP
    ckv_pages = rest[:pps]
    kpe_pages = rest[pps:2 * pps]
    o_ref, q_sc, qp_sc, kc_sc, kp_sc, m_sc, l_sc, acc_sc = rest[2 * pps:]
    j = pl.program_id(1)
    rows = MLA_HEADS * DEC_SEQ

    @pl.when(j == 0)
    def _():
        for h in range(MLA_HEADS):
            q_sc[h * DEC_SEQ:(h + 1) * DEC_SEQ, :] = ql_ref[:, h * KV_RANK:(h + 1) * KV_RANK]
            qp_sc[h * DEC_SEQ:(h + 1) * DEC_SEQ, :] = qp_ref[:, h * LANES:(h + 1) * LANES]
        m_sc[...] = jnp.full_like(m_sc, -jnp.inf)
        l_sc[...] = jnp.zeros_like(l_sc)
        acc_sc[...] = jnp.zeros_like(acc_sc)

    qb = q_sc[...].astype(bf16)
    qpb = qp_sc[...][:, :MLA_ROPE].astype(bf16)

    def online(s, vals):
        m_prev = m_sc[...]
        m_new = jnp.maximum(m_prev, jnp.max(s, axis=1, keepdims=True))
        alpha = jnp.exp(m_prev - m_new)
        p = jnp.exp(s - m_new[:, :1])
        l_sc[...] = alpha * l_sc[...] + jnp.sum(p, axis=1, keepdims=True)
        acc_sc[...] = acc_sc[...] * alpha[:, :1] + _dot(p.astype(bf16), vals)
        m_sc[...] = m_new

    for i in range(pps):
        kc_sc[i * PAGE_SIZE:(i + 1) * PAGE_SIZE, :] = ckv_pages[i][...].astype(bf16)
        kp_sc[i * PAGE_SIZE:(i + 1) * PAGE_SIZE, :] = kpe_pages[i][...].astype(bf16)
    kc = kc_sc[...]
    online(_dot_nt(qb, kc) + _dot_nt(qpb, kp_sc[...]), kc)

    @pl.when(j == pl.num_programs(1) - 1)
    def _():
        pad = jnp.zeros((LANES - DEC_SEQ, KV_RANK), f32)
        cn = jnp.concatenate([cn_ref[...], pad], axis=0).astype(bf16)
        pn = jnp.concatenate([pn_ref[...], pad[:, :MLA_ROPE]], axis=0).astype(bf16)
        s = _dot_nt(qb, cn) + _dot_nt(qpb, pn)
        key = lax.broadcasted_iota(jnp.int32, (rows, LANES), 1)
        tok = lax.broadcasted_iota(jnp.int32, (rows, LANES), 0) % DEC_SEQ
        s = jnp.where(key <= tok, s, -jnp.inf)
        online(s, cn)
        o = acc_sc[...] / l_sc[...][:, :1]
        for h in range(MLA_HEADS):
            oh = o[h * DEC_SEQ:(h + 1) * DEC_SEQ, :].astype(bf16)
            o_ref[:, h * MLA_V:(h + 1) * MLA_V] = _dot(oh, wuv_ref[h])


def attn_sample(ql, qp, c_new, p_new, w_uv_h, cache_ckv, cache_kpe, page_table, layer):
    pps = PAGES_PER_STEP
    assert N_PAGES % pps == 0
    nchunk = N_PAGES // pps
    rows = MLA_HEADS * DEC_SEQ

    def seqspec(cols):
        return pl.BlockSpec((DEC_SEQ, cols), lambda b, j, pt: (b, 0))

    def page_spec(width, i):
        return pl.BlockSpec((None, None, PAGE_SIZE, width), lambda b, j, pt: (layer, pt[b, j * pps + i], 0, 0))

    in_specs = [seqspec(QL_COLS), seqspec(QP_COLS), seqspec(KV_RANK), seqspec(MLA_ROPE),
                pl.BlockSpec((MLA_HEADS, KV_RANK, MLA_V), lambda b, j, pt: (0, 0, 0))]
    in_specs += [page_spec(KV_RANK, i) for i in range(pps)]
    in_specs += [page_spec(MLA_ROPE, i) for i in range(pps)]
    grid_spec = pltpu.PrefetchScalarGridSpec(
        num_scalar_prefetch=1,
        grid=(DEC_BATCH, nchunk),
        in_specs=in_specs,
        out_specs=pl.BlockSpec((DEC_SEQ, MLA_HEADS * MLA_V), lambda b, j, pt: (b, 0)),
        scratch_shapes=[pltpu.VMEM((rows, KV_RANK), f32), pltpu.VMEM((rows, LANES), f32),
                        pltpu.VMEM((pps * PAGE_SIZE, KV_RANK), bf16), pltpu.VMEM((pps * PAGE_SIZE, MLA_ROPE), bf16),
                        pltpu.VMEM((rows, LANES), f32), pltpu.VMEM((rows, LANES), f32), pltpu.VMEM((rows, KV_RANK), f32)],
    )
    return pl.pallas_call(
        _attn_sample_kernel,
        grid_spec=grid_spec,
        out_shape=jax.ShapeDtypeStruct((M_SAMPLE, MLA_HEADS * MLA_V), f32),
        compiler_params=_cparams("parallel", "arbitrary"),
        name="attn_sample",
    )(page_table, ql, qp, c_new, p_new, w_uv_h, *([cache_ckv] * pps), *([cache_kpe] * pps))


def _head_sums(x, bd):
    return _dot_exact_lhs(x, bd)


def _rwkv_prep_kernel(u_ref, prev_ref, mu_ref, w0_ref, a0_ref, kkw_ref, ka_ref, rk_ref, wwa_ref, g2_ref, bd_ref,
                      r_ref, k_ref, v_ref, p_ref, q_ref, w_ref, g_ref, bonus_ref):
    w_ = RWKV_WIDTH
    u = u_ref[...]
    z = u + (prev_ref[...] - u) * mu_ref[...]
    r = z[:, :w_]
    k = z[:, w_:2 * w_]
    v = z[:, 2 * w_:3 * w_]
    wa = z[:, 3 * w_:3 * w_ + W_LORA + A_LORA]
    gd = z[:, 3 * w_ + W_LORA + A_LORA:]
    lane = lax.broadcasted_iota(jnp.int32, wa.shape, 1)
    wa = jnp.where(lane < W_LORA, jnp.tanh(wa), wa)
    lora = _dot(wa.astype(bf16), wwa_ref[...])
    w_log = -jax.nn.softplus(-(w0_ref[...] + lora[:, :w_])) - 0.5
    a = jax.nn.sigmoid(a0_ref[...] + lora[:, w_:])
    g = _dot(jax.nn.sigmoid(gd).astype(bf16), g2_ref[...])
    bd = bd_ref[...]
    kk = k * kkw_ref[...]
    kk = kk / jnp.maximum(jnp.sqrt(_head_sums(kk * kk, bd)), 1e-12)
    k = k * (1.0 + (a - 1.0) * ka_ref[...])
    bonus = _head_sums(r * k * rk_ref[...], bd) * v
    g_ref[...] = g
    bonus_ref[...] = bonus
    wdec = -jnp.exp(w_log)
    for h in range(RWKV_HEADS):
        sl = slice(h * RWKV_HEAD, (h + 1) * RWKV_HEAD)
        r_ref[h] = r[:, sl]
        k_ref[h] = k[:, sl]
        v_ref[h] = v[:, sl]
        p_ref[h] = -kk[:, sl]
        q_ref[h] = (kk * a)[:, sl]
        w_ref[h] = wdec[:, sl]


def rwkv_prep(u, prev, lp_rows, w_wa, g2, bd):
    m = u.shape[0]
    tb = _pick(m, (256, 128))
    rowc = pl.BlockSpec((tb, RWKV_COLS), lambda i: (i, 0))

    def vec(n):
        return pl.BlockSpec((1, n), lambda i: (0, 0))

    def full(a):
        return pl.BlockSpec(a.shape, lambda i: (0, 0))

    headmajor = pl.BlockSpec((RWKV_HEADS, tb, RWKV_HEAD), lambda i: (0, i, 0))
    tokmajor = pl.BlockSpec((tb, RWKV_WIDTH), lambda i: (i, 0))
    hm_shape = jax.ShapeDtypeStruct((RWKV_HEADS, m, RWKV_HEAD), f32)
    tm_shape = jax.ShapeDtypeStruct((m, RWKV_WIDTH), f32)
    return pl.pallas_call(
        _rwkv_prep_kernel,
        grid=(m // tb,),
        in_specs=[rowc, rowc, vec(RWKV_COLS)] + [vec(RWKV_WIDTH)] * 5 + [full(w_wa), full(g2), full(bd)],
        out_specs=[headmajor] * 6 + [tokmajor] * 2,
        out_shape=[hm_shape] * 6 + [tm_shape] * 2,
        compiler_params=_cparams("parallel"),
        name="rwkv_prep",
    )(u, prev, *lp_rows, w_wa, g2, bd)


def _rwkv_chunk_terms(r, k, v, p, q, w, tril_incl, tril_strict, eye, csz):
    g = _dot_exact_rhs(tril_incl, w)
    gp = g - w
    ref = g[csz // 2 - 1:csz // 2, :] if csz > 1 else g[:1, :]
    gl = g[csz - 1:csz, :]
    e_in = jnp.exp(ref - g)
    pt = (p * jnp.exp(gp - ref)).astype(bf16)
    rt = (r * jnp.exp(g - ref)).astype(bf16)
    qt = (q * e_in).astype(bf16)
    kt = (k * e_in).astype(bf16)
    p0 = (p * jnp.exp(gp)).astype(bf16)
    r0 = r * jnp.exp(g)
    e_out = jnp.exp(gl - g)
    qh = (q * e_out).astype(bf16)
    kh = (k * e_out).astype(bf16)
    vb = v.astype(bf16)
    a_qp = _dot_nt(pt, qt) * tril_strict
    a_kp = _dot_nt(pt, kt) * tril_strict
    a_qr = (_dot_nt(rt, qt) * tril_incl).astype(bf16)
    a_kr = (_dot_nt(rt, kt) * tril_incl).astype(bf16)
    tinv = eye + a_qp
    apow = a_qp
    n = 1
    while 2 * n < csz:
        apb = apow.astype(bf16)
        apow = _dot(apb, apb)
        tinv = tinv + _dot(tinv.astype(bf16), apow.astype(bf16))
        n *= 2
    tb_ = tinv.astype(bf16)
    akpv = _dot(a_kp.astype(bf16), vb)
    ph = _dot(tb_, p0)
    wv = _dot(tb_, akpv.astype(bf16))
    phb = ph.astype(bf16)
    wvb = wv.astype(bf16)
    rhat = r0 + _dot(a_qr, phb)
    yint = _dot(a_qr, wvb) + _dot(a_kr, vb)
    mlow = _dot_tn(phb, qh)
    nn = _dot_tn(wvb, qh) + _dot_tn(vb, kh)
    return rhat, yint, mlow, nn, jnp.exp(gl)


def _tri_consts(csz):
    row = lax.broadcasted_iota(jnp.int32, (csz, csz), 0)
    col = lax.broadcasted_iota(jnp.int32, (csz, csz), 1)
    return (col <= row).astype(f32), (col < row).astype(f32), (col == row).astype(f32)


def _rwkv_phase1_kernel(r_ref, k_ref, v_ref, p_ref, q_ref, w_ref, rhat_ref, yint_ref, mlow_ref, nn_ref, e_ref, *, csz, nck):
    tril_incl, tril_strict, eye = _tri_consts(csz)
    tril_b = tril_incl.astype(bf16)
    for c in range(nck):
        sl = slice(c * csz, (c + 1) * csz)
        rhat, yint, mlow, nn, e = _rwkv_chunk_terms(
            r_ref[sl, :], k_ref[sl, :], v_ref[sl, :], p_ref[sl, :], q_ref[sl, :], w_ref[sl, :],
            tril_b, tril_strict, eye, csz)
        rhat_ref[sl, :] = rhat
        yint_ref[sl, :] = yint
        mlow_ref[c] = mlow
        nn_ref[c] = nn
        e_ref[c] = jnp.broadcast_to(e, (8, RWKV_HEAD))


def rwkv_phase1(r, k, v, p, q, w, row0, rows, csz):
    nck = min(rows // csz, 8)
    tb = nck * csz
    assert rows % tb == 0 and row0 % tb == 0
    off = row0 // tb
    nchunks = rows // csz
    hm_in = pl.BlockSpec((None, tb, RWKV_HEAD), lambda h, i: (h, i + off, 0))
    hm_out = pl.BlockSpec((None, tb, RWKV_HEAD), lambda h, i: (h, i, 0))
    sq = pl.BlockSpec((None, nck, RWKV_HEAD, RWKV_HEAD), lambda h, i: (h, i, 0, 0))
    ev = pl.BlockSpec((None, nck, 8, RWKV_HEAD), lambda h, i: (h, i, 0, 0))
    return pl.pallas_call(
        functools.partial(_rwkv_phase1_kernel, csz=csz, nck=nck),
        grid=(RWKV_HEADS, rows // tb),
        in_specs=[hm_in] * 6,
        out_specs=[hm_out, hm_out, sq, sq, ev],
        out_shape=[jax.ShapeDtypeStruct((RWKV_HEADS, rows, RWKV_HEAD), f32)] * 2
        + [jax.ShapeDtypeStruct((RWKV_HEADS, nchunks, RWKV_HEAD, RWKV_HEAD), f32)] * 2
        + [jax.ShapeDtypeStruct((RWKV_HEADS, nchunks, 8, RWKV_HEAD), f32)],
        compiler_params=_cparams("parallel", "parallel"),
        name="rwkv_phase1",
    )(r, k, v, p, q, w)


def _rwkv_phase2_kernel(s0_ref, rhat_ref, yint_ref, mlow_ref, nn_ref, e_ref, y_ref, sout_ref, s_sc, *, csz, nck):
    ci = pl.program_id(1)

    @pl.when(ci == 0)
    def _():
        s_sc[...] = s0_ref[...]

    for h in range(RWKV_HEADS):
        s = s_sc[h]
        for c in range(nck):
            sl = slice(c * csz, (c + 1) * csz)
            sb = s.astype(bf16)
            y_ref[h, sl, :] = _dot_nt(rhat_ref[h, sl, :].astype(bf16), sb) + yint_ref[h, sl, :]
            s = s * e_ref[h, c][:1, :] + _dot(sb, mlow_ref[h, c].astype(bf16)) + nn_ref[h, c]
        s_sc[h] = s

    @pl.when(ci == pl.num_programs(1) - 1)
    def _():
        sout_ref[...] = s_sc[...]


def rwkv_phase2(s0, rhat, yint, mlow, nn, e, nseq, tlen, csz):
    cps = tlen // csz
    nck = min(cps, 4)
    assert cps % nck == 0
    nsteps = cps // nck
    tb = nck * csz
    hm = pl.BlockSpec((RWKV_HEADS, tb, RWKV_HEAD), lambda s, c: (0, s * nsteps + c, 0))
    sq = pl.BlockSpec((RWKV_HEADS, nck, RWKV_HEAD, RWKV_HEAD), lambda s, c: (0, s * nsteps + c, 0, 0))
    ev = pl.BlockSpec((RWKV_HEADS, nck, 8, RWKV_HEAD), lambda s, c: (0, s * nsteps + c, 0, 0))
    st = pl.BlockSpec((None, RWKV_HEADS, RWKV_HEAD, RWKV_HEAD), lambda s, c: (s, 0, 0, 0))
    return pl.pallas_call(
        functools.partial(_rwkv_phase2_kernel, csz=csz, nck=nck),
        grid=(nseq, nsteps),
        in_specs=[st, hm, hm, sq, sq, ev],
        out_specs=[hm, st],
        out_shape=[jax.ShapeDtypeStruct((RWKV_HEADS, nseq * tlen, RWKV_HEAD), f32),
                   jax.ShapeDtypeStruct((nseq, RWKV_HEADS, RWKV_HEAD, RWKV_HEAD), f32)],
        scratch_shapes=[pltpu.VMEM((RWKV_HEADS, RWKV_HEAD, RWKV_HEAD), f32)],
        compiler_params=_cparams("parallel", "arbitrary"),
        name="rwkv_phase2",
    )(s0, rhat, yint, mlow, nn, e)


def _rwkv_out_kernel(y_ref, g_ref, bonus_ref, lnw_ref, lnb_ref, bd_ref, o_ref, y_sc):
    for h in range(RWKV_HEADS):
        y_sc[:, h * RWKV_HEAD:(h + 1) * RWKV_HEAD] = y_ref[h]
    y = y_sc[...]
    bd = bd_ref[...]
    mean = _head_sums(y, bd) * (1.0 / RWKV_HEAD)
    yc = y - mean
    var = _head_sums(yc * yc, bd) * (1.0 / RWKV_HEAD)
    yn = yc * lax.rsqrt(var + RWKV_GN_EPS) * lnw_ref[...] + lnb_ref[...]
    o_ref[...] = ((yn + bonus_ref[...]) * g_ref[...]).astype(o_ref.dtype)


def rwkv_out(y_hm, g, bonus, ln_w, ln_b, bd):
    m = g.shape[0]
    tb = _pick(m, (256, 128))
    tok = pl.BlockSpec((tb, RWKV_WIDTH), lambda i: (i, 0))
    vec = pl.BlockSpec((1, RWKV_WIDTH), lambda i: (0, 0))
    return pl.pallas_call(
        _rwkv_out_kernel,
        grid=(m // tb,),
        in_specs=[pl.BlockSpec((RWKV_HEADS, tb, RWKV_HEAD), lambda i: (0, i, 0)), tok, tok, vec, vec,
                  pl.BlockSpec(bd.shape, lambda i: (0, 0))],
        out_specs=tok,
        out_shape=jax.ShapeDtypeStruct((m, RWKV_WIDTH), bf16),
        scratch_shapes=[pltpu.VMEM((tb, RWKV_WIDTH), f32)],
        compiler_params=_cparams("parallel"),
        name="rwkv_out",
    )(y_hm, g, bonus, ln_w.reshape(1, -1), ln_b.reshape(1, -1), bd)


def _mlstm_kernel(u_ref, ug_ref, gb_ref, ng_ref, c0_ref, n0_ref, m0_ref, y_ref, cout_ref, nout_ref, mout_ref,
                  c_sc, n_sc, m_sc, *, csz):
    ci = pl.program_id(1)
    dh = MLSTM_DH
    w_ = MLSTM_WIDTH

    @pl.when(ci == 0)
    def _():
        c_sc[...] = c0_ref[...]
        n_sc[...] = n0_ref[...]
        m_sc[...] = m0_ref[...]

    tril_incl, _, _ = _tri_consts(csz)
    causal = tril_incl > 0.0
    gates = ug_ref[...] + gb_ref[...]
    lane = lax.broadcasted_iota(jnp.int32, gates.shape, 1)
    logf = jax.nn.log_sigmoid(gates)
    fcum = _dot_exact_rhs(tril_incl.astype(bf16), logf)
    cols = jnp.where(lane < MLSTM_HEADS, gates, fcum)
    sel = (lax.broadcasted_iota(jnp.int32, (8, LANES), 0) == lax.broadcasted_iota(jnp.int32, (8, LANES), 1))
    hi, mid, lo = _split3(cols)
    selb = sel.astype(bf16)
    rows = _dot_nt(selb, hi) + _dot_nt(selb, mid) + _dot_nt(selb, lo)
    for h in range(MLSTM_HEADS):
        q = u_ref[:, h * dh:(h + 1) * dh]
        k = u_ref[:, w_ + h * dh:w_ + (h + 1) * dh] * (dh ** -0.5)
        v = u_ref[:, 2 * w_ + h * dh:2 * w_ + (h + 1) * dh]
        og = jax.nn.sigmoid(u_ref[:, 3 * w_ + h * dh:3 * w_ + (h + 1) * dh])
        ig_row = rows[h:h + 1, :]
        f_row = rows[MLSTM_HEADS + h:MLSTM_HEADS + h + 1, :]
        ig_col = cols[:, h:h + 1]
        f_col = cols[:, MLSTM_HEADS + h:MLSTM_HEADS + h + 1]
        m_old = m_sc[h:h + 1, :1]
        d_ts = jnp.where(causal, f_col - f_row + ig_row, -jnp.inf)
        inter = m_old + f_col
        m_t = jnp.maximum(inter, jnp.max(d_ts, axis=1, keepdims=True))
        w_inter = jnp.exp(inter - m_t)
        qb = q.astype(bf16)
        kb = k.astype(bf16)
        vb = v.astype(bf16)
        a = jnp.exp(d_ts - m_t) * _dot_nt(qb, kb)
        c_old = c_sc[h]
        n_old = n_sc[h:h + 1, :]
        num = w_inter * _dot_nt(qb, c_old.astype(bf16)) + _dot(a.astype(bf16), vb)
        den = w_inter * jnp.sum(q * n_old, axis=1, keepdims=True) + jnp.sum(a, axis=1, keepdims=True)
        hh = num / jnp.maximum(jnp.abs(den), jnp.exp(-m_t))
        hh = hh * lax.rsqrt(jnp.mean(hh * hh, axis=1, keepdims=True) + NORM_EPS)
        y_ref[:, h * dh:(h + 1) * dh] = (og * hh * ng_ref[:, h * dh:(h + 1) * dh]).astype(y_ref.dtype)
        m_new = m_t[csz - 1:csz, :]
        f_last = f_col[csz - 1:csz, :]
        carry = jnp.exp(m_old + f_last - m_new)
        w_write = jnp.exp(f_last - f_col + ig_col - m_new)
        c_sc[h] = carry * c_old + _dot_tn((v * w_write).astype(bf16), kb)
        n_sc[h:h + 1, :] = carry * n_old + jnp.sum(k * w_write, axis=0, keepdims=True)
        m_sc[h:h + 1, :] = jnp.broadcast_to(m_new, (1, LANES))

    @pl.when(ci == pl.num_programs(1) - 1)
    def _():
        cout_ref[...] = c_sc[...]
        nout_ref[...] = n_sc[...]
        mout_ref[...] = m_sc[...]


def mlstm(u, ug, gate_bias_row, norm_g, c0, n0, m0, row0, nseq, tlen, csz):
    cps = tlen // csz
    assert row0 % csz == 0
    off = row0 // csz

    def rowspec(cols):
        return pl.BlockSpec((csz, cols), lambda s, c: (off + s * cps + c, 0))

    def state(shape):
        return pl.BlockSpec((None,) + shape, lambda s, c: (s,) + (0,) * len(shape))

    return pl.pallas_call(
        functools.partial(_mlstm_kernel, csz=csz),
        grid=(nseq, cps),
        in_specs=[rowspec(4 * MLSTM_WIDTH), rowspec(LANES), pl.BlockSpec((1, LANES), lambda s, c: (0, 0)),
                  pl.BlockSpec((1, MLSTM_WIDTH), lambda s, c: (0, 0)),
                  state((MLSTM_HEADS, MLSTM_DH, MLSTM_DH)), state((8, MLSTM_DH)), state((8, LANES))],
        out_specs=[pl.BlockSpec((csz, MLSTM_WIDTH), lambda s, c: (s * cps + c, 0)),
                   state((MLSTM_HEADS, MLSTM_DH, MLSTM_DH)), state((8, MLSTM_DH)), state((8, LANES))],
        out_shape=[jax.ShapeDtypeStruct((nseq * tlen, MLSTM_WIDTH), f32),
                   jax.ShapeDtypeStruct((nseq, MLSTM_HEADS, MLSTM_DH, MLSTM_DH), f32),
                   jax.ShapeDtypeStruct((nseq, 8, MLSTM_DH), f32), jax.ShapeDtypeStruct((nseq, 8, LANES), f32)],
        scratch_shapes=[pltpu.VMEM((MLSTM_HEADS, MLSTM_DH, MLSTM_DH), f32), pltpu.VMEM((8, MLSTM_DH), f32),
                        pltpu.VMEM((8, LANES), f32)],
        compiler_params=_cparams("parallel", "arbitrary"),
        name="mlstm",
    )(u, ug, gate_bias_row, norm_g.reshape(1, -1), c0, n0, m0)


def _column_plan():
    per_head = MLA_NOPE + MLA_ROPE
    q_nope = [h * per_head + d for h in range(MLA_HEADS) for d in range(MLA_NOPE)]
    q_rope = []
    for h in range(MLA_HEADS):
        base = h * per_head + MLA_NOPE
        x1 = [base + e for e in range(ROPE_HALF)]
        x2 = [base + ROPE_HALF + e for e in range(ROPE_HALF)]
        q_rope += x1 + x2 + x2 + x1
    kv_base = MLA_Q_COLS
    ckv = [kv_base + c for c in range(KV_RANK)]
    x1 = [kv_base + KV_RANK + e for e in range(ROPE_HALF)]
    x2 = [kv_base + KV_RANK + ROPE_HALF + e for e in range(ROPE_HALF)]
    return np.asarray(q_nope + q_rope, np.int32), np.asarray(ckv + x1 + x2 + x2 + x1, np.int32)


def _rope_tables():
    pos = jnp.concatenate([jnp.tile(jnp.arange(SEQ), BATCH), jnp.tile(PAST_LEN + jnp.arange(DEC_SEQ), DEC_BATCH)])
    inv = ROPE_THETA ** (-jnp.arange(ROPE_HALF, dtype=f32) / ROPE_HALF)
    ang = pos.astype(f32)[:, None] * inv[None, :]
    cos, sin = jnp.cos(ang), jnp.sin(ang)
    zero = jnp.zeros_like(cos)
    return jnp.concatenate([cos, cos, zero, zero], axis=1), jnp.concatenate([-sin, sin, zero, zero], axis=1)


def _pad8(x, rows_axis):
    pad = [(0, 0)] * x.ndim
    pad[rows_axis] = (0, 8 - x.shape[rows_axis])
    return jnp.pad(x, pad)


def kernel(x_prompt, x_sample, cache_ckv, cache_kpe, page_table, state_rwkv_shift, state_rwkv_S, state_mlstm_C, state_mlstm_n, state_mlstm_m, norm_gains, w_in, g_ckv, w_uk, w_uv, rwkv_mu, rwkv_w0, rwkv_w2, rwkv_a0, rwkv_a2, rwkv_g2, rwkv_kk, rwkv_ka, rwkv_rk, rwkv_ln_w, rwkv_ln_b, mlstm_gate_b, mlstm_norm, gate_b, w_br_mla, w_br_rwkv, w_br_mlstm, w_out, w_up, w_down):
    d = D_MODEL
    o0 = MLA_Q_COLS
    o1 = o0 + MLA_KV_COLS
    o2 = o1 + RWKV_COLS
    o3 = o2 + MLSTM_COLS
    q_cols, kv_cols = _column_plan()
    c4, s4 = _rope_tables()
    head_id = np.arange(RWKV_WIDTH) // RWKV_HEAD
    bd = jnp.asarray(head_id[:, None] == head_id[None, :], bf16)
    zeros_s = jnp.zeros((BATCH, RWKV_HEADS, RWKV_HEAD, RWKV_HEAD), f32)
    zeros_c = jnp.zeros((BATCH, MLSTM_HEADS, MLSTM_DH, MLSTM_DH), f32)
    zeros_n = jnp.zeros((BATCH, 8, MLSTM_DH), f32)
    zeros_m = jnp.zeros((BATCH, 8, LANES), f32)

    x = jnp.concatenate([x_prompt.reshape(M_PROMPT, d), x_sample.reshape(M_SAMPLE, d)], axis=0)
    h, h_lo = rmsnorm_bf16(x, norm_gains[0, 0])
    prompt_states, sample_states = [], []
    for l in range(DEPTH):
        w = w_in[l]
        w_q = jnp.take(w, q_cols, axis=1).astype(bf16)
        w_kv = jnp.take(w, kv_cols, axis=1).astype(bf16)
        w_rw = w[:, o1:o2].astype(bf16)
        w_ml = w[:, o2:o2 + 4 * MLSTM_WIDTH].astype(bf16)
        w_mg = jnp.pad(w[:, o2 + 4 * MLSTM_WIDTH:o3], ((0, 0), (0, LANES - 2 * MLSTM_HEADS)))
        w_gt = w[:, o3:].astype(bf16)

        u_q = matmul(h, w_q)
        u_kv = matmul(h, w_kv)
        u_rw = matmul(h, w_rw)
        u_ml = matmul(h, w_ml)
        u_mg = matmul_split(h, h_lo, w_mg)
        gates = matmul(h, w_gt, out_dtype=bf16, bias=gate_b[l].reshape(-1))

        w_uk_t = jnp.transpose(w_uk[l], (1, 2, 0)).astype(bf16)
        w_uv_h = jnp.transpose(w_uv[l], (1, 0, 2)).astype(bf16)
        ql_p, qp_p, kc_p, kpe_p, ckv_p, kpef_p = mla_prep(u_q, u_kv, c4, s4, w_uk_t, g_ckv[l], 0, M_PROMPT, bf16)
        ql_s, qp_s, _, _, ckv_s, kpef_s = mla_prep(u_q, u_kv, c4, s4, w_uk_t, g_ckv[l], M_PROMPT, M_SAMPLE, f32)
        y_mla_p = attn_prompt(ql_p, qp_p, kc_p, kpe_p, w_uv_h)
        y_mla_s = attn_sample(ql_s, qp_s, ckv_s, kpef_s, w_uv_h, cache_ckv, cache_kpe, page_table, l)
        y_mla = jnp.concatenate([y_mla_p, y_mla_s.astype(bf16)], axis=0)

        up = u_rw[:M_PROMPT].reshape(BATCH, SEQ, RWKV_COLS)
        us = u_rw[M_PROMPT:].reshape(DEC_BATCH, DEC_SEQ, RWKV_COLS)
        prev = jnp.concatenate([
            jnp.concatenate([jnp.zeros((BATCH, 1, RWKV_COLS), f32), up[:, :-1]], axis=1).reshape(M_PROMPT, RWKV_COLS),
            jnp.concatenate([state_rwkv_shift[l][:, None, :], us[:, :-1]], axis=1).reshape(M_SAMPLE, RWKV_COLS)], axis=0)
        zero_blk = jnp.zeros((W_LORA, RWKV_WIDTH), f32)
        w_wa = jnp.concatenate([jnp.concatenate([rwkv_w2[l], zero_blk], axis=1),
                                jnp.concatenate([zero_blk, rwkv_a2[l]], axis=1)], axis=0).astype(bf16)
        lp_rows = [rwkv_mu[l].reshape(1, -1), rwkv_w0[l].reshape(1, -1), rwkv_a0[l].reshape(1, -1),
                   rwkv_kk[l].reshape(1, -1), rwkv_ka[l].reshape(1, -1), rwkv_rk[l].reshape(1, -1)]
        r_, k_, v_, p_, q_, wd_, g_, bonus = rwkv_prep(u_rw, prev, lp_rows, w_wa, rwkv_g2[l].astype(bf16), bd)
        t_p = rwkv_phase1(r_, k_, v_, p_, q_, wd_, 0, M_PROMPT, RWKV_CHUNK)
        t_s = rwkv_phase1(r_, k_, v_, p_, q_, wd_, M_PROMPT, M_SAMPLE, DEC_SEQ)
        y_p, s_p = rwkv_phase2(zeros_s, *t_p, BATCH, SEQ, RWKV_CHUNK)
        y_s, s_s = rwkv_phase2(state_rwkv_S[l], *t_s, DEC_BATCH, DEC_SEQ, DEC_SEQ)
        y_rwkv = rwkv_out(jnp.concatenate([y_p, y_s], axis=1), g_, bonus, rwkv_ln_w[l], rwkv_ln_b[l], bd)

        gb_row = jnp.pad(mlstm_gate_b[l].reshape(1, -1), ((0, 0), (0, LANES - 2 * MLSTM_HEADS)))
        ym_p, c_p, n_p, m_p = mlstm(u_ml, u_mg, gb_row, mlstm_norm[l], zeros_c, zeros_n, zeros_m, 0, BATCH, SEQ, MLSTM_CHUNK)
        m0_s = jnp.broadcast_to(_pad8(state_mlstm_m[l], 1)[:, :, None], (DEC_BATCH, 8, LANES))
        ym_s, c_s, n_s, m_s = mlstm(u_ml, u_mg, gb_row, mlstm_norm[l], state_mlstm_C[l], _pad8(state_mlstm_n[l], 1), m0_s,
                                    M_PROMPT, DEC_BATCH, DEC_SEQ, DEC_SEQ)
        y_mlstm = jnp.concatenate([ym_p, ym_s], axis=0).astype(bf16)

        merged = merge_branches(y_mla, y_rwkv, y_mlstm, w_br_mla[l].astype(bf16), w_br_rwkv[l].astype(bf16),
                                w_br_mlstm[l].astype(bf16), gates)
        attn_out = matmul(merged, w_out[l].astype(bf16))
        x, hf = resnorm(x, attn_out, norm_gains[l, 1], norm_gains[l, 2], False)
        ff = matmul_ksplit(matmul(hf, w_up[l].astype(bf16), out_dtype=bf16, act="relu2"), w_down[l].astype(bf16))
        g_next = norm_gains[l + 1, 0] if l + 1 < DEPTH else norm_gains[l, 3]
        x, h, h_lo = resnorm(x, ff, norm_gains[l, 3], g_next, True)

        prompt_states.append((ckv_p.reshape(BATCH, SEQ, KV_RANK), kpef_p.reshape(BATCH, SEQ, MLA_ROPE), up[:, -1], s_p,
                              c_p, n_p[:, :MLSTM_HEADS], m_p[:, :MLSTM_HEADS, 0]))
        sample_states.append((ckv_s.reshape(DEC_BATCH, DEC_SEQ, KV_RANK), kpef_s.reshape(DEC_BATCH, DEC_SEQ, MLA_ROPE),
                              us[:, -1], s_s, c_s, n_s[:, :MLSTM_HEADS], m_s[:, :MLSTM_HEADS, 0]))

    outs = [x[:M_PROMPT].reshape(BATCH, SEQ, d), x[M_PROMPT:].reshape(DEC_BATCH, DEC_SEQ, d)]
    for states in (prompt_states, sample_states):
        for i in range(7):
            outs.append(jnp.stack([st[i] for st in states]))
    return tuple(outs)
```

```python
import functools

import jax
import jax.numpy as jnp
import numpy as np
from jax import lax
from jax.experimental import pallas as pl
from jax.experimental.pallas import tpu as pltpu

D_MODEL = 2048
BATCH = 2
SEQ = 4096
DEPTH = 2
DEC_BATCH = 128
DEC_SEQ = 8
PAST_LEN = 16384
PAGE_SIZE = 128

MLA_HEADS = 8
MLA_NOPE = 128
MLA_ROPE = 64
MLA_V = 128
KV_RANK = 256
ROPE_THETA = 10000.0
MLA_SCALE = (MLA_NOPE + MLA_ROPE) ** -0.5
RWKV_HEAD = 64
RWKV_WIDTH = 1024
RWKV_HEADS = RWKV_WIDTH // RWKV_HEAD
W_LORA = 64
A_LORA = 64
G_LORA = 128
RWKV_GN_EPS = 64e-5
MLSTM_HEADS = 4
MLSTM_DH = 256
MLSTM_WIDTH = MLSTM_HEADS * MLSTM_DH
MLSTM_CHUNK = 64
N_BRANCH = 3
D_FF = 4 * D_MODEL
NORM_EPS = 1e-6

MLA_Q_COLS = MLA_HEADS * (MLA_NOPE + MLA_ROPE)
MLA_KV_COLS = KV_RANK + MLA_ROPE
RWKV_COLS = 3 * RWKV_WIDTH + W_LORA + A_LORA + G_LORA
MLSTM_COLS = 4 * MLSTM_WIDTH + 2 * MLSTM_HEADS
GATE_COLS = N_BRANCH * D_MODEL

LANES = 128
ROPE_HALF = MLA_ROPE // 2
QL_COLS = MLA_HEADS * KV_RANK
QP_COLS = MLA_HEADS * LANES
UQ_COLS = MLA_HEADS * MLA_NOPE + QP_COLS
UKV_COLS = KV_RANK + LANES
N_PAGES = PAST_LEN // PAGE_SIZE
PAGES_PER_STEP = 8
SEQS_PER_STEP = 2
RWKV_CHUNK = 64
M_PROMPT = BATCH * SEQ
M_SAMPLE = DEC_BATCH * DEC_SEQ
M_TOK = M_PROMPT + M_SAMPLE
VMEM_LIMIT = 56 * 1024 * 1024

f32 = jnp.float32
bf16 = jnp.bfloat16


def _cparams(*sem):
    return pltpu.CompilerParams(dimension_semantics=sem, vmem_limit_bytes=VMEM_LIMIT)


def _pick(n, cands):
    for c in cands:
        if n % c == 0:
            return c
    raise ValueError(f"no tile for {n} in {cands}")


def _dot(a, b):
    return jnp.dot(a, b, preferred_element_type=f32)


def _dot_nt(a, b):
    return lax.dot_general(a, b, (((1,), (1,)), ((), ())), preferred_element_type=f32)


def _dot_tn(a, b):
    return lax.dot_general(a, b, (((0,), (0,)), ((), ())), preferred_element_type=f32)


def _split3(x):
    hi = x.astype(bf16)
    r1 = x - hi.astype(f32)
    mid = r1.astype(bf16)
    lo = (r1 - mid.astype(f32)).astype(bf16)
    return hi, mid, lo


def _dot_exact_rhs(sel, x):
    hi, mid, lo = _split3(x)
    return _dot(sel, hi) + _dot(sel, mid) + _dot(sel, lo)


def _dot_exact_lhs(x, sel):
    hi, mid, lo = _split3(x)
    return _dot(hi, sel) + _dot(mid, sel) + _dot(lo, sel)


def _rms(x, g):
    return x * lax.rsqrt(jnp.mean(x * x, axis=-1, keepdims=True) + NORM_EPS) * g


def _emit_norm(hn, h_ref, lo_ref):
    hb = hn.astype(bf16)
    h_ref[...] = hb
    if lo_ref is not None:
        lo_ref[...] = (hn - hb.astype(f32)).astype(bf16)


def _norm_kernel(x_ref, g_ref, h_ref, lo_ref):
    _emit_norm(_rms(x_ref[...], g_ref[...]), h_ref, lo_ref)


def rmsnorm_bf16(x, g):
    m, d = x.shape
    tm = _pick(m, (512, 256))
    row = pl.BlockSpec((tm, d), lambda i: (i, 0))
    return pl.pallas_call(
        _norm_kernel,
        grid=(m // tm,),
        in_specs=[row, pl.BlockSpec((1, d), lambda i: (0, 0))],
        out_specs=[row, row],
        out_shape=[jax.ShapeDtypeStruct((m, d), bf16)] * 2,
        compiler_params=_cparams("parallel"),
        name="rmsnorm",
    )(x, g.reshape(1, d))


def _resnorm_kernel(x_ref, y_ref, gp_ref, gn_ref, xo_ref, h_ref, *lo_ref):
    xn = x_ref[...] + _rms(y_ref[...], gp_ref[...])
    xo_ref[...] = xn
    _emit_norm(_rms(xn, gn_ref[...]), h_ref, lo_ref[0] if lo_ref else None)


def resnorm(x, y, g_post, g_next, with_lo):
    m, d = x.shape
    tm = _pick(m, (512, 256))
    row = pl.BlockSpec((tm, d), lambda i: (i, 0))
    gain = pl.BlockSpec((1, d), lambda i: (0, 0))
    n_h = 2 if with_lo else 1
    return pl.pallas_call(
        _resnorm_kernel,
        grid=(m // tm,),
        in_specs=[row, row, gain, gain],
        out_specs=[row] * (1 + n_h),
        out_shape=[jax.ShapeDtypeStruct((m, d), f32)] + [jax.ShapeDtypeStruct((m, d), bf16)] * n_h,
        compiler_params=_cparams("parallel"),
        name="resnorm",
    )(x, y, g_post.reshape(1, d), g_next.reshape(1, d))


def _mm_kernel(a_ref, b_ref, o_ref, *, act):
    acc = _dot(a_ref[...], b_ref[...])
    if act == "relu2":
        acc = jnp.square(jnp.maximum(acc, 0.0))
    o_ref[...] = acc.astype(o_ref.dtype)


def _mm_bias_sigmoid_kernel(a_ref, b_ref, bias_ref, o_ref):
    acc = _dot(a_ref[...], b_ref[...]) + bias_ref[...]
    o_ref[...] = jax.nn.sigmoid(acc).astype(o_ref.dtype)


def _mm_tiles(m, n):
    tm = _pick(m, (1024, 512, 256))
    tn = _pick(n, (1024, 512, 384, 256, 128)) if n % 1664 else 1664
    return tm, tn


def matmul(a, b, out_dtype=f32, act=None, bias=None):
    m, k = a.shape
    n = b.shape[1]
    tm, tn = _mm_tiles(m, n)
    in_specs = [pl.BlockSpec((tm, k), lambda i, j: (i, 0)), pl.BlockSpec((k, tn), lambda i, j: (0, j))]
    args = [a, b]
    if bias is not None:
        kern = _mm_bias_sigmoid_kernel
        in_specs.append(pl.BlockSpec((1, tn), lambda i, j: (0, j)))
        args.append(bias.reshape(1, n))
    else:
        kern = functools.partial(_mm_kernel, act=act)
    return pl.pallas_call(
        kern,
        grid=(m // tm, n // tn),
        in_specs=in_specs,
        out_specs=pl.BlockSpec((tm, tn), lambda i, j: (i, j)),
        out_shape=jax.ShapeDtypeStruct((m, n), out_dtype),
        compiler_params=_cparams("parallel", "parallel"),
        name="matmul",
    )(*args)


def _mm3_kernel(a_ref, alo_ref, b_ref, blo_ref, o_ref):
    a = a_ref[...]
    b = b_ref[...]
    o_ref[...] = _dot(a, b) + _dot(alo_ref[...], b) + _dot(a, blo_ref[...])


def matmul_split(a, a_lo, b):
    m, k = a.shape
    n = b.shape[1]
    b_hi = b.astype(bf16)
    b_lo = (b - b_hi.astype(f32)).astype(bf16)
    tm = _pick(m, (1024, 512, 256))
    row = pl.BlockSpec((tm, k), lambda i: (i, 0))
    col = pl.BlockSpec((k, n), lambda i: (0, 0))
    return pl.pallas_call(
        _mm3_kernel,
        grid=(m // tm,),
        in_specs=[row, row, col, col],
        out_specs=pl.BlockSpec((tm, n), lambda i: (i, 0)),
        out_shape=jax.ShapeDtypeStruct((m, n), f32),
        compiler_params=_cparams("parallel"),
        name="matmul_split",
    )(a, a_lo, b_hi, b_lo)


def _mm_acc_kernel(a_ref, b_ref, o_ref, acc_ref):
    kk = pl.program_id(2)

    @pl.when(kk == 0)
    def _():
        acc_ref[...] = jnp.zeros_like(acc_ref)

    acc_ref[...] += _dot(a_ref[...], b_ref[...])

    @pl.when(kk == pl.num_programs(2) - 1)
    def _():
        o_ref[...] = acc_ref[...]


def matmul_ksplit(a, b):
    m, k = a.shape
    n = b.shape[1]
    tm = _pick(m, (512, 256))
    tn = _pick(n, (1024, 512, 256))
    tk = _pick(k, (2048, 1024, 512, 256))
    return pl.pallas_call(
        _mm_acc_kernel,
        grid=(m // tm, n // tn, k // tk),
        in_specs=[pl.BlockSpec((tm, tk), lambda i, j, kk: (i, kk)), pl.BlockSpec((tk, tn), lambda i, j, kk: (kk, j))],
        out_specs=pl.BlockSpec((tm, tn), lambda i, j, kk: (i, j)),
        out_shape=jax.ShapeDtypeStruct((m, n), f32),
        scratch_shapes=[pltpu.VMEM((tm, tn), f32)],
        compiler_params=_cparams("parallel", "parallel", "arbitrary"),
        name="matmul_ksplit",
    )(a, b)


def _merge_kernel(ya_ref, yr_ref, ym_ref, wa_ref, wr_ref, wm_ref, ga_ref, gr_ref, gm_ref, o_ref):
    acc = ga_ref[...].astype(f32) * _dot(ya_ref[...], wa_ref[...])
    acc += gr_ref[...].astype(f32) * _dot(yr_ref[...], wr_ref[...])
    acc += gm_ref[...].astype(f32) * _dot(ym_ref[...], wm_ref[...])
    o_ref[...] = acc.astype(o_ref.dtype)


def merge_branches(y_mla, y_rwkv, y_mlstm, w_mla, w_rwkv, w_mlstm, gates):
    m = y_mla.shape[0]
    d = w_mla.shape[1]
    tm = _pick(m, (1024, 512, 256))
    tn = _pick(d, (512, 256))
    nj = d // tn
    ys = [pl.BlockSpec((tm, y.shape[1]), lambda i, j: (i, 0)) for y in (y_mla, y_rwkv, y_mlstm)]
    ws = [pl.BlockSpec((w.shape[0], tn), lambda i, j: (0, j)) for w in (w_mla, w_rwkv, w_mlstm)]
    gs = [pl.BlockSpec((tm, tn), functools.partial(lambda i, j, b: (i, j + b * nj), b=b)) for b in range(N_BRANCH)]
    return pl.pallas_call(
        _merge_kernel,
        grid=(m // tm, nj),
        in_specs=ys + ws + gs,
        out_specs=pl.BlockSpec((tm, tn), lambda i, j: (i, j)),
        out_shape=jax.ShapeDtypeStruct((m, d), bf16),
        compiler_params=_cparams("parallel", "parallel"),
        name="merge_branches",
    )(y_mla, y_rwkv, y_mlstm, w_mla, w_rwkv, w_mlstm, gates, gates, gates)


def _rope_group(p, c4, s4):
    return p * c4 + pltpu.roll(p, 2 * ROPE_HALF, axis=1) * s4


def _mla_prep_kernel(uq_ref, ukv_ref, c4_ref, s4_ref, wuk_ref, g_ref, ql_ref, qp_ref, kc_ref, kpe_ref, ckv_ref, kpef_ref):
    c4 = c4_ref[...]
    s4 = s4_ref[...]
    nope = MLA_HEADS * MLA_NOPE
    for h in range(MLA_HEADS):
        qn = uq_ref[:, h * MLA_NOPE:(h + 1) * MLA_NOPE].astype(bf16)
        ql = _dot(qn, wuk_ref[h]) * MLA_SCALE
        ql_ref[:, h * KV_RANK:(h + 1) * KV_RANK] = ql.astype(ql_ref.dtype)
        qp = _rope_group(uq_ref[:, nope + h * LANES:nope + (h + 1) * LANES], c4, s4) * MLA_SCALE
        qp_ref[:, h * LANES:(h + 1) * LANES] = qp.astype(qp_ref.dtype)
    ckv = _rms(ukv_ref[:, :KV_RANK], g_ref[...])
    ckv_ref[...] = ckv
    kc_ref[...] = ckv.astype(bf16)
    kpe = _rope_group(ukv_ref[:, KV_RANK:], c4, s4)
    kpe_ref[...] = kpe.astype(bf16)
    kpef_ref[...] = kpe[:, :MLA_ROPE]


def mla_prep(u_q, u_kv, c4, s4, w_uk_t, g_ckv, row0, rows, q_dtype):
    tb = _pick(rows, (256, 128))
    assert row0 % tb == 0
    off = row0 // tb

    def rowspec(cols):
        return pl.BlockSpec((tb, cols), lambda i: (i + off, 0))

    def outspec(cols):
        return pl.BlockSpec((tb, cols), lambda i: (i, 0))

    return pl.pallas_call(
        _mla_prep_kernel,
        grid=(rows // tb,),
        in_specs=[rowspec(UQ_COLS), rowspec(UKV_COLS), rowspec(LANES), rowspec(LANES),
                  pl.BlockSpec((MLA_HEADS, MLA_NOPE, KV_RANK), lambda i: (0, 0, 0)),
                  pl.BlockSpec((1, KV_RANK), lambda i: (0, 0))],
        out_specs=[outspec(QL_COLS), outspec(QP_COLS), outspec(KV_RANK), outspec(LANES), outspec(KV_RANK), outspec(MLA_ROPE)],
        out_shape=[jax.ShapeDtypeStruct((rows, QL_COLS), q_dtype), jax.ShapeDtypeStruct((rows, QP_COLS), q_dtype),
                   jax.ShapeDtypeStruct((rows, KV_RANK), bf16), jax.ShapeDtypeStruct((rows, LANES), bf16),
                   jax.ShapeDtypeStruct((rows, KV_RANK), f32), jax.ShapeDtypeStruct((rows, MLA_ROPE), f32)],
        compiler_params=_cparams("parallel"),
        name="mla_prep",
    )(u_q, u_kv, c4, s4, w_uk_t, g_ckv.reshape(1, KV_RANK))


def _attn_prompt_kernel(ql_ref, qp_ref, kc_ref, kpe_ref, wuv_ref, o_ref, m_sc, l_sc, acc_sc, *, tq, tk):
    qi = pl.program_id(1)
    ki = pl.program_id(2)

    @pl.when(ki == 0)
    def _():
        m_sc[...] = jnp.full_like(m_sc, -jnp.inf)
        l_sc[...] = jnp.zeros_like(l_sc)
        acc_sc[...] = jnp.zeros_like(acc_sc)

    @pl.when(ki <= qi)
    def _():
        kc = kc_ref[...]
        kpe = kpe_ref[...]
        rel = lax.broadcasted_iota(jnp.int32, (tq, tk), 1) - lax.broadcasted_iota(jnp.int32, (tq, tk), 0)
        visible = rel <= (qi * tq - ki * tk)
        for h in range(MLA_HEADS):
            s = _dot_nt(ql_ref[:, h * KV_RANK:(h + 1) * KV_RANK], kc)
            s += _dot_nt(qp_ref[:, h * LANES:(h + 1) * LANES], kpe)
            s = jnp.where(visible, s, -jnp.inf)
            m_prev = m_sc[h]
            m_new = jnp.maximum(m_prev, jnp.max(s, axis=1, keepdims=True))
            alpha = jnp.exp(m_prev - m_new)
            p = jnp.exp(s - m_new[:, :1])
            l_sc[h] = alpha * l_sc[h] + jnp.sum(p, axis=1, keepdims=True)
            acc_sc[h] = acc_sc[h] * alpha[:, :1] + _dot(p.astype(bf16), kc)
            m_sc[h] = m_new

    @pl.when(ki == qi)
    def _():
        for h in range(MLA_HEADS):
            o = acc_sc[h] / l_sc[h][:, :1]
            o_ref[:, h * MLA_V:(h + 1) * MLA_V] = _dot(o.astype(bf16), wuv_ref[h]).astype(o_ref.dtype)


def attn_prompt(ql, qp, kc, kpe, w_uv_h):
    tq = tk = _pick(SEQ, (512, 256))
    nq = SEQ // tq

    def qmap(b, qi, ki):
        return (b * nq + qi, 0)

    def kmap(b, qi, ki):
        return (b * nq + jnp.minimum(ki, qi), 0)

    return pl.pallas_call(
        functools.partial(_attn_prompt_kernel, tq=tq, tk=tk),
        grid=(BATCH, nq, nq),
        in_specs=[pl.BlockSpec((tq, QL_COLS), qmap), pl.BlockSpec((tq, QP_COLS), qmap),
                  pl.BlockSpec((tk, KV_RANK), kmap), pl.BlockSpec((tk, LANES), kmap),
                  pl.BlockSpec((MLA_HEADS, KV_RANK, MLA_V), lambda b, qi, ki: (0, 0, 0))],
        out_specs=pl.BlockSpec((tq, MLA_HEADS * MLA_V), qmap),
        out_shape=jax.ShapeDtypeStruct((M_PROMPT, MLA_HEADS * MLA_V), bf16),
        scratch_shapes=[pltpu.VMEM((MLA_HEADS, tq, LANES), f32), pltpu.VMEM((MLA_HEADS, tq, LANES), f32),
                        pltpu.VMEM((MLA_HEADS, tq, KV_RANK), f32)],
        compiler_params=_cparams("parallel", "parallel", "arbitrary"),
        name="attn_prompt",
    )(ql, qp, kc, kpe, w_uv_h)


def _attn_sample_kernel(pt_ref, ql_ref, qp_ref, cn_ref, pn_ref, wuv_ref, *rest):
    del pt_ref
    pps = PAGES_PER_STEP
    nsq = SEQS_PER_STEP
    ckv_pages = rest[:nsq * pps]
    kpe_pages = rest[nsq * pps:2 * nsq * pps]
    o_ref, q_sc, qp_sc, kc_sc, kp_sc, m_sc, l_sc, acc_sc = rest[2 * nsq * pps:]
    j = pl.program_id(1)
    rows = MLA_HEADS * DEC_SEQ
    seqs = range(nsq)

    @pl.when(j == 0)
    def _():
        for r in seqs:
            tok = slice(r * DEC_SEQ, (r + 1) * DEC_SEQ)
            for h in range(MLA_HEADS):
                q_sc[r, h * DEC_SEQ:(h + 1) * DEC_SEQ, :] = ql_ref[tok, h * KV_RANK:(h + 1) * KV_RANK]
                qp_sc[r, h * DEC_SEQ:(h + 1) * DEC_SEQ, :] = qp_ref[tok, h * LANES:(h + 1) * LANES]
        m_sc[...] = jnp.full_like(m_sc, -jnp.inf)
        l_sc[...] = jnp.zeros_like(l_sc)
        acc_sc[...] = jnp.zeros_like(acc_sc)

    qb = [q_sc[r].astype(bf16) for r in seqs]
    qpb = [qp_sc[r][:, :MLA_ROPE].astype(bf16) for r in seqs]

    def online(r, s, vals):
        m_prev = m_sc[r]
        m_new = jnp.maximum(m_prev, jnp.max(s, axis=1, keepdims=True))
        alpha = jnp.exp(m_prev - m_new)
        p = jnp.exp(s - m_new[:, :1])
        l_sc[r] = alpha * l_sc[r] + jnp.sum(p, axis=1, keepdims=True)
        acc_sc[r] = acc_sc[r] * alpha[:, :1] + _dot(p.astype(bf16), vals)
        m_sc[r] = m_new

    for r in seqs:
        for i in range(pps):
            kc_sc[r, i * PAGE_SIZE:(i + 1) * PAGE_SIZE, :] = ckv_pages[r * pps + i][...].astype(bf16)
            kp_sc[r, :, i * PAGE_SIZE:(i + 1) * PAGE_SIZE] = kpe_pages[r * pps + i][...].astype(bf16)
    kcs = [kc_sc[r] for r in seqs]
    scores = [_dot_nt(qb[r], kcs[r]) + _dot(qpb[r], kp_sc[r]) for r in seqs]
    for r in seqs:
        online(r, scores[r], kcs[r])

    @pl.when(j == pl.num_programs(1) - 1)
    def _():
        pad = jnp.zeros((LANES - DEC_SEQ, KV_RANK), f32)
        key = lax.broadcasted_iota(jnp.int32, (rows, LANES), 1)
        tokid = lax.broadcasted_iota(jnp.int32, (rows, LANES), 0) % DEC_SEQ
        for r in seqs:
            tok = slice(r * DEC_SEQ, (r + 1) * DEC_SEQ)
            cn = jnp.concatenate([cn_ref[tok, :], pad], axis=0).astype(bf16)
            pn = jnp.concatenate([pn_ref[tok, :], pad[:, :MLA_ROPE]], axis=0).astype(bf16)
            s = _dot_nt(qb[r], cn) + _dot_nt(qpb[r], pn)
            online(r, jnp.where(key <= tokid, s, -jnp.inf), cn)
            o = acc_sc[r] / l_sc[r][:, :1]
            for h in range(MLA_HEADS):
                oh = o[h * DEC_SEQ:(h + 1) * DEC_SEQ, :].astype(bf16)
                o_ref[tok, h * MLA_V:(h + 1) * MLA_V] = _dot(oh, wuv_ref[h])


def attn_sample(ql, qp, c_new, p_new, w_uv_h, cache_ckv, cache_kpe_t, page_table, layer):
    pps = PAGES_PER_STEP
    nsq = SEQS_PER_STEP
    assert N_PAGES % pps == 0 and DEC_BATCH % nsq == 0
    nchunk = N_PAGES // pps
    rows = MLA_HEADS * DEC_SEQ
    keys = pps * PAGE_SIZE

    def seqspec(cols):
        return pl.BlockSpec((nsq * DEC_SEQ, cols), lambda b, j, pt: (b, 0))

    def page_spec(shape, r, i):
        return pl.BlockSpec((None, None) + shape, lambda b, j, pt: (layer, pt[b * nsq + r, j * pps + i], 0, 0))

    in_specs = [seqspec(QL_COLS), seqspec(QP_COLS), seqspec(KV_RANK), seqspec(MLA_ROPE),
                pl.BlockSpec((MLA_HEADS, KV_RANK, MLA_V), lambda b, j, pt: (0, 0, 0))]
    in_specs += [page_spec((PAGE_SIZE, KV_RANK), r, i) for r in range(nsq) for i in range(pps)]
    in_specs += [page_spec((MLA_ROPE, PAGE_SIZE), r, i) for r in range(nsq) for i in range(pps)]
    grid_spec = pltpu.PrefetchScalarGridSpec(
        num_scalar_prefetch=1,
        grid=(DEC_BATCH // nsq, nchunk),
        in_specs=in_specs,
        out_specs=pl.BlockSpec((nsq * DEC_SEQ, MLA_HEADS * MLA_V), lambda b, j, pt: (b, 0)),
        scratch_shapes=[pltpu.VMEM((nsq, rows, KV_RANK), f32), pltpu.VMEM((nsq, rows, LANES), f32),
                        pltpu.VMEM((nsq, keys, KV_RANK), bf16), pltpu.VMEM((nsq, MLA_ROPE, keys), bf16),
                        pltpu.VMEM((nsq, rows, LANES), f32), pltpu.VMEM((nsq, rows, LANES), f32),
                        pltpu.VMEM((nsq, rows, KV_RANK), f32)],
    )
    n_pg = nsq * pps
    return pl.pallas_call(
        _attn_sample_kernel,
        grid_spec=grid_spec,
        out_shape=jax.ShapeDtypeStruct((M_SAMPLE, MLA_HEADS * MLA_V), f32),
        compiler_params=_cparams("parallel", "arbitrary"),
        name="attn_sample",
    )(page_table, ql, qp, c_new, p_new, w_uv_h, *([cache_ckv] * n_pg), *([cache_kpe_t] * n_pg))


def _head_sums(x, bd):
    return _dot_exact_lhs(x, bd)


def _rwkv_prep_kernel(u_ref, st_ref, mu_ref, w0_ref, a0_ref, kkw_ref, ka_ref, rk_ref, wwa_ref, g2_ref, bd_ref,
                      r_ref, k_ref, v_ref, p_ref, q_ref, w_ref, gc_ref, g_ref, bonus_ref, *, tb, n_prompt_blocks):
    w_ = RWKV_WIDTH
    nst = st_ref.shape[0]
    is_sample = pl.program_id(0) >= n_prompt_blocks
    stride = jnp.where(is_sample, DEC_SEQ, tb)
    row_e = lax.broadcasted_iota(jnp.int32, (tb, nst), 0)
    col_e = lax.broadcasted_iota(jnp.int32, (tb, nst), 1)
    place = jnp.where(row_e == col_e * stride, 1.0, 0.0).astype(bf16)
    starts = _dot_exact_rhs(place, st_ref[...])
    u = u_ref[...]
    row = lax.broadcasted_iota(jnp.int32, (tb, 1), 0)
    is_start = jnp.where(is_sample, row % DEC_SEQ, row) == 0
    prev = jnp.where(is_start, starts, pltpu.roll(u, 1, axis=0))
    z = u + (prev - u) * mu_ref[...]
    r = z[:, :w_]
    k = z[:, w_:2 * w_]
    v = z[:, 2 * w_:3 * w_]
    wa = z[:, 3 * w_:3 * w_ + W_LORA + A_LORA]
    gd = z[:, 3 * w_ + W_LORA + A_LORA:]
    lane = lax.broadcasted_iota(jnp.int32, wa.shape, 1)
    wa = jnp.where(lane < W_LORA, jnp.tanh(wa), wa)
    lora = _dot(wa.astype(bf16), wwa_ref[...])
    w_log = -jax.nn.softplus(-(w0_ref[...] + lora[:, :w_])) - 0.5
    a = jax.nn.sigmoid(a0_ref[...] + lora[:, w_:])
    g = _dot(jax.nn.sigmoid(gd).astype(bf16), g2_ref[...])
    bd = bd_ref[...]
    kk = k * kkw_ref[...]
    kk = kk / jnp.maximum(jnp.sqrt(_head_sums(kk * kk, bd)), 1e-12)
    k = k * (1.0 + (a - 1.0) * ka_ref[...])
    bonus = _head_sums(r * k * rk_ref[...], bd) * v
    g_ref[...] = g
    bonus_ref[...] = bonus
    wdec = -jnp.exp(w_log)
    shift = jnp.where(is_sample, DEC_SEQ.bit_length() - 1, RWKV_CHUNK.bit_length() - 1)
    row_c = lax.broadcasted_iota(jnp.int32, (tb, tb), 0)
    col_c = lax.broadcasted_iota(jnp.int32, (tb, tb), 1)
    same_chunk = lax.shift_right_logical(row_c, shift) == lax.shift_right_logical(col_c, shift)
    tril_blk = jnp.where(same_chunk, jnp.where(col_c <= row_c, 1.0, 0.0), 0.0).astype(bf16)
    gcum = _dot_exact_rhs(tril_blk, wdec)
    for h in range(RWKV_HEADS):
        sl = slice(h * RWKV_HEAD, (h + 1) * RWKV_HEAD)
        r_ref[h] = r[:, sl]
        k_ref[h] = k[:, sl]
        v_ref[h] = v[:, sl]
        p_ref[h] = -kk[:, sl]
        q_ref[h] = (kk * a)[:, sl]
        w_ref[h] = wdec[:, sl]
        gc_ref[h] = gcum[:, sl]


def rwkv_prep(u, starts, lp_rows, w_wa, g2, bd, tb):
    m = u.shape[0]
    nst = starts.shape[1]
    rowc = pl.BlockSpec((tb, RWKV_COLS), lambda i: (i, 0))

    def vec(n):
        return pl.BlockSpec((1, n), lambda i: (0, 0))

    def full(a):
        return pl.BlockSpec(a.shape, lambda i: (0, 0))

    headmajor = pl.BlockSpec((RWKV_HEADS, tb, RWKV_HEAD), lambda i: (0, i, 0))
    tokmajor = pl.BlockSpec((tb, RWKV_WIDTH), lambda i: (i, 0))
    hm_shape = jax.ShapeDtypeStruct((RWKV_HEADS, m, RWKV_HEAD), f32)
    tm_shape = jax.ShapeDtypeStruct((m, RWKV_WIDTH), f32)
    return pl.pallas_call(
        functools.partial(_rwkv_prep_kernel, tb=tb, n_prompt_blocks=M_PROMPT // tb),
        grid=(m // tb,),
        in_specs=[rowc, pl.BlockSpec((None, nst, RWKV_COLS), lambda i: (i, 0, 0)), vec(RWKV_COLS)]
        + [vec(RWKV_WIDTH)] * 5 + [full(w_wa), full(g2), full(bd)],
        out_specs=[headmajor] * 7 + [tokmajor] * 2,
        out_shape=[hm_shape] * 7 + [tm_shape] * 2,
        compiler_params=_cparams("parallel"),
        name="rwkv_prep",
    )(u, starts, *lp_rows, w_wa, g2, bd)


def _tri_consts(csz):
    row = lax.broadcasted_iota(jnp.int32, (csz, csz), 0)
    col = lax.broadcasted_iota(jnp.int32, (csz, csz), 1)
    return (col <= row).astype(f32), (col < row).astype(f32), (col == row).astype(f32)


def _rwkv_phase1_kernel(r_ref, k_ref, v_ref, p_ref, q_ref, w_ref, g_ref, rhat_ref, yint_ref, mlow_ref, nn_ref, e_ref,
                        *, csz, nck):
    tril_incl, tril_strict, eye = _tri_consts(csz)
    cs = range(nck)
    sls = [slice(c * csz, (c + 1) * csz) for c in cs]

    def stack(a, b):
        return jnp.concatenate([a, b], axis=0).astype(bf16)

    pr, qt, kt, p0, r0, qh, kh, vb, e_last = [], [], [], [], [], [], [], [], []
    for sl in sls:
        r, k, v, p, q = r_ref[sl, :], k_ref[sl, :], v_ref[sl, :], p_ref[sl, :], q_ref[sl, :]
        g = g_ref[sl, :]
        gp = g - w_ref[sl, :]
        ref = g[csz // 2 - 1:csz // 2, :]
        gl = g[csz - 1:csz, :]
        e_in = jnp.exp(ref - g)
        e_out = jnp.exp(gl - g)
        pr.append(stack(p * jnp.exp(gp - ref), r * jnp.exp(g - ref)))
        qt.append((q * e_in).astype(bf16))
        kt.append((k * e_in).astype(bf16))
        p0.append((p * jnp.exp(gp)).astype(bf16))
        r0.append(r * jnp.exp(g))
        qh.append((q * e_out).astype(bf16))
        kh.append((k * e_out).astype(bf16))
        vb.append(v.astype(bf16))
        e_last.append(jnp.exp(gl))
    gq = [_dot_nt(pr[c], qt[c]) for c in cs]
    gk = [_dot_nt(pr[c], kt[c]) for c in cs]
    a_qp = [gq[c][:csz] * tril_strict for c in cs]
    a_kp = [(gk[c][:csz] * tril_strict).astype(bf16) for c in cs]
    a_qr = [(gq[c][csz:] * tril_incl).astype(bf16) for c in cs]
    a_kr = [(gk[c][csz:] * tril_incl).astype(bf16) for c in cs]
    tinv = [eye + a_qp[c] for c in cs]
    if csz > 2:
        apb = [a_qp[c].astype(bf16) for c in cs]
        apow = [_dot(apb[c], apb[c]) for c in cs]
        n = 2
        while n < csz:
            last = 2 * n >= csz
            rhs = [apow[c].astype(bf16) for c in cs]
            if last:
                prod = [_dot(tinv[c].astype(bf16), rhs[c]) for c in cs]
                tinv = [tinv[c] + prod[c] for c in cs]
            else:
                prod = [_dot(stack(tinv[c], apow[c]), rhs[c]) for c in cs]
                tinv = [tinv[c] + prod[c][:csz] for c in cs]
                apow = [prod[c][csz:] for c in cs]
            n *= 2
    tb_ = [tinv[c].astype(bf16) for c in cs]
    akpv = [_dot(a_kp[c], vb[c]).astype(bf16) for c in cs]
    phb = [_dot(tb_[c], p0[c]).astype(bf16) for c in cs]
    wvb = [_dot(tb_[c], akpv[c]).astype(bf16) for c in cs]
    rhat = [r0[c] + _dot(a_qr[c], phb[c]) for c in cs]
    yint = [_dot(a_qr[c], wvb[c]) + _dot(a_kr[c], vb[c]) for c in cs]
    mlow = [_dot_tn(phb[c], qh[c]) for c in cs]
    nn = [_dot_tn(wvb[c], qh[c]) + _dot_tn(vb[c], kh[c]) for c in cs]
    for c in cs:
        rhat_ref[sls[c], :] = rhat[c]
        yint_ref[sls[c], :] = yint[c]
        mlow_ref[c] = mlow[c]
        nn_ref[c] = nn[c]
        e_ref[c] = jnp.broadcast_to(e_last[c], (8, RWKV_HEAD))


def rwkv_phase1(r, k, v, p, q, w, g, row0, rows, csz, nck):
    tb = nck * csz
    assert rows % tb == 0 and row0 % tb == 0
    off = row0 // tb
    nchunks = rows // csz
    hm_in = pl.BlockSpec((None, tb, RWKV_HEAD), lambda h, i: (h, i + off, 0))
    hm_out = pl.BlockSpec((None, tb, RWKV_HEAD), lambda h, i: (h, i, 0))
    sq = pl.BlockSpec((None, nck, RWKV_HEAD, RWKV_HEAD), lambda h, i: (h, i, 0, 0))
    ev = pl.BlockSpec((None, nck, 8, RWKV_HEAD), lambda h, i: (h, i, 0, 0))
    return pl.pallas_call(
        functools.partial(_rwkv_phase1_kernel, csz=csz, nck=nck),
        grid=(RWKV_HEADS, rows // tb),
        in_specs=[hm_in] * 7,
        out_specs=[hm_out, hm_out, sq, sq, ev],
        out_shape=[jax.ShapeDtypeStruct((RWKV_HEADS, rows, RWKV_HEAD), f32)] * 2
        + [jax.ShapeDtypeStruct((RWKV_HEADS, nchunks, RWKV_HEAD, RWKV_HEAD), f32)] * 2
        + [jax.ShapeDtypeStruct((RWKV_HEADS, nchunks, 8, RWKV_HEAD), f32)],
        compiler_params=_cparams("parallel", "parallel"),
        name="rwkv_phase1",
    )(r, k, v, p, q, w, g)


def _rwkv_phase2_kernel(s0_ref, rhat_ref, yint_ref, mlow_ref, nn_ref, e_ref, y_ref, sout_ref, s_sc, *, csz, nck):
    ci = pl.program_id(1)
    hs = range(RWKV_HEADS)

    @pl.when(ci == 0)
    def _():
        s_sc[...] = s0_ref[...]

    s = [s_sc[h] for h in hs]
    for c in range(nck):
        sl = slice(c * csz, (c + 1) * csz)
        sb = [s[h].astype(bf16) for h in hs]
        ys = [_dot_nt(rhat_ref[h, sl, :].astype(bf16), sb[h]) for h in hs]
        sm = [_dot(sb[h], mlow_ref[h, c].astype(bf16)) for h in hs]
        for h in hs:
            y_ref[h, sl, :] = ys[h] + yint_ref[h, sl, :]
        s = [s[h] * e_ref[h, c][:1, :] + sm[h] + nn_ref[h, c] for h in hs]
    for h in hs:
        s_sc[h] = s[h]

    @pl.when(ci == pl.num_programs(1) - 1)
    def _():
        sout_ref[...] = s_sc[...]


def rwkv_phase2(s0, rhat, yint, mlow, nn, e, nseq, tlen, csz, layer=None):
    cps = tlen // csz
    nck = min(cps, 4)
    assert cps % nck == 0
    nsteps = cps // nck
    tb = nck * csz
    hm = pl.BlockSpec((RWKV_HEADS, tb, RWKV_HEAD), lambda s, c: (0, s * nsteps + c, 0))
    sq = pl.BlockSpec((RWKV_HEADS, nck, RWKV_HEAD, RWKV_HEAD), lambda s, c: (0, s * nsteps + c, 0, 0))
    ev = pl.BlockSpec((RWKV_HEADS, nck, 8, RWKV_HEAD), lambda s, c: (0, s * nsteps + c, 0, 0))
    st = pl.BlockSpec((None, RWKV_HEADS, RWKV_HEAD, RWKV_HEAD), lambda s, c: (s, 0, 0, 0))
    st_in = st if layer is None else pl.BlockSpec((None, None, RWKV_HEADS, RWKV_HEAD, RWKV_HEAD),
                                                  lambda s, c: (layer, s, 0, 0, 0))
    return pl.pallas_call(
        functools.partial(_rwkv_phase2_kernel, csz=csz, nck=nck),
        grid=(nseq, nsteps),
        in_specs=[st_in, hm, hm, sq, sq, ev],
        out_specs=[hm, st],
        out_shape=[jax.ShapeDtypeStruct((RWKV_HEADS, nseq * tlen, RWKV_HEAD), f32),
                   jax.ShapeDtypeStruct((nseq, RWKV_HEADS, RWKV_HEAD, RWKV_HEAD), f32)],
        scratch_shapes=[pltpu.VMEM((RWKV_HEADS, RWKV_HEAD, RWKV_HEAD), f32)],
        compiler_params=_cparams("parallel", "arbitrary"),
        name="rwkv_phase2",
    )(s0, rhat, yint, mlow, nn, e)


def _rwkv_out_kernel(y_ref, g_ref, bonus_ref, lnw_ref, lnb_ref, bd_ref, o_ref, y_sc):
    for h in range(RWKV_HEADS):
        y_sc[:, h * RWKV_HEAD:(h + 1) * RWKV_HEAD] = y_ref[h]
    y = y_sc[...]
    bd = bd_ref[...]
    mean = _head_sums(y, bd) * (1.0 / RWKV_HEAD)
    yc = y - mean
    var = _head_sums(yc * yc, bd) * (1.0 / RWKV_HEAD)
    yn = yc * lax.rsqrt(var + RWKV_GN_EPS) * lnw_ref[...] + lnb_ref[...]
    o_ref[...] = ((yn + bonus_ref[...]) * g_ref[...]).astype(o_ref.dtype)


def rwkv_out(y_hm, g, bonus, ln_w, ln_b, bd):
    m = g.shape[0]
    tb = _pick(m, (256, 128))
    tok = pl.BlockSpec((tb, RWKV_WIDTH), lambda i: (i, 0))
    vec = pl.BlockSpec((1, RWKV_WIDTH), lambda i: (0, 0))
    return pl.pallas_call(
        _rwkv_out_kernel,
        grid=(m // tb,),
        in_specs=[pl.BlockSpec((RWKV_HEADS, tb, RWKV_HEAD), lambda i: (0, i, 0)), tok, tok, vec, vec,
                  pl.BlockSpec(bd.shape, lambda i: (0, 0))],
        out_specs=tok,
        out_shape=jax.ShapeDtypeStruct((m, RWKV_WIDTH), bf16),
        scratch_shapes=[pltpu.VMEM((tb, RWKV_WIDTH), f32)],
        compiler_params=_cparams("parallel"),
        name="rwkv_out",
    )(y_hm, g, bonus, ln_w.reshape(1, -1), ln_b.reshape(1, -1), bd)


def _mlstm_kernel(u_ref, ug_ref, gb_ref, ng_ref, c0_ref, n0_ref, m0_ref, y_ref, cout_ref, nout_ref, mout_ref,
                  c_sc, n_sc, m_sc, *, csz):
    ci = pl.program_id(1)
    dh = MLSTM_DH
    w_ = MLSTM_WIDTH

    @pl.when(ci == 0)
    def _():
        c_sc[...] = c0_ref[...]
        n_sc[...] = n0_ref[...]
        m_sc[...] = m0_ref[...]

    tril_incl, _, _ = _tri_consts(csz)
    causal = tril_incl > 0.0
    gates = ug_ref[...] + gb_ref[...]
    lane = lax.broadcasted_iota(jnp.int32, gates.shape, 1)
    logf = jax.nn.log_sigmoid(gates)
    fcum = _dot_exact_rhs(tril_incl.astype(bf16), logf)
    cols = jnp.where(lane < MLSTM_HEADS, gates, fcum)
    sel = (lax.broadcasted_iota(jnp.int32, (8, LANES), 0) == lax.broadcasted_iota(jnp.int32, (8, LANES), 1))
    hi, mid, lo = _split3(cols)
    selb = sel.astype(bf16)
    rows = _dot_nt(selb, hi) + _dot_nt(selb, mid) + _dot_nt(selb, lo)
    for h in range(MLSTM_HEADS):
        q = u_ref[:, h * dh:(h + 1) * dh]
        k = u_ref[:, w_ + h * dh:w_ + (h + 1) * dh] * (dh ** -0.5)
        v = u_ref[:, 2 * w_ + h * dh:2 * w_ + (h + 1) * dh]
        og = jax.nn.sigmoid(u_ref[:, 3 * w_ + h * dh:3 * w_ + (h + 1) * dh])
        ig_row = rows[h:h + 1, :]
        f_row = rows[MLSTM_HEADS + h:MLSTM_HEADS + h + 1, :]
        ig_col = cols[:, h:h + 1]
        f_col = cols[:, MLSTM_HEADS + h:MLSTM_HEADS + h + 1]
        m_old = m_sc[h:h + 1, :1]
        d_ts = jnp.where(causal, f_col - f_row + ig_row, -jnp.inf)
        inter = m_old + f_col
        m_t = jnp.maximum(inter, jnp.max(d_ts, axis=1, keepdims=True))
        w_inter = jnp.exp(inter - m_t)
        qb = q.astype(bf16)
        kb = k.astype(bf16)
        vb = v.astype(bf16)
        a = jnp.exp(d_ts - m_t) * _dot_nt(qb, kb)
        c_old = c_sc[h]
        n_old = n_sc[h:h + 1, :]
        num = w_inter * _dot_nt(qb, c_old.astype(bf16)) + _dot(a.astype(bf16), vb)
        den = w_inter * jnp.sum(q * n_old, axis=1, keepdims=True) + jnp.sum(a, axis=1, keepdims=True)
        hh = num / jnp.maximum(jnp.abs(den), jnp.exp(-m_t))
        hh = hh * lax.rsqrt(jnp.mean(hh * hh, axis=1, keepdims=True) + NORM_EPS)
        y_ref[:, h * dh:(h + 1) * dh] = (og * hh * ng_ref[:, h * dh:(h + 1) * dh]).astype(y_ref.dtype)
        m_new = m_t[csz - 1:csz, :]
        f_last = f_col[csz - 1:csz, :]
        carry = jnp.exp(m_old + f_last - m_new)
        w_write = jnp.exp(f_last - f_col + ig_col - m_new)
        c_sc[h] = carry * c_old + _dot_tn((v * w_write).astype(bf16), kb)
        n_sc[h:h + 1, :] = carry * n_old + jnp.sum(k * w_write, axis=0, keepdims=True)
        m_sc[h:h + 1, :] = jnp.broadcast_to(m_new, (1, LANES))

    @pl.when(ci == pl.num_programs(1) - 1)
    def _():
        cout_ref[...] = c_sc[...]
        nout_ref[...] = n_sc[...]
        mout_ref[...] = m_sc[...]


def mlstm(u, ug, gate_bias_row, norm_g, c0, n0, m0, row0, nseq, tlen, csz, layer=None):
    cps = tlen // csz
    assert row0 % csz == 0
    off = row0 // csz

    def rowspec(cols):
        return pl.BlockSpec((csz, cols), lambda s, c: (off + s * cps + c, 0))

    def state(shape):
        return pl.BlockSpec((None,) + shape, lambda s, c: (s,) + (0,) * len(shape))

    c_shape = (MLSTM_HEADS, MLSTM_DH, MLSTM_DH)
    c_in = state(c_shape) if layer is None else pl.BlockSpec((None, None) + c_shape, lambda s, c: (layer, s, 0, 0, 0))
    return pl.pallas_call(
        functools.partial(_mlstm_kernel, csz=csz),
        grid=(nseq, cps),
        in_specs=[rowspec(4 * MLSTM_WIDTH), rowspec(LANES), pl.BlockSpec((1, LANES), lambda s, c: (0, 0)),
                  pl.BlockSpec((1, MLSTM_WIDTH), lambda s, c: (0, 0)),
                  c_in, state((8, MLSTM_DH)), state((8, LANES))],
        out_specs=[pl.BlockSpec((csz, MLSTM_WIDTH), lambda s, c: (s * cps + c, 0)),
                   state((MLSTM_HEADS, MLSTM_DH, MLSTM_DH)), state((8, MLSTM_DH)), state((8, LANES))],
        out_shape=[jax.ShapeDtypeStruct((nseq * tlen, MLSTM_WIDTH), f32),
                   jax.ShapeDtypeStruct((nseq, MLSTM_HEADS, MLSTM_DH, MLSTM_DH), f32),
                   jax.ShapeDtypeStruct((nseq, 8, MLSTM_DH), f32), jax.ShapeDtypeStruct((nseq, 8, LANES), f32)],
        scratch_shapes=[pltpu.VMEM((MLSTM_HEADS, MLSTM_DH, MLSTM_DH), f32), pltpu.VMEM((8, MLSTM_DH), f32),
                        pltpu.VMEM((8, LANES), f32)],
        compiler_params=_cparams("parallel", "arbitrary"),
        name="mlstm",
    )(u, ug, gate_bias_row, norm_g.reshape(1, -1), c0, n0, m0)


def _column_plan():
    per_head = MLA_NOPE + MLA_ROPE
    q_nope = [h * per_head + d for h in range(MLA_HEADS) for d in range(MLA_NOPE)]
    q_rope = []
    for h in range(MLA_HEADS):
        base = h * per_head + MLA_NOPE
        x1 = [base + e for e in range(ROPE_HALF)]
        x2 = [base + ROPE_HALF + e for e in range(ROPE_HALF)]
        q_rope += x1 + x2 + x2 + x1
    kv_base = MLA_Q_COLS
    ckv = [kv_base + c for c in range(KV_RANK)]
    x1 = [kv_base + KV_RANK + e for e in range(ROPE_HALF)]
    x2 = [kv_base + KV_RANK + ROPE_HALF + e for e in range(ROPE_HALF)]
    return np.asarray(q_nope + q_rope, np.int32), np.asarray(ckv + x1 + x2 + x2 + x1, np.int32)


def _rope_tables():
    pos = jnp.concatenate([jnp.tile(jnp.arange(SEQ), BATCH), jnp.tile(PAST_LEN + jnp.arange(DEC_SEQ), DEC_BATCH)])
    inv = ROPE_THETA ** (-jnp.arange(ROPE_HALF, dtype=f32) / ROPE_HALF)
    ang = pos.astype(f32)[:, None] * inv[None, :]
    cos, sin = jnp.cos(ang), jnp.sin(ang)
    zero = jnp.zeros_like(cos)
    return jnp.concatenate([cos, cos, zero, zero], axis=1), jnp.concatenate([-sin, sin, zero, zero], axis=1)


def _pad8(x, rows_axis):
    pad = [(0, 0)] * x.ndim
    pad[rows_axis] = (0, 8 - x.shape[rows_axis])
    return jnp.pad(x, pad)


def kernel(x_prompt, x_sample, cache_ckv, cache_kpe, page_table, state_rwkv_shift, state_rwkv_S, state_mlstm_C, state_mlstm_n, state_mlstm_m, norm_gains, w_in, g_ckv, w_uk, w_uv, rwkv_mu, rwkv_w0, rwkv_w2, rwkv_a0, rwkv_a2, rwkv_g2, rwkv_kk, rwkv_ka, rwkv_rk, rwkv_ln_w, rwkv_ln_b, mlstm_gate_b, mlstm_norm, gate_b, w_br_mla, w_br_rwkv, w_br_mlstm, w_out, w_up, w_down):
    d = D_MODEL
    o0 = MLA_Q_COLS
    o1 = o0 + MLA_KV_COLS
    o2 = o1 + RWKV_COLS
    o3 = o2 + MLSTM_COLS
    q_cols, kv_cols = _column_plan()
    c4, s4 = _rope_tables()
    head_id = np.arange(RWKV_WIDTH) // RWKV_HEAD
    bd = jnp.asarray(head_id[:, None] == head_id[None, :], bf16)
    zeros_s = jnp.zeros((BATCH, RWKV_HEADS, RWKV_HEAD, RWKV_HEAD), f32)
    zeros_c = jnp.zeros((BATCH, MLSTM_HEADS, MLSTM_DH, MLSTM_DH), f32)
    zeros_n = jnp.zeros((BATCH, 8, MLSTM_DH), f32)
    zeros_m = jnp.zeros((BATCH, 8, LANES), f32)
    tb_rw = 256
    n_starts = tb_rw // DEC_SEQ
    assert SEQ % tb_rw == 0 and M_SAMPLE % tb_rw == 0
    cache_kpe_t = jnp.swapaxes(cache_kpe, 2, 3)

    x = jnp.concatenate([x_prompt.reshape(M_PROMPT, d), x_sample.reshape(M_SAMPLE, d)], axis=0)
    h, h_lo = rmsnorm_bf16(x, norm_gains[0, 0])
    prompt_states, sample_states = [], []
    for l in range(DEPTH):
        w = w_in[l]
        w_q = jnp.take(w, q_cols, axis=1).astype(bf16)
        w_kv = jnp.take(w, kv_cols, axis=1).astype(bf16)
        w_rw = w[:, o1:o2].astype(bf16)
        w_ml = w[:, o2:o2 + 4 * MLSTM_WIDTH].astype(bf16)
        w_mg = jnp.pad(w[:, o2 + 4 * MLSTM_WIDTH:o3], ((0, 0), (0, LANES - 2 * MLSTM_HEADS)))
        w_gt = w[:, o3:].astype(bf16)

        u_q = matmul(h, w_q)
        u_kv = matmul(h, w_kv)
        u_rw = matmul(h, w_rw)
        u_ml = matmul(h, w_ml)
        u_mg = matmul_split(h, h_lo, w_mg)
        gates = matmul(h, w_gt, out_dtype=bf16, bias=gate_b[l].reshape(-1))

        w_uk_t = jnp.transpose(w_uk[l], (1, 2, 0)).astype(bf16)
        w_uv_h = jnp.transpose(w_uv[l], (1, 0, 2)).astype(bf16)
        ql_p, qp_p, kc_p, kpe_p, ckv_p, kpef_p = mla_prep(u_q, u_kv, c4, s4, w_uk_t, g_ckv[l], 0, M_PROMPT, bf16)
        ql_s, qp_s, _, _, ckv_s, kpef_s = mla_prep(u_q, u_kv, c4, s4, w_uk_t, g_ckv[l], M_PROMPT, M_SAMPLE, f32)
        y_mla_p = attn_prompt(ql_p, qp_p, kc_p, kpe_p, w_uv_h)
        y_mla_s = attn_sample(ql_s, qp_s, ckv_s, kpef_s, w_uv_h, cache_ckv, cache_kpe_t, page_table, l)
        y_mla = jnp.concatenate([y_mla_p, y_mla_s.astype(bf16)], axis=0)

        block_last = u_rw[tb_rw - 1:M_PROMPT:tb_rw].reshape(BATCH, SEQ // tb_rw, 1, RWKV_COLS)
        before = jnp.concatenate([jnp.zeros((BATCH, 1, 1, RWKV_COLS), f32), block_last[:, :-1]], axis=1)
        starts = jnp.concatenate([
            jnp.pad(before.reshape(M_PROMPT // tb_rw, 1, RWKV_COLS), ((0, 0), (0, n_starts - 1), (0, 0))),
            state_rwkv_shift[l].reshape(M_SAMPLE // tb_rw, n_starts, RWKV_COLS)], axis=0)
        zero_blk = jnp.zeros((W_LORA, RWKV_WIDTH), f32)
        w_wa = jnp.concatenate([jnp.concatenate([rwkv_w2[l], zero_blk], axis=1),
                                jnp.concatenate([zero_blk, rwkv_a2[l]], axis=1)], axis=0).astype(bf16)
        lp_rows = [rwkv_mu[l].reshape(1, -1), rwkv_w0[l].reshape(1, -1), rwkv_a0[l].reshape(1, -1),
                   rwkv_kk[l].reshape(1, -1), rwkv_ka[l].reshape(1, -1), rwkv_rk[l].reshape(1, -1)]
        ops = rwkv_prep(u_rw, starts, lp_rows, w_wa, rwkv_g2[l].astype(bf16), bd, tb_rw)
        hm_ops, g_, bonus = ops[:7], ops[7], ops[8]
        t_p = rwkv_phase1(*hm_ops, 0, M_PROMPT, RWKV_CHUNK, 16)
        t_s = rwkv_phase1(*hm_ops, M_PROMPT, M_SAMPLE, DEC_SEQ, 32)
        y_p, s_p = rwkv_phase2(zeros_s, *t_p, BATCH, SEQ, RWKV_CHUNK)
        y_s, s_s = rwkv_phase2(state_rwkv_S, *t_s, DEC_BATCH, DEC_SEQ, DEC_SEQ, layer=l)
        y_rwkv = rwkv_out(jnp.concatenate([y_p, y_s], axis=1), g_, bonus, rwkv_ln_w[l], rwkv_ln_b[l], bd)

        gb_row = jnp.pad(mlstm_gate_b[l].reshape(1, -1), ((0, 0), (0, LANES - 2 * MLSTM_HEADS)))
        ym_p, c_p, n_p, m_p = mlstm(u_ml, u_mg, gb_row, mlstm_norm[l], zeros_c, zeros_n, zeros_m, 0, BATCH, SEQ, MLSTM_CHUNK)
        m0_s = jnp.broadcast_to(_pad8(state_mlstm_m[l], 1)[:, :, None], (DEC_BATCH, 8, LANES))
        ym_s, c_s, n_s, m_s = mlstm(u_ml, u_mg, gb_row, mlstm_norm[l], state_mlstm_C, _pad8(state_mlstm_n[l], 1), m0_s,
                                    M_PROMPT, DEC_BATCH, DEC_SEQ, DEC_SEQ, layer=l)
        y_mlstm = jnp.concatenate([ym_p, ym_s], axis=0).astype(bf16)

        merged = merge_branches(y_mla, y_rwkv, y_mlstm, w_br_mla[l].astype(bf16), w_br_rwkv[l].astype(bf16),
                                w_br_mlstm[l].astype(bf16), gates)
        attn_out = matmul(merged, w_out[l].astype(bf16))
        x, hf = resnorm(x, attn_out, norm_gains[l, 1], norm_gains[l, 2], False)
        ff = matmul_ksplit(matmul(hf, w_up[l].astype(bf16), out_dtype=bf16, act="relu2"), w_down[l].astype(bf16))
        g_next = norm_gains[l + 1, 0] if l + 1 < DEPTH else norm_gains[l, 3]
        x, h, h_lo = resnorm(x, ff, norm_gains[l, 3], g_next, True)

        shift_p = u_rw[SEQ - 1:M_PROMPT:SEQ]
        shift_s = u_rw[M_PROMPT + DEC_SEQ - 1::DEC_SEQ]
        prompt_states.append((ckv_p.reshape(BATCH, SEQ, KV_RANK), kpef_p.reshape(BATCH, SEQ, MLA_ROPE), shift_p, s_p,
                              c_p, n_p[:, :MLSTM_HEADS], m_p[:, :MLSTM_HEADS, 0]))
        sample_states.append((ckv_s.reshape(DEC_BATCH, DEC_SEQ, KV_RANK), kpef_s.reshape(DEC_BATCH, DEC_SEQ, MLA_ROPE),
                              shift_s, s_s, c_s, n_s[:, :MLSTM_HEADS], m_s[:, :MLSTM_HEADS, 0]))

    outs = [x[:M_PROMPT].reshape(BATCH, SEQ, d), x[M_PROMPT:].reshape(DEC_BATCH, DEC_SEQ, d)]
    for states in (prompt_states, sample_states):
        for i in range(7):
            outs.append(jnp.stack([st[i] for st in states]))
    return tuple(outs)
```

```python
import functools

import jax
import jax.numpy as jnp
import numpy as np
from jax import lax
from jax.experimental import pallas as pl
from jax.experimental.pallas import tpu as pltpu

D_MODEL = 2048
BATCH = 2
SEQ = 4096
DEPTH = 2
DEC_BATCH = 128
DEC_SEQ = 8
PAST_LEN = 16384
PAGE_SIZE = 128

MLA_HEADS = 8
MLA_NOPE = 128
MLA_ROPE = 64
MLA_V = 128
KV_RANK = 256
ROPE_THETA = 10000.0
MLA_SCALE = (MLA_NOPE + MLA_ROPE) ** -0.5
RWKV_HEAD = 64
RWKV_WIDTH = 1024
RWKV_HEADS = RWKV_WIDTH // RWKV_HEAD
W_LORA = 64
A_LORA = 64
G_LORA = 128
RWKV_GN_EPS = 64e-5
MLSTM_HEADS = 4
MLSTM_DH = 256
MLSTM_WIDTH = MLSTM_HEADS * MLSTM_DH
MLSTM_CHUNK = 64
N_BRANCH = 3
D_FF = 4 * D_MODEL
NORM_EPS = 1e-6

MLA_Q_COLS = MLA_HEADS * (MLA_NOPE + MLA_ROPE)
MLA_KV_COLS = KV_RANK + MLA_ROPE
RWKV_COLS = 3 * RWKV_WIDTH + W_LORA + A_LORA + G_LORA
MLSTM_COLS = 4 * MLSTM_WIDTH + 2 * MLSTM_HEADS
GATE_COLS = N_BRANCH * D_MODEL

LANES = 128
MXU_DIM = 256
ROPE_HALF = MLA_ROPE // 2
QL_COLS = MLA_HEADS * KV_RANK
QP_COLS = MLA_HEADS * LANES
UQ_COLS = MLA_HEADS * MLA_NOPE + QP_COLS
UKV_COLS = KV_RANK + LANES
N_PAGES = PAST_LEN // PAGE_SIZE
PAGES_PER_STEP = 8
SEQS_PER_STEP = 4
RWKV_CHUNK = 64
M_PROMPT = BATCH * SEQ
M_SAMPLE = DEC_BATCH * DEC_SEQ
M_TOK = M_PROMPT + M_SAMPLE
VMEM_LIMIT = 56 * 1024 * 1024

f32 = jnp.float32
bf16 = jnp.bfloat16


def _cparams(*sem):
    return pltpu.CompilerParams(dimension_semantics=sem, vmem_limit_bytes=VMEM_LIMIT)


def _pick(n, cands):
    for c in cands:
        if n % c == 0:
            return c
    raise ValueError(f"no tile for {n} in {cands}")


def _dot(a, b):
    return jnp.dot(a, b, preferred_element_type=f32)


def _dot_nt(a, b):
    return lax.dot_general(a, b, (((1,), (1,)), ((), ())), preferred_element_type=f32)


def _dot_tn(a, b):
    return lax.dot_general(a, b, (((0,), (0,)), ((), ())), preferred_element_type=f32)


def _split3(x):
    hi = x.astype(bf16)
    r1 = x - hi.astype(f32)
    mid = r1.astype(bf16)
    lo = (r1 - mid.astype(f32)).astype(bf16)
    return hi, mid, lo


def _dot_exact_rhs(sel, x):
    hi, mid, lo = _split3(x)
    return _dot(sel, hi) + _dot(sel, mid) + _dot(sel, lo)


def _dot_exact_lhs(x, sel):
    hi, mid, lo = _split3(x)
    return _dot(hi, sel) + _dot(mid, sel) + _dot(lo, sel)


def _rms(x, g):
    return x * lax.rsqrt(jnp.mean(x * x, axis=-1, keepdims=True) + NORM_EPS) * g


def _emit_norm(hn, h_ref, lo_ref):
    hb = hn.astype(bf16)
    h_ref[...] = hb
    if lo_ref is not None:
        lo_ref[...] = (hn - hb.astype(f32)).astype(bf16)


def _norm_kernel(x_ref, g_ref, h_ref, lo_ref):
    _emit_norm(_rms(x_ref[...], g_ref[...]), h_ref, lo_ref)


def rmsnorm_bf16(x, g):
    m, d = x.shape
    tm = _pick(m, (512, 256))
    row = pl.BlockSpec((tm, d), lambda i: (i, 0))
    return pl.pallas_call(
        _norm_kernel,
        grid=(m // tm,),
        in_specs=[row, pl.BlockSpec((1, d), lambda i: (0, 0))],
        out_specs=[row, row],
        out_shape=[jax.ShapeDtypeStruct((m, d), bf16)] * 2,
        compiler_params=_cparams("parallel"),
        name="rmsnorm",
    )(x, g.reshape(1, d))


def _resnorm_kernel(x_ref, y_ref, gp_ref, gn_ref, xo_ref, h_ref, *lo_ref):
    xn = x_ref[...] + _rms(y_ref[...], gp_ref[...])
    xo_ref[...] = xn
    _emit_norm(_rms(xn, gn_ref[...]), h_ref, lo_ref[0] if lo_ref else None)


def resnorm(x, y, g_post, g_next, with_lo):
    m, d = x.shape
    tm = _pick(m, (512, 256))
    row = pl.BlockSpec((tm, d), lambda i: (i, 0))
    gain = pl.BlockSpec((1, d), lambda i: (0, 0))
    n_h = 2 if with_lo else 1
    return pl.pallas_call(
        _resnorm_kernel,
        grid=(m // tm,),
        in_specs=[row, row, gain, gain],
        out_specs=[row] * (1 + n_h),
        out_shape=[jax.ShapeDtypeStruct((m, d), f32)] + [jax.ShapeDtypeStruct((m, d), bf16)] * n_h,
        compiler_params=_cparams("parallel"),
        name="resnorm",
    )(x, y, g_post.reshape(1, d), g_next.reshape(1, d))


def _weights(b_ref):
    b = b_ref[...]
    return b if b.dtype == bf16 else b.astype(bf16)


def _weight_spec(layer, shape, index_map):
    if layer is None:
        return pl.BlockSpec(shape, index_map)
    return pl.BlockSpec((None,) + shape, lambda *g: (layer,) + index_map(*g))


def _mm_kernel(a_ref, b_ref, o_ref, *, act):
    acc = _dot(a_ref[...], _weights(b_ref))
    if act == "relu2":
        acc = jnp.square(jnp.maximum(acc, 0.0))
    o_ref[...] = acc.astype(o_ref.dtype)


def _mm_bias_sigmoid_kernel(a_ref, b_ref, bias_ref, o_ref):
    acc = _dot(a_ref[...], _weights(b_ref)) + bias_ref[...]
    o_ref[...] = jax.nn.sigmoid(acc).astype(o_ref.dtype)


def _mm_tiles(m, n):
    tm = _pick(m, (1024, 512, 256))
    tn = _pick(n, (1024, 512, 384, 256, 128)) if n % 1664 else 1664
    return tm, tn


def matmul(a, b, out_dtype=f32, act=None, bias=None, layer=None):
    m, k = a.shape
    n = b.shape[-1]
    tm, tn = _mm_tiles(m, n)
    in_specs = [pl.BlockSpec((tm, k), lambda i, j: (i, 0)), _weight_spec(layer, (k, tn), lambda i, j: (0, j))]
    args = [a, b]
    if bias is not None:
        kern = _mm_bias_sigmoid_kernel
        in_specs.append(pl.BlockSpec((1, tn), lambda i, j: (0, j)))
        args.append(bias.reshape(1, n))
    else:
        kern = functools.partial(_mm_kernel, act=act)
    return pl.pallas_call(
        kern,
        grid=(m // tm, n // tn),
        in_specs=in_specs,
        out_specs=pl.BlockSpec((tm, tn), lambda i, j: (i, j)),
        out_shape=jax.ShapeDtypeStruct((m, n), out_dtype),
        compiler_params=_cparams("parallel", "parallel"),
        name="matmul",
    )(*args)


def _mm3_kernel(a_ref, alo_ref, b_ref, blo_ref, o_ref):
    a = a_ref[...]
    b = b_ref[...]
    o_ref[...] = _dot(a, b) + _dot(alo_ref[...], b) + _dot(a, blo_ref[...])


def matmul_split(a, a_lo, b):
    m, k = a.shape
    n = b.shape[1]
    b_hi = b.astype(bf16)
    b_lo = (b - b_hi.astype(f32)).astype(bf16)
    tm = _pick(m, (1024, 512, 256))
    row = pl.BlockSpec((tm, k), lambda i: (i, 0))
    col = pl.BlockSpec((k, n), lambda i: (0, 0))
    return pl.pallas_call(
        _mm3_kernel,
        grid=(m // tm,),
        in_specs=[row, row, col, col],
        out_specs=pl.BlockSpec((tm, n), lambda i: (i, 0)),
        out_shape=jax.ShapeDtypeStruct((m, n), f32),
        compiler_params=_cparams("parallel"),
        name="matmul_split",
    )(a, a_lo, b_hi, b_lo)


def _mm_acc_kernel(a_ref, b_ref, o_ref, acc_ref):
    kk = pl.program_id(2)

    @pl.when(kk == 0)
    def _():
        acc_ref[...] = jnp.zeros_like(acc_ref)

    acc_ref[...] += _dot(a_ref[...], _weights(b_ref))

    @pl.when(kk == pl.num_programs(2) - 1)
    def _():
        o_ref[...] = acc_ref[...]


def matmul_ksplit(a, b, layer=None):
    m, k = a.shape
    n = b.shape[-1]
    tm = _pick(m, (512, 256))
    tn = _pick(n, (1024, 512, 256))
    tk = _pick(k, (2048, 1024, 512, 256))
    return pl.pallas_call(
        _mm_acc_kernel,
        grid=(m // tm, n // tn, k // tk),
        in_specs=[pl.BlockSpec((tm, tk), lambda i, j, kk: (i, kk)),
                  _weight_spec(layer, (tk, tn), lambda i, j, kk: (kk, j))],
        out_specs=pl.BlockSpec((tm, tn), lambda i, j, kk: (i, j)),
        out_shape=jax.ShapeDtypeStruct((m, n), f32),
        scratch_shapes=[pltpu.VMEM((tm, tn), f32)],
        compiler_params=_cparams("parallel", "parallel", "arbitrary"),
        name="matmul_ksplit",
    )(a, b)


def _merge_kernel(ya_ref, yr_ref, ym_ref, wa_ref, wr_ref, wm_ref, ga_ref, gr_ref, gm_ref, o_ref):
    acc = ga_ref[...].astype(f32) * _dot(ya_ref[...], _weights(wa_ref))
    acc += gr_ref[...].astype(f32) * _dot(yr_ref[...], _weights(wr_ref))
    acc += gm_ref[...].astype(f32) * _dot(ym_ref[...], _weights(wm_ref))
    o_ref[...] = acc.astype(o_ref.dtype)


def merge_branches(y_mla, y_rwkv, y_mlstm, w_mla, w_rwkv, w_mlstm, gates, layer):
    m = y_mla.shape[0]
    d = w_mla.shape[-1]
    tm = _pick(m, (1024, 512, 256))
    tn = _pick(d, (512, 256))
    nj = d // tn
    ys = [pl.BlockSpec((tm, y.shape[1]), lambda i, j: (i, 0)) for y in (y_mla, y_rwkv, y_mlstm)]
    ws = [_weight_spec(layer, (w.shape[1], tn), lambda i, j: (0, j)) for w in (w_mla, w_rwkv, w_mlstm)]
    gs = [pl.BlockSpec((tm, tn), functools.partial(lambda i, j, b: (i, j + b * nj), b=b)) for b in range(N_BRANCH)]
    return pl.pallas_call(
        _merge_kernel,
        grid=(m // tm, nj),
        in_specs=ys + ws + gs,
        out_specs=pl.BlockSpec((tm, tn), lambda i, j: (i, j)),
        out_shape=jax.ShapeDtypeStruct((m, d), bf16),
        compiler_params=_cparams("parallel", "parallel"),
        name="merge_branches",
    )(y_mla, y_rwkv, y_mlstm, w_mla, w_rwkv, w_mlstm, gates, gates, gates)


def _rope_group(p, c4, s4):
    return p * c4 + pltpu.roll(p, 2 * ROPE_HALF, axis=1) * s4


def _mla_prep_kernel(uq_ref, ukv_ref, c4_ref, s4_ref, wuk_ref, g_ref, ql_ref, qp_ref, kc_ref, kpe_ref, ckv_ref, kpef_ref):
    c4 = c4_ref[...]
    s4 = s4_ref[...]
    nope = MLA_HEADS * MLA_NOPE
    for h in range(MLA_HEADS):
        qn = uq_ref[:, h * MLA_NOPE:(h + 1) * MLA_NOPE].astype(bf16)
        ql = _dot(qn, wuk_ref[h]) * MLA_SCALE
        ql_ref[:, h * KV_RANK:(h + 1) * KV_RANK] = ql.astype(ql_ref.dtype)
        qp = _rope_group(uq_ref[:, nope + h * LANES:nope + (h + 1) * LANES], c4, s4) * MLA_SCALE
        qp_ref[:, h * LANES:(h + 1) * LANES] = qp.astype(qp_ref.dtype)
    ckv = _rms(ukv_ref[:, :KV_RANK], g_ref[...])
    ckv_ref[...] = ckv
    kc_ref[...] = ckv.astype(bf16)
    kpe = _rope_group(ukv_ref[:, KV_RANK:], c4, s4)
    kpe_ref[...] = kpe.astype(bf16)
    kpef_ref[...] = kpe[:, :MLA_ROPE]


def mla_prep(u_q, u_kv, c4, s4, w_uk_t, g_ckv, row0, rows, q_dtype):
    tb = _pick(rows, (256, 128))
    assert row0 % tb == 0
    off = row0 // tb

    def rowspec(cols):
        return pl.BlockSpec((tb, cols), lambda i: (i + off, 0))

    def outspec(cols):
        return pl.BlockSpec((tb, cols), lambda i: (i, 0))

    return pl.pallas_call(
        _mla_prep_kernel,
        grid=(rows // tb,),
        in_specs=[rowspec(UQ_COLS), rowspec(UKV_COLS), rowspec(LANES), rowspec(LANES),
                  pl.BlockSpec((MLA_HEADS, MLA_NOPE, KV_RANK), lambda i: (0, 0, 0)),
                  pl.BlockSpec((1, KV_RANK), lambda i: (0, 0))],
        out_specs=[outspec(QL_COLS), outspec(QP_COLS), outspec(KV_RANK), outspec(LANES), outspec(KV_RANK), outspec(MLA_ROPE)],
        out_shape=[jax.ShapeDtypeStruct((rows, QL_COLS), q_dtype), jax.ShapeDtypeStruct((rows, QP_COLS), q_dtype),
                   jax.ShapeDtypeStruct((rows, KV_RANK), bf16), jax.ShapeDtypeStruct((rows, LANES), bf16),
                   jax.ShapeDtypeStruct((rows, KV_RANK), f32), jax.ShapeDtypeStruct((rows, MLA_ROPE), f32)],
        compiler_params=_cparams("parallel"),
        name="mla_prep",
    )(u_q, u_kv, c4, s4, w_uk_t, g_ckv.reshape(1, KV_RANK))


def _attn_prompt_kernel(ql_ref, qp_ref, kc_ref, kpe_ref, wuv_ref, o_ref, m_sc, l_sc, acc_sc, *, tq, tk):
    qi = pl.program_id(1)
    ki = pl.program_id(2)

    @pl.when(ki == 0)
    def _():
        m_sc[...] = jnp.full_like(m_sc, -jnp.inf)
        l_sc[...] = jnp.zeros_like(l_sc)
        acc_sc[...] = jnp.zeros_like(acc_sc)

    @pl.when(ki <= qi)
    def _():
        kc = kc_ref[...]
        kpe = kpe_ref[...]
        rel = lax.broadcasted_iota(jnp.int32, (tq, tk), 1) - lax.broadcasted_iota(jnp.int32, (tq, tk), 0)
        visible = rel <= (qi * tq - ki * tk)
        for h in range(MLA_HEADS):
            s = _dot_nt(ql_ref[:, h * KV_RANK:(h + 1) * KV_RANK], kc)
            s += _dot_nt(qp_ref[:, h * LANES:(h + 1) * LANES], kpe)
            s = jnp.where(visible, s, -jnp.inf)
            m_prev = m_sc[h]
            m_new = jnp.maximum(m_prev, jnp.max(s, axis=1, keepdims=True))
            alpha = jnp.exp(m_prev - m_new)
            p = jnp.exp(s - m_new[:, :1])
            l_sc[h] = alpha * l_sc[h] + jnp.sum(p, axis=1, keepdims=True)
            acc_sc[h] = acc_sc[h] * alpha[:, :1] + _dot(p.astype(bf16), kc)
            m_sc[h] = m_new

    @pl.when(ki == qi)
    def _():
        for h in range(MLA_HEADS):
            o = acc_sc[h] / l_sc[h][:, :1]
            o_ref[:, h * MLA_V:(h + 1) * MLA_V] = _dot(o.astype(bf16), wuv_ref[h]).astype(o_ref.dtype)


def attn_prompt(ql, qp, kc, kpe, w_uv_h, tq):
    tk = tq
    nq = SEQ // tq

    def qmap(b, qi, ki):
        return (b * nq + qi, 0)

    def kmap(b, qi, ki):
        return (b * nq + jnp.minimum(ki, qi), 0)

    return pl.pallas_call(
        functools.partial(_attn_prompt_kernel, tq=tq, tk=tk),
        grid=(BATCH, nq, nq),
        in_specs=[pl.BlockSpec((tq, QL_COLS), qmap), pl.BlockSpec((tq, QP_COLS), qmap),
                  pl.BlockSpec((tk, KV_RANK), kmap), pl.BlockSpec((tk, LANES), kmap),
                  pl.BlockSpec((MLA_HEADS, KV_RANK, MLA_V), lambda b, qi, ki: (0, 0, 0))],
        out_specs=pl.BlockSpec((tq, MLA_HEADS * MLA_V), qmap),
        out_shape=jax.ShapeDtypeStruct((M_PROMPT, MLA_HEADS * MLA_V), bf16),
        scratch_shapes=[pltpu.VMEM((MLA_HEADS, tq, LANES), f32), pltpu.VMEM((MLA_HEADS, tq, LANES), f32),
                        pltpu.VMEM((MLA_HEADS, tq, KV_RANK), f32)],
        compiler_params=_cparams("parallel", "parallel", "arbitrary"),
        name="attn_prompt",
    )(ql, qp, kc, kpe, w_uv_h)


def _attn_sample_kernel(pt_ref, ql_ref, qp_ref, cn_ref, pn_ref, wuv_ref, *rest):
    del pt_ref
    pps = PAGES_PER_STEP
    nsq = SEQS_PER_STEP
    ckv_pages = rest[:nsq * pps]
    kpe_pages = rest[nsq * pps:2 * nsq * pps]
    o_ref, q_sc, qp_sc, kc_sc, kp_sc, m_sc, l_sc, acc_sc = rest[2 * nsq * pps:]
    j = pl.program_id(1)
    rows = MLA_HEADS * DEC_SEQ
    seqs = range(nsq)

    @pl.when(j == 0)
    def _():
        for r in seqs:
            tok = slice(r * DEC_SEQ, (r + 1) * DEC_SEQ)
            for h in range(MLA_HEADS):
                q_sc[r, h * DEC_SEQ:(h + 1) * DEC_SEQ, :] = ql_ref[tok, h * KV_RANK:(h + 1) * KV_RANK]
                qp_sc[r, h * DEC_SEQ:(h + 1) * DEC_SEQ, :] = qp_ref[tok, h * LANES:(h + 1) * LANES]
        m_sc[...] = jnp.full_like(m_sc, -jnp.inf)
        l_sc[...] = jnp.zeros_like(l_sc)
        acc_sc[...] = jnp.zeros_like(acc_sc)

    qb = [q_sc[r].astype(bf16) for r in seqs]
    qpb = [qp_sc[r][:, :MLA_ROPE].astype(bf16) for r in seqs]

    def online(r, s, vals):
        m_prev = m_sc[r]
        m_new = jnp.maximum(m_prev, jnp.max(s, axis=1, keepdims=True))
        alpha = jnp.exp(m_prev - m_new)
        p = jnp.exp(s - m_new[:, :1])
        l_sc[r] = alpha * l_sc[r] + jnp.sum(p, axis=1, keepdims=True)
        acc_sc[r] = acc_sc[r] * alpha[:, :1] + _dot(p.astype(bf16), vals)
        m_sc[r] = m_new

    for r in seqs:
        for i in range(pps):
            kc_sc[r, i * PAGE_SIZE:(i + 1) * PAGE_SIZE, :] = ckv_pages[r * pps + i][...].astype(bf16)
            kp_sc[r, :, i * PAGE_SIZE:(i + 1) * PAGE_SIZE] = kpe_pages[r * pps + i][...].astype(bf16)
    kcs = [kc_sc[r] for r in seqs]
    scores = [_dot_nt(qb[r], kcs[r]) + _dot(qpb[r], kp_sc[r]) for r in seqs]
    for r in seqs:
        online(r, scores[r], kcs[r])

    @pl.when(j == pl.num_programs(1) - 1)
    def _():
        pad = jnp.zeros((LANES - DEC_SEQ, KV_RANK), f32)
        key = lax.broadcasted_iota(jnp.int32, (rows, LANES), 1)
        tokid = lax.broadcasted_iota(jnp.int32, (rows, LANES), 0) % DEC_SEQ
        for r in seqs:
            tok = slice(r * DEC_SEQ, (r + 1) * DEC_SEQ)
            cn = jnp.concatenate([cn_ref[tok, :], pad], axis=0).astype(bf16)
            pn = jnp.concatenate([pn_ref[tok, :], pad[:, :MLA_ROPE]], axis=0).astype(bf16)
            s = _dot_nt(qb[r], cn) + _dot_nt(qpb[r], pn)
            online(r, jnp.where(key <= tokid, s, -jnp.inf), cn)
            o = acc_sc[r] / l_sc[r][:, :1]
            for h in range(MLA_HEADS):
                oh = o[h * DEC_SEQ:(h + 1) * DEC_SEQ, :].astype(bf16)
                o_ref[tok, h * MLA_V:(h + 1) * MLA_V] = _dot(oh, wuv_ref[h])


def attn_sample(ql, qp, c_new, p_new, w_uv_h, cache_ckv, cache_kpe_t, page_table, layer):
    pps = PAGES_PER_STEP
    nsq = SEQS_PER_STEP
    assert N_PAGES % pps == 0 and DEC_BATCH % nsq == 0
    nchunk = N_PAGES // pps
    rows = MLA_HEADS * DEC_SEQ
    keys = pps * PAGE_SIZE

    def seqspec(cols):
        return pl.BlockSpec((nsq * DEC_SEQ, cols), lambda b, j, pt: (b, 0))

    def page_spec(shape, r, i):
        return pl.BlockSpec((None, None) + shape, lambda b, j, pt: (layer, pt[b * nsq + r, j * pps + i], 0, 0))

    in_specs = [seqspec(QL_COLS), seqspec(QP_COLS), seqspec(KV_RANK), seqspec(MLA_ROPE),
                pl.BlockSpec((MLA_HEADS, KV_RANK, MLA_V), lambda b, j, pt: (0, 0, 0))]
    in_specs += [page_spec((PAGE_SIZE, KV_RANK), r, i) for r in range(nsq) for i in range(pps)]
    in_specs += [page_spec((MLA_ROPE, PAGE_SIZE), r, i) for r in range(nsq) for i in range(pps)]
    grid_spec = pltpu.PrefetchScalarGridSpec(
        num_scalar_prefetch=1,
        grid=(DEC_BATCH // nsq, nchunk),
        in_specs=in_specs,
        out_specs=pl.BlockSpec((nsq * DEC_SEQ, MLA_HEADS * MLA_V), lambda b, j, pt: (b, 0)),
        scratch_shapes=[pltpu.VMEM((nsq, rows, KV_RANK), f32), pltpu.VMEM((nsq, rows, LANES), f32),
                        pltpu.VMEM((nsq, keys, KV_RANK), bf16), pltpu.VMEM((nsq, MLA_ROPE, keys), bf16),
                        pltpu.VMEM((nsq, rows, LANES), f32), pltpu.VMEM((nsq, rows, LANES), f32),
                        pltpu.VMEM((nsq, rows, KV_RANK), f32)],
    )
    n_pg = nsq * pps
    return pl.pallas_call(
        _attn_sample_kernel,
        grid_spec=grid_spec,
        out_shape=jax.ShapeDtypeStruct((M_SAMPLE, MLA_HEADS * MLA_V), f32),
        compiler_params=_cparams("parallel", "arbitrary"),
        name="attn_sample",
    )(page_table, ql, qp, c_new, p_new, w_uv_h, *([cache_ckv] * n_pg), *([cache_kpe_t] * n_pg))


def _head_sums(x, bd):
    hi = x.astype(bf16)
    lo = (x - hi.astype(f32)).astype(bf16)
    parts = []
    for g in range(x.shape[1] // MXU_DIM):
        sl = slice(g * MXU_DIM, (g + 1) * MXU_DIM)
        parts.append(_dot(hi[:, sl], bd) + _dot(lo[:, sl], bd))
    return jnp.concatenate(parts, axis=1)


def _rwkv_prep_kernel(u_ref, st_ref, mu_ref, w0_ref, a0_ref, kkw_ref, ka_ref, rk_ref, wwa_ref, g2_ref, bd_ref,
                      r_ref, k_ref, v_ref, p_ref, q_ref, w_ref, gc_ref, g_ref, bonus_ref, *, tb, n_prompt_blocks):
    w_ = RWKV_WIDTH
    nst = st_ref.shape[0]
    is_sample = pl.program_id(0) >= n_prompt_blocks
    stride = jnp.where(is_sample, DEC_SEQ, tb)
    row_e = lax.broadcasted_iota(jnp.int32, (tb, nst), 0)
    col_e = lax.broadcasted_iota(jnp.int32, (tb, nst), 1)
    place = jnp.where(row_e == col_e * stride, 1.0, 0.0).astype(bf16)
    starts = _dot_exact_rhs(place, st_ref[...])
    u = u_ref[...]
    row = lax.broadcasted_iota(jnp.int32, (tb, 1), 0)
    is_start = jnp.where(is_sample, row % DEC_SEQ, row) == 0
    prev = jnp.where(is_start, starts, pltpu.roll(u, 1, axis=0))
    z = u + (prev - u) * mu_ref[...]
    r = z[:, :w_]
    k = z[:, w_:2 * w_]
    v = z[:, 2 * w_:3 * w_]
    wa = z[:, 3 * w_:3 * w_ + W_LORA + A_LORA]
    gd = z[:, 3 * w_ + W_LORA + A_LORA:]
    lane = lax.broadcasted_iota(jnp.int32, wa.shape, 1)
    wa = jnp.where(lane < W_LORA, jnp.tanh(wa), wa)
    lora = _dot(wa.astype(bf16), wwa_ref[...])
    w_log = -jax.nn.softplus(-(w0_ref[...] + lora[:, :w_])) - 0.5
    a = jax.nn.sigmoid(a0_ref[...] + lora[:, w_:])
    g = _dot(jax.nn.sigmoid(gd).astype(bf16), g2_ref[...])
    bd = bd_ref[...]
    kk = k * kkw_ref[...]
    kk = kk / jnp.maximum(jnp.sqrt(_head_sums(kk * kk, bd)), 1e-12)
    k = k * (1.0 + (a - 1.0) * ka_ref[...])
    bonus = _head_sums(r * k * rk_ref[...], bd) * v
    g_ref[...] = g
    bonus_ref[...] = bonus
    wdec = -jnp.exp(w_log)
    shift = jnp.where(is_sample, DEC_SEQ.bit_length() - 1, RWKV_CHUNK.bit_length() - 1)
    row_c = lax.broadcasted_iota(jnp.int32, (tb, tb), 0)
    col_c = lax.broadcasted_iota(jnp.int32, (tb, tb), 1)
    same_chunk = lax.shift_right_logical(row_c, shift) == lax.shift_right_logical(col_c, shift)
    tril_blk = jnp.where(same_chunk, jnp.where(col_c <= row_c, 1.0, 0.0), 0.0).astype(bf16)
    gcum = _dot_exact_rhs(tril_blk, wdec)
    for h in range(RWKV_HEADS):
        sl = slice(h * RWKV_HEAD, (h + 1) * RWKV_HEAD)
        r_ref[h] = r[:, sl]
        k_ref[h] = k[:, sl]
        v_ref[h] = v[:, sl]
        p_ref[h] = -kk[:, sl]
        q_ref[h] = (kk * a)[:, sl]
        w_ref[h] = wdec[:, sl]
        gc_ref[h] = gcum[:, sl]


def rwkv_prep(u, starts, lp_rows, w_wa, g2, bd, tb):
    m = u.shape[0]
    nst = starts.shape[1]
    rowc = pl.BlockSpec((tb, RWKV_COLS), lambda i: (i, 0))

    def vec(n):
        return pl.BlockSpec((1, n), lambda i: (0, 0))

    def full(a):
        return pl.BlockSpec(a.shape, lambda i: (0, 0))

    headmajor = pl.BlockSpec((RWKV_HEADS, tb, RWKV_HEAD), lambda i: (0, i, 0))
    tokmajor = pl.BlockSpec((tb, RWKV_WIDTH), lambda i: (i, 0))
    hm_shape = jax.ShapeDtypeStruct((RWKV_HEADS, m, RWKV_HEAD), f32)
    tm_shape = jax.ShapeDtypeStruct((m, RWKV_WIDTH), f32)
    return pl.pallas_call(
        functools.partial(_rwkv_prep_kernel, tb=tb, n_prompt_blocks=M_PROMPT // tb),
        grid=(m // tb,),
        in_specs=[rowc, pl.BlockSpec((None, nst, RWKV_COLS), lambda i: (i, 0, 0)), vec(RWKV_COLS)]
        + [vec(RWKV_WIDTH)] * 5 + [full(w_wa), full(g2), full(bd)],
        out_specs=[headmajor] * 7 + [tokmajor] * 2,
        out_shape=[hm_shape] * 7 + [tm_shape] * 2,
        compiler_params=_cparams("parallel"),
        name="rwkv_prep",
    )(u, starts, *lp_rows, w_wa, g2, bd)


def _tri_consts(csz):
    row = lax.broadcasted_iota(jnp.int32, (csz, csz), 0)
    col = lax.broadcasted_iota(jnp.int32, (csz, csz), 1)
    return (col <= row).astype(f32), (col < row).astype(f32), (col == row).astype(f32)


def _rwkv_phase1_kernel(r_ref, k_ref, v_ref, p_ref, q_ref, w_ref, g_ref, rhat_ref, yint_ref, mlow_ref, nn_ref, e_ref,
                        *, csz, nck):
    tril_incl, tril_strict, eye = _tri_consts(csz)
    cs = range(nck)
    sls = [slice(c * csz, (c + 1) * csz) for c in cs]

    def stack(a, b):
        return jnp.concatenate([a, b], axis=0).astype(bf16)

    pr, qt, kt, p0, r0, qh, kh, vb, e_last = [], [], [], [], [], [], [], [], []
    for sl in sls:
        r, k, v, p, q = r_ref[sl, :], k_ref[sl, :], v_ref[sl, :], p_ref[sl, :], q_ref[sl, :]
        g = g_ref[sl, :]
        gp = g - w_ref[sl, :]
        ref = g[csz // 2 - 1:csz // 2, :]
        gl = g[csz - 1:csz, :]
        e_in = jnp.exp(ref - g)
        e_out = jnp.exp(gl - g)
        pr.append(stack(p * jnp.exp(gp - ref), r * jnp.exp(g - ref)))
        qt.append((q * e_in).astype(bf16))
        kt.append((k * e_in).astype(bf16))
        p0.append((p * jnp.exp(gp)).astype(bf16))
        r0.append(r * jnp.exp(g))
        qh.append((q * e_out).astype(bf16))
        kh.append((k * e_out).astype(bf16))
        vb.append(v.astype(bf16))
        e_last.append(jnp.exp(gl))
    gq = [_dot_nt(pr[c], qt[c]) for c in cs]
    gk = [_dot_nt(pr[c], kt[c]) for c in cs]
    a_qp = [gq[c][:csz] * tril_strict for c in cs]
    a_kp = [(gk[c][:csz] * tril_strict).astype(bf16) for c in cs]
    a_qr = [(gq[c][csz:] * tril_incl).astype(bf16) for c in cs]
    a_kr = [(gk[c][csz:] * tril_incl).astype(bf16) for c in cs]
    tinv = [eye + a_qp[c] for c in cs]
    if csz > 2:
        apb = [a_qp[c].astype(bf16) for c in cs]
        apow = [_dot(apb[c], apb[c]) for c in cs]
        n = 2
        while n < csz:
            last = 2 * n >= csz
            rhs = [apow[c].astype(bf16) for c in cs]
            if last:
                prod = [_dot(tinv[c].astype(bf16), rhs[c]) for c in cs]
                tinv = [tinv[c] + prod[c] for c in cs]
            else:
                prod = [_dot(stack(tinv[c], apow[c]), rhs[c]) for c in cs]
                tinv = [tinv[c] + prod[c][:csz] for c in cs]
                apow = [prod[c][csz:] for c in cs]
            n *= 2
    tb_ = [tinv[c].astype(bf16) for c in cs]
    akpv = [_dot(a_kp[c], vb[c]).astype(bf16) for c in cs]
    phb = [_dot(tb_[c], p0[c]).astype(bf16) for c in cs]
    wvb = [_dot(tb_[c], akpv[c]).astype(bf16) for c in cs]
    rhat = [r0[c] + _dot(a_qr[c], phb[c]) for c in cs]
    yint = [_dot(a_qr[c], wvb[c]) + _dot(a_kr[c], vb[c]) for c in cs]
    mlow = [_dot_tn(phb[c], qh[c]) for c in cs]
    nn = [_dot_tn(wvb[c], qh[c]) + _dot_tn(vb[c], kh[c]) for c in cs]
    for c in cs:
        rhat_ref[sls[c], :] = rhat[c]
        yint_ref[sls[c], :] = yint[c]
        mlow_ref[c] = mlow[c]
        nn_ref[c] = nn[c]
        e_ref[c] = jnp.broadcast_to(e_last[c], (8, RWKV_HEAD))


def rwkv_phase1(r, k, v, p, q, w, g, row0, rows, csz, nck):
    tb = nck * csz
    assert rows % tb == 0 and row0 % tb == 0
    off = row0 // tb
    nchunks = rows // csz
    hm_in = pl.BlockSpec((None, tb, RWKV_HEAD), lambda h, i: (h, i + off, 0))
    hm_out = pl.BlockSpec((None, tb, RWKV_HEAD), lambda h, i: (h, i, 0))
    sq = pl.BlockSpec((None, nck, RWKV_HEAD, RWKV_HEAD), lambda h, i: (h, i, 0, 0))
    ev = pl.BlockSpec((None, nck, 8, RWKV_HEAD), lambda h, i: (h, i, 0, 0))
    return pl.pallas_call(
        functools.partial(_rwkv_phase1_kernel, csz=csz, nck=nck),
        grid=(RWKV_HEADS, rows // tb),
        in_specs=[hm_in] * 7,
        out_specs=[hm_out, hm_out, sq, sq, ev],
        out_shape=[jax.ShapeDtypeStruct((RWKV_HEADS, rows, RWKV_HEAD), f32)] * 2
        + [jax.ShapeDtypeStruct((RWKV_HEADS, nchunks, RWKV_HEAD, RWKV_HEAD), f32)] * 2
        + [jax.ShapeDtypeStruct((RWKV_HEADS, nchunks, 8, RWKV_HEAD), f32)],
        compiler_params=_cparams("parallel", "parallel"),
        name="rwkv_phase1",
    )(r, k, v, p, q, w, g)


def _rwkv_phase2_kernel(*refs, csz, nck, n_alias):
    s0_ref, rhat_ref, yint_ref, mlow_ref, nn_ref, e_ref = refs[:6]
    y_ref, sout_ref, s_sc = refs[6 + n_alias:]
    ci = pl.program_id(1)
    hs = range(RWKV_HEADS)

    @pl.when(ci == 0)
    def _():
        s_sc[...] = s0_ref[...]

    s = [s_sc[h] for h in hs]
    for c in range(nck):
        sl = slice(c * csz, (c + 1) * csz)
        sb = [s[h].astype(bf16) for h in hs]
        ys = [_dot_nt(rhat_ref[h, sl, :].astype(bf16), sb[h]) for h in hs]
        sm = [_dot(sb[h], mlow_ref[h, c].astype(bf16)) for h in hs]
        for h in hs:
            y_ref[h, sl, :] = ys[h] + yint_ref[h, sl, :]
        s = [s[h] * e_ref[h, c][:1, :] + sm[h] + nn_ref[h, c] for h in hs]
    for h in hs:
        s_sc[h] = s[h]

    @pl.when(ci == pl.num_programs(1) - 1)
    def _():
        sout_ref[...] = s_sc[...]


def rwkv_phase2(s0, rhat, yint, mlow, nn, e, nseq, tlen, csz, layer=None, s_stack=None):
    cps = tlen // csz
    nck = min(cps, 4)
    assert cps % nck == 0
    nsteps = cps // nck
    tb = nck * csz
    hm = pl.BlockSpec((RWKV_HEADS, tb, RWKV_HEAD), lambda s, c: (0, s * nsteps + c, 0))
    sq = pl.BlockSpec((RWKV_HEADS, nck, RWKV_HEAD, RWKV_HEAD), lambda s, c: (0, s * nsteps + c, 0, 0))
    ev = pl.BlockSpec((RWKV_HEADS, nck, 8, RWKV_HEAD), lambda s, c: (0, s * nsteps + c, 0, 0))
    s_shape = (RWKV_HEADS, RWKV_HEAD, RWKV_HEAD)
    if layer is None:
        st_in = st_out = pl.BlockSpec((None,) + s_shape, lambda s, c: (s, 0, 0, 0))
        st_out_shape = jax.ShapeDtypeStruct((nseq,) + s_shape, f32)
        extra_specs, extra_args = [], []
    else:
        st_in = pl.BlockSpec((None, None) + s_shape, lambda s, c: (layer, s, 0, 0, 0))
        st_out, st_out_shape, extra_specs, extra_args = _stacked_state_out(layer, s_stack, nseq, s_shape)
    n_in = 6
    return pl.pallas_call(
        functools.partial(_rwkv_phase2_kernel, csz=csz, nck=nck, n_alias=len(extra_args)),
        grid=(nseq, nsteps),
        in_specs=[st_in, hm, hm, sq, sq, ev] + extra_specs,
        out_specs=[hm, st_out],
        out_shape=[jax.ShapeDtypeStruct((RWKV_HEADS, nseq * tlen, RWKV_HEAD), f32), st_out_shape],
        scratch_shapes=[pltpu.VMEM(s_shape, f32)],
        input_output_aliases={n_in: 1} if extra_args else {},
        compiler_params=_cparams("parallel", "arbitrary"),
        name="rwkv_phase2",
    )(s0, rhat, yint, mlow, nn, e, *extra_args)


def _rwkv_out_kernel(y_ref, g_ref, bonus_ref, lnw_ref, lnb_ref, bd_ref, o_ref, y_sc):
    for h in range(RWKV_HEADS):
        y_sc[:, h * RWKV_HEAD:(h + 1) * RWKV_HEAD] = y_ref[h]
    y = y_sc[...]
    bd = bd_ref[...]
    mean = _head_sums(y, bd) * (1.0 / RWKV_HEAD)
    yc = y - mean
    var = _head_sums(yc * yc, bd) * (1.0 / RWKV_HEAD)
    yn = yc * lax.rsqrt(var + RWKV_GN_EPS) * lnw_ref[...] + lnb_ref[...]
    o_ref[...] = ((yn + bonus_ref[...]) * g_ref[...]).astype(o_ref.dtype)


def rwkv_out(y_hm, g, bonus, ln_w, ln_b, bd):
    m = g.shape[0]
    tb = _pick(m, (256, 128))
    tok = pl.BlockSpec((tb, RWKV_WIDTH), lambda i: (i, 0))
    vec = pl.BlockSpec((1, RWKV_WIDTH), lambda i: (0, 0))
    return pl.pallas_call(
        _rwkv_out_kernel,
        grid=(m // tb,),
        in_specs=[pl.BlockSpec((RWKV_HEADS, tb, RWKV_HEAD), lambda i: (0, i, 0)), tok, tok, vec, vec,
                  pl.BlockSpec(bd.shape, lambda i: (0, 0))],
        out_specs=tok,
        out_shape=jax.ShapeDtypeStruct((m, RWKV_WIDTH), bf16),
        scratch_shapes=[pltpu.VMEM((tb, RWKV_WIDTH), f32)],
        compiler_params=_cparams("parallel"),
        name="rwkv_out",
    )(y_hm, g, bonus, ln_w.reshape(1, -1), ln_b.reshape(1, -1), bd)


def _mlstm_kernel(*refs, csz, n_alias):
    u_ref, ug_ref, gb_ref, ng_ref, c0_ref, n0_ref, m0_ref = refs[:7]
    y_ref, cout_ref, nout_ref, mout_ref, c_sc, n_sc, m_sc = refs[7 + n_alias:]
    ci = pl.program_id(1)
    dh = MLSTM_DH
    w_ = MLSTM_WIDTH

    @pl.when(ci == 0)
    def _():
        c_sc[...] = c0_ref[...]
        n_sc[...] = n0_ref[...]
        m_sc[...] = m0_ref[...]

    tril_incl, _, _ = _tri_consts(csz)
    causal = tril_incl > 0.0
    gates = ug_ref[...] + gb_ref[...]
    lane = lax.broadcasted_iota(jnp.int32, gates.shape, 1)
    logf = jax.nn.log_sigmoid(gates)
    fcum = _dot_exact_rhs(tril_incl.astype(bf16), logf)
    cols = jnp.where(lane < MLSTM_HEADS, gates, fcum)
    sel = (lax.broadcasted_iota(jnp.int32, (8, LANES), 0) == lax.broadcasted_iota(jnp.int32, (8, LANES), 1))
    hi, mid, lo = _split3(cols)
    selb = sel.astype(bf16)
    rows = _dot_nt(selb, hi) + _dot_nt(selb, mid) + _dot_nt(selb, lo)
    hs = range(MLSTM_HEADS)
    q = [u_ref[:, h * dh:(h + 1) * dh] for h in hs]
    k = [u_ref[:, w_ + h * dh:w_ + (h + 1) * dh] * (dh ** -0.5) for h in hs]
    v = [u_ref[:, 2 * w_ + h * dh:2 * w_ + (h + 1) * dh] for h in hs]
    qb = [q[h].astype(bf16) for h in hs]
    kb = [k[h].astype(bf16) for h in hs]
    vb = [v[h].astype(bf16) for h in hs]
    c_old = [c_sc[h] for h in hs]
    n_old = [n_sc[h:h + 1, :] for h in hs]
    m_old = [m_sc[h:h + 1, :1] for h in hs]
    ig_col = [cols[:, h:h + 1] for h in hs]
    f_col = [cols[:, MLSTM_HEADS + h:MLSTM_HEADS + h + 1] for h in hs]
    d_ts = [jnp.where(causal, f_col[h] - rows[MLSTM_HEADS + h:MLSTM_HEADS + h + 1, :] + rows[h:h + 1, :], -jnp.inf)
            for h in hs]
    inter = [m_old[h] + f_col[h] for h in hs]
    m_t = [jnp.maximum(inter[h], jnp.max(d_ts[h], axis=1, keepdims=True)) for h in hs]
    w_inter = [jnp.exp(inter[h] - m_t[h]) for h in hs]
    qk = [_dot_nt(qb[h], kb[h]) for h in hs]
    cq = [_dot_nt(qb[h], c_old[h].astype(bf16)) for h in hs]
    a = [jnp.exp(d_ts[h] - m_t[h]) * qk[h] for h in hs]
    av = [_dot(a[h].astype(bf16), vb[h]) for h in hs]
    m_new = [m_t[h][csz - 1:csz, :] for h in hs]
    f_last = [f_col[h][csz - 1:csz, :] for h in hs]
    carry = [jnp.exp(m_old[h] + f_last[h] - m_new[h]) for h in hs]
    w_write = [jnp.exp(f_last[h] - f_col[h] + ig_col[h] - m_new[h]) for h in hs]
    vk = [_dot_tn((v[h] * w_write[h]).astype(bf16), kb[h]) for h in hs]
    for h in hs:
        num = w_inter[h] * cq[h] + av[h]
        den = w_inter[h] * jnp.sum(q[h] * n_old[h], axis=1, keepdims=True) + jnp.sum(a[h], axis=1, keepdims=True)
        hh = num / jnp.maximum(jnp.abs(den), jnp.exp(-m_t[h]))
        hh = hh * lax.rsqrt(jnp.mean(hh * hh, axis=1, keepdims=True) + NORM_EPS)
        og = jax.nn.sigmoid(u_ref[:, 3 * w_ + h * dh:3 * w_ + (h + 1) * dh])
        y_ref[:, h * dh:(h + 1) * dh] = (og * hh * ng_ref[:, h * dh:(h + 1) * dh]).astype(y_ref.dtype)
        c_sc[h] = carry[h] * c_old[h] + vk[h]
        n_sc[h:h + 1, :] = carry[h] * n_old[h] + jnp.sum(k[h] * w_write[h], axis=0, keepdims=True)
        m_sc[h:h + 1, :] = jnp.broadcast_to(m_new[h], (1, LANES))

    @pl.when(ci == pl.num_programs(1) - 1)
    def _():
        cout_ref[...] = c_sc[...]
        nout_ref[...] = n_sc[...]
        mout_ref[...] = m_sc[...]


def _stacked_state_out(layer, stack, nseq, shape):
    spec = pl.BlockSpec((None, None) + shape, lambda s, c: (layer, s) + (0,) * len(shape))
    out_shape = jax.ShapeDtypeStruct((DEPTH, nseq) + shape, f32)
    extra_specs = [] if stack is None else [pl.BlockSpec(memory_space=pl.ANY)]
    extra_args = [] if stack is None else [stack]
    return spec, out_shape, extra_specs, extra_args


def mlstm(u, ug, gate_bias_row, norm_g, c0, n0, m0, row0, nseq, tlen, csz, layer=None, c_stack=None):
    cps = tlen // csz
    assert row0 % csz == 0
    off = row0 // csz

    def rowspec(cols):
        return pl.BlockSpec((csz, cols), lambda s, c: (off + s * cps + c, 0))

    def state(shape):
        return pl.BlockSpec((None,) + shape, lambda s, c: (s,) + (0,) * len(shape))

    c_shape = (MLSTM_HEADS, MLSTM_DH, MLSTM_DH)
    if layer is None:
        c_in, c_out, c_out_shape = state(c_shape), state(c_shape), jax.ShapeDtypeStruct((nseq,) + c_shape, f32)
        extra_specs, extra_args = [], []
    else:
        c_in = pl.BlockSpec((None, None) + c_shape, lambda s, c: (layer, s, 0, 0, 0))
        c_out, c_out_shape, extra_specs, extra_args = _stacked_state_out(layer, c_stack, nseq, c_shape)
    n_in = 7
    return pl.pallas_call(
        functools.partial(_mlstm_kernel, csz=csz, n_alias=len(extra_args)),
        grid=(nseq, cps),
        in_specs=[rowspec(4 * MLSTM_WIDTH), rowspec(LANES), pl.BlockSpec((1, LANES), lambda s, c: (0, 0)),
                  pl.BlockSpec((1, MLSTM_WIDTH), lambda s, c: (0, 0)),
                  c_in, state((8, MLSTM_DH)), state((8, LANES))] + extra_specs,
        out_specs=[pl.BlockSpec((csz, MLSTM_WIDTH), lambda s, c: (s * cps + c, 0)),
                   c_out, state((8, MLSTM_DH)), state((8, LANES))],
        out_shape=[jax.ShapeDtypeStruct((nseq * tlen, MLSTM_WIDTH), f32), c_out_shape,
                   jax.ShapeDtypeStruct((nseq, 8, MLSTM_DH), f32), jax.ShapeDtypeStruct((nseq, 8, LANES), f32)],
        scratch_shapes=[pltpu.VMEM((MLSTM_HEADS, MLSTM_DH, MLSTM_DH), f32), pltpu.VMEM((8, MLSTM_DH), f32),
                        pltpu.VMEM((8, LANES), f32)],
        input_output_aliases={n_in: 1} if extra_args else {},
        compiler_params=_cparams("parallel", "arbitrary"),
        name="mlstm",
    )(u, ug, gate_bias_row, norm_g.reshape(1, -1), c0, n0, m0, *extra_args)


def _column_plan():
    per_head = MLA_NOPE + MLA_ROPE
    q_nope = [h * per_head + d for h in range(MLA_HEADS) for d in range(MLA_NOPE)]
    q_rope = []
    for h in range(MLA_HEADS):
        base = h * per_head + MLA_NOPE
        x1 = [base + e for e in range(ROPE_HALF)]
        x2 = [base + ROPE_HALF + e for e in range(ROPE_HALF)]
        q_rope += x1 + x2 + x2 + x1
    kv_base = MLA_Q_COLS
    ckv = [kv_base + c for c in range(KV_RANK)]
    x1 = [kv_base + KV_RANK + e for e in range(ROPE_HALF)]
    x2 = [kv_base + KV_RANK + ROPE_HALF + e for e in range(ROPE_HALF)]
    return np.asarray(q_nope + q_rope, np.int32), np.asarray(ckv + x1 + x2 + x2 + x1, np.int32)


def _rope_tables():
    pos = jnp.concatenate([jnp.tile(jnp.arange(SEQ), BATCH), jnp.tile(PAST_LEN + jnp.arange(DEC_SEQ), DEC_BATCH)])
    inv = ROPE_THETA ** (-jnp.arange(ROPE_HALF, dtype=f32) / ROPE_HALF)
    ang = pos.astype(f32)[:, None] * inv[None, :]
    cos, sin = jnp.cos(ang), jnp.sin(ang)
    zero = jnp.zeros_like(cos)
    return jnp.concatenate([cos, cos, zero, zero], axis=1), jnp.concatenate([-sin, sin, zero, zero], axis=1)


def _pad8(x, rows_axis):
    pad = [(0, 0)] * x.ndim
    pad[rows_axis] = (0, 8 - x.shape[rows_axis])
    return jnp.pad(x, pad)


def kernel(x_prompt, x_sample, cache_ckv, cache_kpe, page_table, state_rwkv_shift, state_rwkv_S, state_mlstm_C, state_mlstm_n, state_mlstm_m, norm_gains, w_in, g_ckv, w_uk, w_uv, rwkv_mu, rwkv_w0, rwkv_w2, rwkv_a0, rwkv_a2, rwkv_g2, rwkv_kk, rwkv_ka, rwkv_rk, rwkv_ln_w, rwkv_ln_b, mlstm_gate_b, mlstm_norm, gate_b, w_br_mla, w_br_rwkv, w_br_mlstm, w_out, w_up, w_down):
    d = D_MODEL
    o0 = MLA_Q_COLS
    o1 = o0 + MLA_KV_COLS
    o2 = o1 + RWKV_COLS
    o3 = o2 + MLSTM_COLS
    q_cols, kv_cols = _column_plan()
    c4, s4 = _rope_tables()
    head_id = np.arange(MXU_DIM) // RWKV_HEAD
    bd = jnp.asarray(head_id[:, None] == head_id[None, :], bf16)
    zeros_s = jnp.zeros((BATCH, RWKV_HEADS, RWKV_HEAD, RWKV_HEAD), f32)
    zeros_c = jnp.zeros((BATCH, MLSTM_HEADS, MLSTM_DH, MLSTM_DH), f32)
    zeros_n = jnp.zeros((BATCH, 8, MLSTM_DH), f32)
    zeros_m = jnp.zeros((BATCH, 8, LANES), f32)
    tq = _pick(SEQ, (512, 256))
    tb_rw = 256
    n_starts = tb_rw // DEC_SEQ
    assert SEQ % tb_rw == 0 and M_SAMPLE % tb_rw == 0
    cache_kpe_t = jnp.swapaxes(cache_kpe, 2, 3)

    x = jnp.concatenate([x_prompt.reshape(M_PROMPT, d), x_sample.reshape(M_SAMPLE, d)], axis=0)
    h, h_lo = rmsnorm_bf16(x, norm_gains[0, 0])
    prompt_states, sample_states = [], []
    s_stack = c_stack = None
    for l in range(DEPTH):
        w = w_in[l]
        w_q = jnp.take(w, q_cols, axis=1).astype(bf16)
        w_kv = jnp.take(w, kv_cols, axis=1).astype(bf16)
        w_rw = w[:, o1:o2].astype(bf16)
        w_ml = w[:, o2:o2 + 4 * MLSTM_WIDTH].astype(bf16)
        w_mg = jnp.pad(w[:, o2 + 4 * MLSTM_WIDTH:o3], ((0, 0), (0, LANES - 2 * MLSTM_HEADS)))
        w_gt = w[:, o3:].astype(bf16)

        u_q = matmul(h, w_q)
        u_kv = matmul(h, w_kv)
        u_rw = matmul(h, w_rw)
        u_ml = matmul(h, w_ml)
        u_mg = matmul_split(h, h_lo, w_mg)
        gates = matmul(h, w_gt, out_dtype=bf16, bias=gate_b[l].reshape(-1))

        w_uk_t = jnp.transpose(w_uk[l], (1, 2, 0)).astype(bf16)
        w_uv_h = jnp.transpose(w_uv[l], (1, 0, 2)).astype(bf16)
        ql_p, qp_p, kc_p, kpe_p, ckv_p, kpef_p = mla_prep(u_q, u_kv, c4, s4, w_uk_t, g_ckv[l], 0, M_PROMPT, bf16)
        ql_s, qp_s, _, _, ckv_s, kpef_s = mla_prep(u_q, u_kv, c4, s4, w_uk_t, g_ckv[l], M_PROMPT, M_SAMPLE, f32)
        y_mla_p = attn_prompt(ql_p, qp_p, kc_p, kpe_p, w_uv_h, tq)
        y_mla_s = attn_sample(ql_s, qp_s, ckv_s, kpef_s, w_uv_h, cache_ckv, cache_kpe_t, page_table, l)
        y_mla = jnp.concatenate([y_mla_p, y_mla_s.astype(bf16)], axis=0)

        block_last = u_rw[tb_rw - 1:M_PROMPT:tb_rw].reshape(BATCH, SEQ // tb_rw, 1, RWKV_COLS)
        before = jnp.concatenate([jnp.zeros((BATCH, 1, 1, RWKV_COLS), f32), block_last[:, :-1]], axis=1)
        starts = jnp.concatenate([
            jnp.pad(before.reshape(M_PROMPT // tb_rw, 1, RWKV_COLS), ((0, 0), (0, n_starts - 1), (0, 0))),
            state_rwkv_shift[l].reshape(M_SAMPLE // tb_rw, n_starts, RWKV_COLS)], axis=0)
        zero_blk = jnp.zeros((W_LORA, RWKV_WIDTH), f32)
        w_wa = jnp.concatenate([jnp.concatenate([rwkv_w2[l], zero_blk], axis=1),
                                jnp.concatenate([zero_blk, rwkv_a2[l]], axis=1)], axis=0).astype(bf16)
        lp_rows = [rwkv_mu[l].reshape(1, -1), rwkv_w0[l].reshape(1, -1), rwkv_a0[l].reshape(1, -1),
                   rwkv_kk[l].reshape(1, -1), rwkv_ka[l].reshape(1, -1), rwkv_rk[l].reshape(1, -1)]
        ops = rwkv_prep(u_rw, starts, lp_rows, w_wa, rwkv_g2[l].astype(bf16), bd, tb_rw)
        hm_ops, g_, bonus = ops[:7], ops[7], ops[8]
        t_p = rwkv_phase1(*hm_ops, 0, M_PROMPT, RWKV_CHUNK, 16)
        t_s = rwkv_phase1(*hm_ops, M_PROMPT, M_SAMPLE, DEC_SEQ, 32)
        y_p, s_p = rwkv_phase2(zeros_s, *t_p, BATCH, SEQ, RWKV_CHUNK)
        y_s, s_stack = rwkv_phase2(state_rwkv_S, *t_s, DEC_BATCH, DEC_SEQ, DEC_SEQ, layer=l, s_stack=s_stack)
        y_rwkv = rwkv_out(jnp.concatenate([y_p, y_s], axis=1), g_, bonus, rwkv_ln_w[l], rwkv_ln_b[l], bd)

        gb_row = jnp.pad(mlstm_gate_b[l].reshape(1, -1), ((0, 0), (0, LANES - 2 * MLSTM_HEADS)))
        ym_p, c_p, n_p, m_p = mlstm(u_ml, u_mg, gb_row, mlstm_norm[l], zeros_c, zeros_n, zeros_m, 0, BATCH, SEQ, MLSTM_CHUNK)
        m0_s = jnp.broadcast_to(_pad8(state_mlstm_m[l], 1)[:, :, None], (DEC_BATCH, 8, LANES))
        ym_s, c_stack, n_s, m_s = mlstm(u_ml, u_mg, gb_row, mlstm_norm[l], state_mlstm_C, _pad8(state_mlstm_n[l], 1), m0_s,
                                        M_PROMPT, DEC_BATCH, DEC_SEQ, DEC_SEQ, layer=l, c_stack=c_stack)
        y_mlstm = jnp.concatenate([ym_p, ym_s], axis=0).astype(bf16)

        merged = merge_branches(y_mla, y_rwkv, y_mlstm, w_br_mla, w_br_rwkv, w_br_mlstm, gates, l)
        attn_out = matmul(merged, w_out, layer=l)
        x, hf = resnorm(x, attn_out, norm_gains[l, 1], norm_gains[l, 2], False)
        ff = matmul_ksplit(matmul(hf, w_up, out_dtype=bf16, act="relu2", layer=l), w_down, layer=l)
        g_next = norm_gains[l + 1, 0] if l + 1 < DEPTH else norm_gains[l, 3]
        x, h, h_lo = resnorm(x, ff, norm_gains[l, 3], g_next, True)

        shift_p = u_rw[SEQ - 1:M_PROMPT:SEQ]
        shift_s = u_rw[M_PROMPT + DEC_SEQ - 1::DEC_SEQ]
        prompt_states.append((ckv_p.reshape(BATCH, SEQ, KV_RANK), kpef_p.reshape(BATCH, SEQ, MLA_ROPE), shift_p, s_p,
                              c_p, n_p[:, :MLSTM_HEADS], m_p[:, :MLSTM_HEADS, 0]))
        sample_states.append((ckv_s.reshape(DEC_BATCH, DEC_SEQ, KV_RANK), kpef_s.reshape(DEC_BATCH, DEC_SEQ, MLA_ROPE),
                              shift_s, None, None, n_s[:, :MLSTM_HEADS], m_s[:, :MLSTM_HEADS, 0]))

    outs = [x[:M_PROMPT].reshape(BATCH, SEQ, d), x[M_PROMPT:].reshape(DEC_BATCH, DEC_SEQ, d)]
    for states, stacked in ((prompt_states, {}), (sample_states, {3: s_stack, 4: c_stack})):
        for i in range(7):
            outs.append(stacked[i] if i in stacked else jnp.stack([st[i] for st in states]))
    return tuple(outs)
```

```python
import functools

import jax
import jax.numpy as jnp
import numpy as np
from jax import lax
from jax.experimental import pallas as pl
from jax.experimental.pallas import tpu as pltpu

D_MODEL = 2048
BATCH = 2
SEQ = 4096
DEPTH = 2
DEC_BATCH = 128
DEC_SEQ = 8
PAST_LEN = 16384
PAGE_SIZE = 128

MLA_HEADS = 8
MLA_NOPE = 128
MLA_ROPE = 64
MLA_V = 128
KV_RANK = 256
ROPE_THETA = 10000.0
MLA_SCALE = (MLA_NOPE + MLA_ROPE) ** -0.5
RWKV_HEAD = 64
RWKV_WIDTH = 1024
RWKV_HEADS = RWKV_WIDTH // RWKV_HEAD
W_LORA = 64
A_LORA = 64
G_LORA = 128
RWKV_GN_EPS = 64e-5
MLSTM_HEADS = 4
MLSTM_DH = 256
MLSTM_WIDTH = MLSTM_HEADS * MLSTM_DH
MLSTM_CHUNK = 64
N_BRANCH = 3
D_FF = 4 * D_MODEL
NORM_EPS = 1e-6

MLA_Q_COLS = MLA_HEADS * (MLA_NOPE + MLA_ROPE)
MLA_KV_COLS = KV_RANK + MLA_ROPE
RWKV_COLS = 3 * RWKV_WIDTH + W_LORA + A_LORA + G_LORA
MLSTM_COLS = 4 * MLSTM_WIDTH + 2 * MLSTM_HEADS
GATE_COLS = N_BRANCH * D_MODEL

LANES = 128
MXU_DIM = 256
ROPE_HALF = MLA_ROPE // 2
QL_COLS = MLA_HEADS * KV_RANK
QP_COLS = MLA_HEADS * LANES
UQ_COLS = MLA_HEADS * MLA_NOPE + QP_COLS
UKV_COLS = KV_RANK + LANES
N_PAGES = PAST_LEN // PAGE_SIZE
PAGES_PER_STEP = 8
SEQS_PER_STEP = 4
RWKV_CHUNK = 64
M_PROMPT = BATCH * SEQ
M_SAMPLE = DEC_BATCH * DEC_SEQ
M_TOK = M_PROMPT + M_SAMPLE
VMEM_LIMIT = 56 * 1024 * 1024

f32 = jnp.float32
bf16 = jnp.bfloat16


def _cparams(*sem):
    return pltpu.CompilerParams(dimension_semantics=sem, vmem_limit_bytes=VMEM_LIMIT)


def _pick(n, cands):
    for c in cands:
        if n % c == 0:
            return c
    raise ValueError(f"no tile for {n} in {cands}")


def _dot(a, b):
    return jnp.dot(a, b, preferred_element_type=f32)


def _dot_nt(a, b):
    return lax.dot_general(a, b, (((1,), (1,)), ((), ())), preferred_element_type=f32)


def _dot_tn(a, b):
    return lax.dot_general(a, b, (((0,), (0,)), ((), ())), preferred_element_type=f32)


def _split3(x):
    hi = x.astype(bf16)
    r1 = x - hi.astype(f32)
    mid = r1.astype(bf16)
    lo = (r1 - mid.astype(f32)).astype(bf16)
    return hi, mid, lo


def _dot_exact_rhs(sel, x):
    hi, mid, lo = _split3(x)
    return _dot(sel, hi) + _dot(sel, mid) + _dot(sel, lo)


def _dot_exact_lhs(x, sel):
    hi, mid, lo = _split3(x)
    return _dot(hi, sel) + _dot(mid, sel) + _dot(lo, sel)


def _rms(x, g):
    return x * lax.rsqrt(jnp.mean(x * x, axis=-1, keepdims=True) + NORM_EPS) * g


def _emit_norm(hn, h_ref, lo_ref):
    hb = hn.astype(bf16)
    h_ref[...] = hb
    if lo_ref is not None:
        lo_ref[...] = (hn - hb.astype(f32)).astype(bf16)


def _norm_kernel(x_ref, g_ref, h_ref, lo_ref):
    _emit_norm(_rms(x_ref[...], g_ref[...]), h_ref, lo_ref)


def rmsnorm_bf16(x, g):
    m, d = x.shape
    tm = _pick(m, (512, 256))
    row = pl.BlockSpec((tm, d), lambda i: (i, 0))
    return pl.pallas_call(
        _norm_kernel,
        grid=(m // tm,),
        in_specs=[row, pl.BlockSpec((1, d), lambda i: (0, 0))],
        out_specs=[row, row],
        out_shape=[jax.ShapeDtypeStruct((m, d), bf16)] * 2,
        compiler_params=_cparams("parallel"),
        name="rmsnorm",
    )(x, g.reshape(1, d))


def _resnorm_kernel(x_ref, y_ref, gp_ref, gn_ref, xo_ref, h_ref, *lo_ref):
    xn = x_ref[...] + _rms(y_ref[...], gp_ref[...])
    xo_ref[...] = xn
    _emit_norm(_rms(xn, gn_ref[...]), h_ref, lo_ref[0] if lo_ref else None)


def resnorm(x, y, g_post, g_next, with_lo):
    m, d = x.shape
    tm = _pick(m, (512, 256))
    row = pl.BlockSpec((tm, d), lambda i: (i, 0))
    gain = pl.BlockSpec((1, d), lambda i: (0, 0))
    n_h = 2 if with_lo else 1
    return pl.pallas_call(
        _resnorm_kernel,
        grid=(m // tm,),
        in_specs=[row, row, gain, gain],
        out_specs=[row] * (1 + n_h),
        out_shape=[jax.ShapeDtypeStruct((m, d), f32)] + [jax.ShapeDtypeStruct((m, d), bf16)] * n_h,
        compiler_params=_cparams("parallel"),
        name="resnorm",
    )(x, y, g_post.reshape(1, d), g_next.reshape(1, d))


def _weights(b_ref):
    b = b_ref[...]
    return b if b.dtype == bf16 else b.astype(bf16)


def _weight_spec(layer, shape, index_map):
    if layer is None:
        return pl.BlockSpec(shape, index_map)
    return pl.BlockSpec((None,) + shape, lambda *g: (layer,) + index_map(*g))


def _mm_kernel(a_ref, b_ref, o_ref, *, act):
    acc = _dot(a_ref[...], _weights(b_ref))
    if act == "relu2":
        acc = jnp.square(jnp.maximum(acc, 0.0))
    o_ref[...] = acc.astype(o_ref.dtype)


def _mm_bias_sigmoid_kernel(a_ref, b_ref, bias_ref, o_ref):
    acc = _dot(a_ref[...], _weights(b_ref)) + bias_ref[...]
    o_ref[...] = jax.nn.sigmoid(acc).astype(o_ref.dtype)


def _mm_tiles(m, n):
    tm = _pick(m, (1024, 512, 256))
    tn = _pick(n, (1024, 512, 384, 256, 128)) if n % 1664 else 1664
    return tm, tn


def matmul(a, b, out_dtype=f32, act=None, bias=None, layer=None):
    m, k = a.shape
    n = b.shape[-1]
    tm, tn = _mm_tiles(m, n)
    in_specs = [pl.BlockSpec((tm, k), lambda i, j: (i, 0)), _weight_spec(layer, (k, tn), lambda i, j: (0, j))]
    args = [a, b]
    if bias is not None:
        kern = _mm_bias_sigmoid_kernel
        in_specs.append(pl.BlockSpec((1, tn), lambda i, j: (0, j)))
        args.append(bias.reshape(1, n))
    else:
        kern = functools.partial(_mm_kernel, act=act)
    return pl.pallas_call(
        kern,
        grid=(m // tm, n // tn),
        in_specs=in_specs,
        out_specs=pl.BlockSpec((tm, tn), lambda i, j: (i, j)),
        out_shape=jax.ShapeDtypeStruct((m, n), out_dtype),
        compiler_params=_cparams("parallel", "parallel"),
        name="matmul",
    )(*args)


def _mm3_kernel(a_ref, alo_ref, b_ref, blo_ref, o_ref):
    a = a_ref[...]
    b = b_ref[...]
    o_ref[...] = _dot(a, b) + _dot(alo_ref[...], b) + _dot(a, blo_ref[...])


def matmul_split(a, a_lo, b):
    m, k = a.shape
    n = b.shape[1]
    b_hi = b.astype(bf16)
    b_lo = (b - b_hi.astype(f32)).astype(bf16)
    tm = _pick(m, (1024, 512, 256))
    row = pl.BlockSpec((tm, k), lambda i: (i, 0))
    col = pl.BlockSpec((k, n), lambda i: (0, 0))
    return pl.pallas_call(
        _mm3_kernel,
        grid=(m // tm,),
        in_specs=[row, row, col, col],
        out_specs=pl.BlockSpec((tm, n), lambda i: (i, 0)),
        out_shape=jax.ShapeDtypeStruct((m, n), f32),
        compiler_params=_cparams("parallel"),
        name="matmul_split",
    )(a, a_lo, b_hi, b_lo)


def _mm_acc_kernel(a_ref, b_ref, o_ref, acc_ref):
    kk = pl.program_id(2)

    @pl.when(kk == 0)
    def _():
        acc_ref[...] = jnp.zeros_like(acc_ref)

    acc_ref[...] += _dot(a_ref[...], _weights(b_ref))

    @pl.when(kk == pl.num_programs(2) - 1)
    def _():
        o_ref[...] = acc_ref[...]


def matmul_ksplit(a, b, layer=None):
    m, k = a.shape
    n = b.shape[-1]
    tm = _pick(m, (1024, 512, 256))
    tn = _pick(n, (1024, 512, 256))
    tk = _pick(k, (2048, 1024, 512, 256))
    return pl.pallas_call(
        _mm_acc_kernel,
        grid=(m // tm, n // tn, k // tk),
        in_specs=[pl.BlockSpec((tm, tk), lambda i, j, kk: (i, kk)),
                  _weight_spec(layer, (tk, tn), lambda i, j, kk: (kk, j))],
        out_specs=pl.BlockSpec((tm, tn), lambda i, j, kk: (i, j)),
        out_shape=jax.ShapeDtypeStruct((m, n), f32),
        scratch_shapes=[pltpu.VMEM((tm, tn), f32)],
        compiler_params=_cparams("parallel", "parallel", "arbitrary"),
        name="matmul_ksplit",
    )(a, b)


def _merge_kernel(ya_ref, yr_ref, ym_ref, wa_ref, wr_ref, wm_ref, ga_ref, gr_ref, gm_ref, o_ref):
    acc = ga_ref[...].astype(f32) * _dot(ya_ref[...], _weights(wa_ref))
    acc += gr_ref[...].astype(f32) * _dot(yr_ref[...], _weights(wr_ref))
    acc += gm_ref[...].astype(f32) * _dot(ym_ref[...], _weights(wm_ref))
    o_ref[...] = acc.astype(o_ref.dtype)


def merge_branches(y_mla, y_rwkv, y_mlstm, w_mla, w_rwkv, w_mlstm, gates, layer):
    m = y_mla.shape[0]
    d = w_mla.shape[-1]
    tm = _pick(m, (1024, 512, 256))
    tn = _pick(d, (512, 256))
    nj = d // tn
    ys = [pl.BlockSpec((tm, y.shape[1]), lambda i, j: (i, 0)) for y in (y_mla, y_rwkv, y_mlstm)]
    ws = [_weight_spec(layer, (w.shape[1], tn), lambda i, j: (0, j)) for w in (w_mla, w_rwkv, w_mlstm)]
    gs = [pl.BlockSpec((tm, tn), functools.partial(lambda i, j, b: (i, j + b * nj), b=b)) for b in range(N_BRANCH)]
    return pl.pallas_call(
        _merge_kernel,
        grid=(m // tm, nj),
        in_specs=ys + ws + gs,
        out_specs=pl.BlockSpec((tm, tn), lambda i, j: (i, j)),
        out_shape=jax.ShapeDtypeStruct((m, d), bf16),
        compiler_params=_cparams("parallel", "parallel"),
        name="merge_branches",
    )(y_mla, y_rwkv, y_mlstm, w_mla, w_rwkv, w_mlstm, gates, gates, gates)


def _rope_group(p, c4, s4):
    return p * c4 + pltpu.roll(p, 2 * ROPE_HALF, axis=1) * s4


def _mla_prep_kernel(uq_ref, ukv_ref, c4_ref, s4_ref, wuk_ref, g_ref, ql_ref, qp_ref, kc_ref, kpe_ref, ckv_ref, kpef_ref):
    c4 = c4_ref[...]
    s4 = s4_ref[...]
    nope = MLA_HEADS * MLA_NOPE
    for h in range(MLA_HEADS):
        qn = uq_ref[:, h * MLA_NOPE:(h + 1) * MLA_NOPE].astype(bf16)
        ql = _dot(qn, wuk_ref[h]) * MLA_SCALE
        ql_ref[:, h * KV_RANK:(h + 1) * KV_RANK] = ql.astype(ql_ref.dtype)
        qp = _rope_group(uq_ref[:, nope + h * LANES:nope + (h + 1) * LANES], c4, s4) * MLA_SCALE
        qp_ref[:, h * LANES:(h + 1) * LANES] = qp.astype(qp_ref.dtype)
    ckv = _rms(ukv_ref[:, :KV_RANK], g_ref[...])
    ckv_ref[...] = ckv
    kc_ref[...] = ckv.astype(bf16)
    kpe = _rope_group(ukv_ref[:, KV_RANK:], c4, s4)
    kpe_ref[...] = kpe.astype(bf16)
    kpef_ref[...] = kpe[:, :MLA_ROPE]


def mla_prep(u_q, u_kv, c4, s4, w_uk_t, g_ckv, row0, rows, q_dtype):
    tb = _pick(rows, (256, 128))
    assert row0 % tb == 0
    off = row0 // tb

    def rowspec(cols):
        return pl.BlockSpec((tb, cols), lambda i: (i + off, 0))

    def outspec(cols):
        return pl.BlockSpec((tb, cols), lambda i: (i, 0))

    return pl.pallas_call(
        _mla_prep_kernel,
        grid=(rows // tb,),
        in_specs=[rowspec(UQ_COLS), rowspec(UKV_COLS), rowspec(LANES), rowspec(LANES),
                  pl.BlockSpec((MLA_HEADS, MLA_NOPE, KV_RANK), lambda i: (0, 0, 0)),
                  pl.BlockSpec((1, KV_RANK), lambda i: (0, 0))],
        out_specs=[outspec(QL_COLS), outspec(QP_COLS), outspec(KV_RANK), outspec(LANES), outspec(KV_RANK), outspec(MLA_ROPE)],
        out_shape=[jax.ShapeDtypeStruct((rows, QL_COLS), q_dtype), jax.ShapeDtypeStruct((rows, QP_COLS), q_dtype),
                   jax.ShapeDtypeStruct((rows, KV_RANK), bf16), jax.ShapeDtypeStruct((rows, LANES), bf16),
                   jax.ShapeDtypeStruct((rows, KV_RANK), f32), jax.ShapeDtypeStruct((rows, MLA_ROPE), f32)],
        compiler_params=_cparams("parallel"),
        name="mla_prep",
    )(u_q, u_kv, c4, s4, w_uk_t, g_ckv.reshape(1, KV_RANK))


def _attn_prompt_kernel(ql_ref, qp_ref, kc_ref, kpe_ref, wuv_ref, o_ref, m_sc, l_sc, acc_sc, *, tq, tk):
    qi = pl.program_id(1)
    ki = pl.program_id(2)

    @pl.when(ki == 0)
    def _():
        m_sc[...] = jnp.full_like(m_sc, -jnp.inf)
        l_sc[...] = jnp.zeros_like(l_sc)
        acc_sc[...] = jnp.zeros_like(acc_sc)

    @pl.when(ki <= qi)
    def _():
        kc = kc_ref[...]
        kpe = kpe_ref[...]
        rel = lax.broadcasted_iota(jnp.int32, (tq, tk), 1) - lax.broadcasted_iota(jnp.int32, (tq, tk), 0)
        visible = rel <= (qi * tq - ki * tk)
        for h in range(MLA_HEADS):
            s = _dot_nt(ql_ref[:, h * KV_RANK:(h + 1) * KV_RANK], kc)
            s += _dot_nt(qp_ref[:, h * LANES:(h + 1) * LANES], kpe)
            s = jnp.where(visible, s, -jnp.inf)
            m_prev = m_sc[h]
            m_new = jnp.maximum(m_prev, jnp.max(s, axis=1, keepdims=True))
            alpha = jnp.exp(m_prev - m_new)
            p = jnp.exp(s - m_new[:, :1])
            l_sc[h] = alpha * l_sc[h] + jnp.sum(p, axis=1, keepdims=True)
            acc_sc[h] = acc_sc[h] * alpha[:, :1] + _dot(p.astype(bf16), kc)
            m_sc[h] = m_new

    @pl.when(ki == qi)
    def _():
        for h in range(MLA_HEADS):
            o = acc_sc[h] / l_sc[h][:, :1]
            o_ref[:, h * MLA_V:(h + 1) * MLA_V] = _dot(o.astype(bf16), wuv_ref[h]).astype(o_ref.dtype)


def attn_prompt(ql, qp, kc, kpe, w_uv_h, tq):
    tk = tq
    nq = SEQ // tq

    def qmap(b, qi, ki):
        return (b * nq + qi, 0)

    def kmap(b, qi, ki):
        return (b * nq + jnp.minimum(ki, qi), 0)

    return pl.pallas_call(
        functools.partial(_attn_prompt_kernel, tq=tq, tk=tk),
        grid=(BATCH, nq, nq),
        in_specs=[pl.BlockSpec((tq, QL_COLS), qmap), pl.BlockSpec((tq, QP_COLS), qmap),
                  pl.BlockSpec((tk, KV_RANK), kmap), pl.BlockSpec((tk, LANES), kmap),
                  pl.BlockSpec((MLA_HEADS, KV_RANK, MLA_V), lambda b, qi, ki: (0, 0, 0))],
        out_specs=pl.BlockSpec((tq, MLA_HEADS * MLA_V), qmap),
        out_shape=jax.ShapeDtypeStruct((M_PROMPT, MLA_HEADS * MLA_V), bf16),
        scratch_shapes=[pltpu.VMEM((MLA_HEADS, tq, LANES), f32), pltpu.VMEM((MLA_HEADS, tq, LANES), f32),
                        pltpu.VMEM((MLA_HEADS, tq, KV_RANK), f32)],
        compiler_params=_cparams("parallel", "parallel", "arbitrary"),
        name="attn_prompt",
    )(ql, qp, kc, kpe, w_uv_h)


def _attn_sample_kernel(pt_ref, ql_ref, qp_ref, cn_ref, pn_ref, wuv_ref, ckv_hbm, kpe_hbm, o_ref,
                        q_sc, qp_sc, kc_sc, kp_sc, m_sc, l_sc, acc_sc, ckv_buf, kpe_buf, sem_c, sem_p, *, layer):
    pps = PAGES_PER_STEP
    nsq = SEQS_PER_STEP
    n_pg = nsq * pps
    b = pl.program_id(0)
    j = pl.program_id(1)
    nchunk = pl.num_programs(1)
    rows = MLA_HEADS * DEC_SEQ
    seqs = range(nsq)

    def page_copies(slot, idx, page):
        return (pltpu.make_async_copy(ckv_hbm.at[layer, page], ckv_buf.at[slot, idx], sem_c.at[slot]),
                pltpu.make_async_copy(kpe_hbm.at[layer, page], kpe_buf.at[slot, idx], sem_p.at[slot]))

    def start_step(bb, jj, slot):
        for r in seqs:
            for i in range(pps):
                for cp in page_copies(slot, r * pps + i, pt_ref[bb * nsq + r, jj * pps + i]):
                    cp.start()

    t = b * nchunk + j
    slot = t % 2

    @pl.when(t == 0)
    def _():
        start_step(b, j, slot)

    @pl.when(t + 1 < pl.num_programs(0) * nchunk)
    def _():
        last_chunk = j == nchunk - 1
        start_step(jnp.where(last_chunk, b + 1, b), jnp.where(last_chunk, 0, j + 1), 1 - slot)

    for idx in range(n_pg):
        for cp in page_copies(slot, idx, 0):
            cp.wait()
    ckv_pages = [ckv_buf.at[slot, idx] for idx in range(n_pg)]
    kpe_pages = [kpe_buf.at[slot, idx] for idx in range(n_pg)]

    @pl.when(j == 0)
    def _():
        for r in seqs:
            tok = slice(r * DEC_SEQ, (r + 1) * DEC_SEQ)
            for h in range(MLA_HEADS):
                q_sc[r, h * DEC_SEQ:(h + 1) * DEC_SEQ, :] = ql_ref[tok, h * KV_RANK:(h + 1) * KV_RANK]
                qp_sc[r, h * DEC_SEQ:(h + 1) * DEC_SEQ, :] = qp_ref[tok, h * LANES:(h + 1) * LANES]
        m_sc[...] = jnp.full_like(m_sc, -jnp.inf)
        l_sc[...] = jnp.zeros_like(l_sc)
        acc_sc[...] = jnp.zeros_like(acc_sc)

    qb = [q_sc[r].astype(bf16) for r in seqs]
    qpb = [qp_sc[r][:, :MLA_ROPE].astype(bf16) for r in seqs]

    def online(r, s, vals):
        m_prev = m_sc[r]
        m_new = jnp.maximum(m_prev, jnp.max(s, axis=1, keepdims=True))
        alpha = jnp.exp(m_prev - m_new)
        p = jnp.exp(s - m_new[:, :1])
        l_sc[r] = alpha * l_sc[r] + jnp.sum(p, axis=1, keepdims=True)
        acc_sc[r] = acc_sc[r] * alpha[:, :1] + _dot(p.astype(bf16), vals)
        m_sc[r] = m_new

    for r in seqs:
        for i in range(pps):
            kc_sc[r, i * PAGE_SIZE:(i + 1) * PAGE_SIZE, :] = ckv_pages[r * pps + i][...].astype(bf16)
            kp_sc[r, :, i * PAGE_SIZE:(i + 1) * PAGE_SIZE] = kpe_pages[r * pps + i][...].astype(bf16)
    kcs = [kc_sc[r] for r in seqs]
    scores = [_dot_nt(qb[r], kcs[r]) + _dot(qpb[r], kp_sc[r]) for r in seqs]
    for r in seqs:
        online(r, scores[r], kcs[r])

    @pl.when(j == pl.num_programs(1) - 1)
    def _():
        pad = jnp.zeros((LANES - DEC_SEQ, KV_RANK), f32)
        key = lax.broadcasted_iota(jnp.int32, (rows, LANES), 1)
        tokid = lax.broadcasted_iota(jnp.int32, (rows, LANES), 0) % DEC_SEQ
        for r in seqs:
            tok = slice(r * DEC_SEQ, (r + 1) * DEC_SEQ)
            cn = jnp.concatenate([cn_ref[tok, :], pad], axis=0).astype(bf16)
            pn = jnp.concatenate([pn_ref[tok, :], pad[:, :MLA_ROPE]], axis=0).astype(bf16)
            s = _dot_nt(qb[r], cn) + _dot_nt(qpb[r], pn)
            online(r, jnp.where(key <= tokid, s, -jnp.inf), cn)
            o = acc_sc[r] / l_sc[r][:, :1]
            for h in range(MLA_HEADS):
                oh = o[h * DEC_SEQ:(h + 1) * DEC_SEQ, :].astype(bf16)
                o_ref[tok, h * MLA_V:(h + 1) * MLA_V] = _dot(oh, wuv_ref[h])


def attn_sample(ql, qp, c_new, p_new, w_uv_h, cache_ckv, cache_kpe_t, page_table, layer):
    pps = PAGES_PER_STEP
    nsq = SEQS_PER_STEP
    assert N_PAGES % pps == 0 and DEC_BATCH % nsq == 0
    nchunk = N_PAGES // pps
    rows = MLA_HEADS * DEC_SEQ
    keys = pps * PAGE_SIZE

    def seqspec(cols):
        return pl.BlockSpec((nsq * DEC_SEQ, cols), lambda b, j, pt: (b, 0))

    n_pg = nsq * pps
    in_specs = [seqspec(QL_COLS), seqspec(QP_COLS), seqspec(KV_RANK), seqspec(MLA_ROPE),
                pl.BlockSpec((MLA_HEADS, KV_RANK, MLA_V), lambda b, j, pt: (0, 0, 0)),
                pl.BlockSpec(memory_space=pl.ANY), pl.BlockSpec(memory_space=pl.ANY)]
    grid_spec = pltpu.PrefetchScalarGridSpec(
        num_scalar_prefetch=1,
        grid=(DEC_BATCH // nsq, nchunk),
        in_specs=in_specs,
        out_specs=pl.BlockSpec((nsq * DEC_SEQ, MLA_HEADS * MLA_V), lambda b, j, pt: (b, 0)),
        scratch_shapes=[pltpu.VMEM((nsq, rows, KV_RANK), f32), pltpu.VMEM((nsq, rows, LANES), f32),
                        pltpu.VMEM((nsq, keys, KV_RANK), bf16), pltpu.VMEM((nsq, MLA_ROPE, keys), bf16),
                        pltpu.VMEM((nsq, rows, LANES), f32), pltpu.VMEM((nsq, rows, LANES), f32),
                        pltpu.VMEM((nsq, rows, KV_RANK), f32),
                        pltpu.VMEM((2, n_pg, PAGE_SIZE, KV_RANK), f32), pltpu.VMEM((2, n_pg, MLA_ROPE, PAGE_SIZE), f32),
                        pltpu.SemaphoreType.DMA((2,)), pltpu.SemaphoreType.DMA((2,))],
    )
    return pl.pallas_call(
        functools.partial(_attn_sample_kernel, layer=layer),
        grid_spec=grid_spec,
        out_shape=jax.ShapeDtypeStruct((M_SAMPLE, MLA_HEADS * MLA_V), f32),
        compiler_params=_cparams("arbitrary", "arbitrary"),
        name="attn_sample",
    )(page_table, ql, qp, c_new, p_new, w_uv_h, cache_ckv, cache_kpe_t)


def _head_sums(x, bd):
    hi = x.astype(bf16)
    lo = (x - hi.astype(f32)).astype(bf16)
    parts = []
    for g in range(x.shape[1] // MXU_DIM):
        sl = slice(g * MXU_DIM, (g + 1) * MXU_DIM)
        parts.append(_dot(hi[:, sl], bd) + _dot(lo[:, sl], bd))
    return jnp.concatenate(parts, axis=1)


def _rwkv_prep_kernel(u_ref, st_ref, mu_ref, w0_ref, a0_ref, kkw_ref, ka_ref, rk_ref, wwa_ref, g2_ref, bd_ref,
                      r_ref, k_ref, v_ref, p_ref, q_ref, w_ref, gc_ref, g_ref, bonus_ref, *, tb, n_prompt_blocks):
    w_ = RWKV_WIDTH
    nst = st_ref.shape[0]
    is_sample = pl.program_id(0) >= n_prompt_blocks
    stride = jnp.where(is_sample, DEC_SEQ, tb)
    row_e = lax.broadcasted_iota(jnp.int32, (tb, nst), 0)
    col_e = lax.broadcasted_iota(jnp.int32, (tb, nst), 1)
    place = jnp.where(row_e == col_e * stride, 1.0, 0.0).astype(bf16)
    starts = _dot_exact_rhs(place, st_ref[...])
    u = u_ref[...]
    row = lax.broadcasted_iota(jnp.int32, (tb, 1), 0)
    is_start = jnp.where(is_sample, row % DEC_SEQ, row) == 0
    prev = jnp.where(is_start, starts, pltpu.roll(u, 1, axis=0))
    z = u + (prev - u) * mu_ref[...]
    r = z[:, :w_]
    k = z[:, w_:2 * w_]
    v = z[:, 2 * w_:3 * w_]
    wa = z[:, 3 * w_:3 * w_ + W_LORA + A_LORA]
    gd = z[:, 3 * w_ + W_LORA + A_LORA:]
    lane = lax.broadcasted_iota(jnp.int32, wa.shape, 1)
    wa = jnp.where(lane < W_LORA, jnp.tanh(wa), wa)
    lora = _dot(wa.astype(bf16), wwa_ref[...])
    w_log = -jax.nn.softplus(-(w0_ref[...] + lora[:, :w_])) - 0.5
    a = jax.nn.sigmoid(a0_ref[...] + lora[:, w_:])
    g = _dot(jax.nn.sigmoid(gd).astype(bf16), g2_ref[...])
    bd = bd_ref[...]
    kk = k * kkw_ref[...]
    kk = kk / jnp.maximum(jnp.sqrt(_head_sums(kk * kk, bd)), 1e-12)
    k = k * (1.0 + (a - 1.0) * ka_ref[...])
    bonus = _head_sums(r * k * rk_ref[...], bd) * v
    g_ref[...] = g
    bonus_ref[...] = bonus
    wdec = -jnp.exp(w_log)
    shift = jnp.where(is_sample, DEC_SEQ.bit_length() - 1, RWKV_CHUNK.bit_length() - 1)
    row_c = lax.broadcasted_iota(jnp.int32, (tb, tb), 0)
    col_c = lax.broadcasted_iota(jnp.int32, (tb, tb), 1)
    same_chunk = lax.shift_right_logical(row_c, shift) == lax.shift_right_logical(col_c, shift)
    tril_blk = jnp.where(same_chunk, jnp.where(col_c <= row_c, 1.0, 0.0), 0.0).astype(bf16)
    gcum = _dot_exact_rhs(tril_blk, wdec)
    for h in range(RWKV_HEADS):
        sl = slice(h * RWKV_HEAD, (h + 1) * RWKV_HEAD)
        r_ref[h] = r[:, sl]
        k_ref[h] = k[:, sl]
        v_ref[h] = v[:, sl]
        p_ref[h] = -kk[:, sl]
        q_ref[h] = (kk * a)[:, sl]
        w_ref[h] = wdec[:, sl]
        gc_ref[h] = gcum[:, sl]


def rwkv_prep(u, starts, lp_rows, w_wa, g2, bd, tb):
    m = u.shape[0]
    nst = starts.shape[1]
    rowc = pl.BlockSpec((tb, RWKV_COLS), lambda i: (i, 0))

    def vec(n):
        return pl.BlockSpec((1, n), lambda i: (0, 0))

    def full(a):
        return pl.BlockSpec(a.shape, lambda i: (0, 0))

    headmajor = pl.BlockSpec((RWKV_HEADS, tb, RWKV_HEAD), lambda i: (0, i, 0))
    tokmajor = pl.BlockSpec((tb, RWKV_WIDTH), lambda i: (i, 0))
    hm_shape = jax.ShapeDtypeStruct((RWKV_HEADS, m, RWKV_HEAD), f32)
    tm_shape = jax.ShapeDtypeStruct((m, RWKV_WIDTH), f32)
    return pl.pallas_call(
        functools.partial(_rwkv_prep_kernel, tb=tb, n_prompt_blocks=M_PROMPT // tb),
        grid=(m // tb,),
        in_specs=[rowc, pl.BlockSpec((None, nst, RWKV_COLS), lambda i: (i, 0, 0)), vec(RWKV_COLS)]
        + [vec(RWKV_WIDTH)] * 5 + [full(w_wa), full(g2), full(bd)],
        out_specs=[headmajor] * 7 + [tokmajor] * 2,
        out_shape=[hm_shape] * 7 + [tm_shape] * 2,
        compiler_params=_cparams("parallel"),
        name="rwkv_prep",
    )(u, starts, *lp_rows, w_wa, g2, bd)


def _tri_consts(csz):
    row = lax.broadcasted_iota(jnp.int32, (csz, csz), 0)
    col = lax.broadcasted_iota(jnp.int32, (csz, csz), 1)
    return (col <= row).astype(f32), (col < row).astype(f32), (col == row).astype(f32)


def _rwkv_phase1_kernel(r_ref, k_ref, v_ref, p_ref, q_ref, w_ref, g_ref, rhat_ref, yint_ref, mlow_ref, nn_ref, e_ref,
                        *, csz, nck):
    tril_incl, tril_strict, eye = _tri_consts(csz)
    cs = range(nck)
    sls = [slice(c * csz, (c + 1) * csz) for c in cs]

    def stack(a, b):
        return jnp.concatenate([a, b], axis=0).astype(bf16)

    pr, qt, kt, p0, r0, qh, kh, vb, e_last = [], [], [], [], [], [], [], [], []
    for sl in sls:
        r, k, v, p, q = r_ref[sl, :], k_ref[sl, :], v_ref[sl, :], p_ref[sl, :], q_ref[sl, :]
        g = g_ref[sl, :]
        gp = g - w_ref[sl, :]
        ref = g[csz // 2 - 1:csz // 2, :]
        gl = g[csz - 1:csz, :]
        e_in = jnp.exp(ref - g)
        e_out = jnp.exp(gl - g)
        pr.append(stack(p * jnp.exp(gp - ref), r * jnp.exp(g - ref)))
        qt.append((q * e_in).astype(bf16))
        kt.append((k * e_in).astype(bf16))
        p0.append((p * jnp.exp(gp)).astype(bf16))
        r0.append(r * jnp.exp(g))
        qh.append((q * e_out).astype(bf16))
        kh.append((k * e_out).astype(bf16))
        vb.append(v.astype(bf16))
        e_last.append(jnp.exp(gl))
    gq = [_dot_nt(pr[c], qt[c]) for c in cs]
    gk = [_dot_nt(pr[c], kt[c]) for c in cs]
    a_qp = [gq[c][:csz] * tril_strict for c in cs]
    a_kp = [(gk[c][:csz] * tril_strict).astype(bf16) for c in cs]
    a_qr = [(gq[c][csz:] * tril_incl).astype(bf16) for c in cs]
    a_kr = [(gk[c][csz:] * tril_incl).astype(bf16) for c in cs]
    tinv = [eye + a_qp[c] for c in cs]
    if csz > 2:
        apb = [a_qp[c].astype(bf16) for c in cs]
        apow = [_dot(apb[c], apb[c]) for c in cs]
        n = 2
        while n < csz:
            last = 2 * n >= csz
            rhs = [apow[c].astype(bf16) for c in cs]
            if last:
                prod = [_dot(tinv[c].astype(bf16), rhs[c]) for c in cs]
                tinv = [tinv[c] + prod[c] for c in cs]
            else:
                prod = [_dot(stack(tinv[c], apow[c]), rhs[c]) for c in cs]
                tinv = [tinv[c] + prod[c][:csz] for c in cs]
                apow = [prod[c][csz:] for c in cs]
            n *= 2
    tb_ = [tinv[c].astype(bf16) for c in cs]
    akpv = [_dot(a_kp[c], vb[c]).astype(bf16) for c in cs]
    phb = [_dot(tb_[c], p0[c]).astype(bf16) for c in cs]
    wvb = [_dot(tb_[c], akpv[c]).astype(bf16) for c in cs]
    rhat = [r0[c] + _dot(a_qr[c], phb[c]) for c in cs]
    yint = [_dot(a_qr[c], wvb[c]) + _dot(a_kr[c], vb[c]) for c in cs]
    mlow = [_dot_tn(phb[c], qh[c]) for c in cs]
    nn = [_dot_tn(wvb[c], qh[c]) + _dot_tn(vb[c], kh[c]) for c in cs]
    for c in cs:
        rhat_ref[sls[c], :] = rhat[c]
        yint_ref[sls[c], :] = yint[c]
        mlow_ref[c] = mlow[c]
        nn_ref[c] = nn[c]
        e_ref[c] = jnp.broadcast_to(e_last[c], (8, RWKV_HEAD))


def rwkv_phase1(r, k, v, p, q, w, g, row0, rows, csz, nck):
    tb = nck * csz
    assert rows % tb == 0 and row0 % tb == 0
    off = row0 // tb
    nchunks = rows // csz
    hm_in = pl.BlockSpec((None, tb, RWKV_HEAD), lambda h, i: (h, i + off, 0))
    hm_out = pl.BlockSpec((None, tb, RWKV_HEAD), lambda h, i: (h, i, 0))
    sq = pl.BlockSpec((None, nck, RWKV_HEAD, RWKV_HEAD), lambda h, i: (h, i, 0, 0))
    ev = pl.BlockSpec((None, nck, 8, RWKV_HEAD), lambda h, i: (h, i, 0, 0))
    return pl.pallas_call(
        functools.partial(_rwkv_phase1_kernel, csz=csz, nck=nck),
        grid=(RWKV_HEADS, rows // tb),
        in_specs=[hm_in] * 7,
        out_specs=[hm_out, hm_out, sq, sq, ev],
        out_shape=[jax.ShapeDtypeStruct((RWKV_HEADS, rows, RWKV_HEAD), f32)] * 2
        + [jax.ShapeDtypeStruct((RWKV_HEADS, nchunks, RWKV_HEAD, RWKV_HEAD), f32)] * 2
        + [jax.ShapeDtypeStruct((RWKV_HEADS, nchunks, 8, RWKV_HEAD), f32)],
        compiler_params=_cparams("parallel", "parallel"),
        name="rwkv_phase1",
    )(r, k, v, p, q, w, g)


def _rwkv_phase2_kernel(*refs, csz, nck, n_alias):
    s0_ref, rhat_ref, yint_ref, mlow_ref, nn_ref, e_ref = refs[:6]
    y_ref, sout_ref, s_sc = refs[6 + n_alias:]
    ci = pl.program_id(1)
    hs = range(RWKV_HEADS)

    @pl.when(ci == 0)
    def _():
        s_sc[...] = s0_ref[...]

    s = [s_sc[h] for h in hs]
    for c in range(nck):
        sl = slice(c * csz, (c + 1) * csz)
        sb = [s[h].astype(bf16) for h in hs]
        ys = [_dot_nt(rhat_ref[h, sl, :].astype(bf16), sb[h]) for h in hs]
        sm = [_dot(sb[h], mlow_ref[h, c].astype(bf16)) for h in hs]
        for h in hs:
            y_ref[h, sl, :] = ys[h] + yint_ref[h, sl, :]
        s = [s[h] * e_ref[h, c][:1, :] + sm[h] + nn_ref[h, c] for h in hs]
    for h in hs:
        s_sc[h] = s[h]

    @pl.when(ci == pl.num_programs(1) - 1)
    def _():
        sout_ref[...] = s_sc[...]


def rwkv_phase2(s0, rhat, yint, mlow, nn, e, nseq, tlen, csz, layer=None, s_stack=None):
    cps = tlen // csz
    nck = min(cps, 4)
    assert cps % nck == 0
    nsteps = cps // nck
    tb = nck * csz
    hm = pl.BlockSpec((RWKV_HEADS, tb, RWKV_HEAD), lambda s, c: (0, s * nsteps + c, 0))
    sq = pl.BlockSpec((RWKV_HEADS, nck, RWKV_HEAD, RWKV_HEAD), lambda s, c: (0, s * nsteps + c, 0, 0))
    ev = pl.BlockSpec((RWKV_HEADS, nck, 8, RWKV_HEAD), lambda s, c: (0, s * nsteps + c, 0, 0))
    s_shape = (RWKV_HEADS, RWKV_HEAD, RWKV_HEAD)
    if layer is None:
        st_in = st_out = pl.BlockSpec((None,) + s_shape, lambda s, c: (s, 0, 0, 0))
        st_out_shape = jax.ShapeDtypeStruct((nseq,) + s_shape, f32)
        extra_specs, extra_args = [], []
    else:
        st_in = pl.BlockSpec((None, None) + s_shape, lambda s, c: (layer, s, 0, 0, 0))
        st_out, st_out_shape, extra_specs, extra_args = _stacked_state_out(layer, s_stack, nseq, s_shape)
    n_in = 6
    return pl.pallas_call(
        functools.partial(_rwkv_phase2_kernel, csz=csz, nck=nck, n_alias=len(extra_args)),
        grid=(nseq, nsteps),
        in_specs=[st_in, hm, hm, sq, sq, ev] + extra_specs,
        out_specs=[hm, st_out],
        out_shape=[jax.ShapeDtypeStruct((RWKV_HEADS, nseq * tlen, RWKV_HEAD), f32), st_out_shape],
        scratch_shapes=[pltpu.VMEM(s_shape, f32)],
        input_output_aliases={n_in: 1} if extra_args else {},
        compiler_params=_cparams("parallel", "arbitrary"),
        name="rwkv_phase2",
    )(s0, rhat, yint, mlow, nn, e, *extra_args)


def _rwkv_out_kernel(y_ref, g_ref, bonus_ref, lnw_ref, lnb_ref, bd_ref, o_ref, y_sc):
    for h in range(RWKV_HEADS):
        y_sc[:, h * RWKV_HEAD:(h + 1) * RWKV_HEAD] = y_ref[h]
    y = y_sc[...]
    bd = bd_ref[...]
    mean = _head_sums(y, bd) * (1.0 / RWKV_HEAD)
    yc = y - mean
    var = _head_sums(yc * yc, bd) * (1.0 / RWKV_HEAD)
    yn = yc * lax.rsqrt(var + RWKV_GN_EPS) * lnw_ref[...] + lnb_ref[...]
    o_ref[...] = ((yn + bonus_ref[...]) * g_ref[...]).astype(o_ref.dtype)


def rwkv_out(y_hm, g, bonus, ln_w, ln_b, bd):
    m = g.shape[0]
    tb = _pick(m, (256, 128))
    tok = pl.BlockSpec((tb, RWKV_WIDTH), lambda i: (i, 0))
    vec = pl.BlockSpec((1, RWKV_WIDTH), lambda i: (0, 0))
    return pl.pallas_call(
        _rwkv_out_kernel,
        grid=(m // tb,),
        in_specs=[pl.BlockSpec((RWKV_HEADS, tb, RWKV_HEAD), lambda i: (0, i, 0)), tok, tok, vec, vec,
                  pl.BlockSpec(bd.shape, lambda i: (0, 0))],
        out_specs=tok,
        out_shape=jax.ShapeDtypeStruct((m, RWKV_WIDTH), bf16),
        scratch_shapes=[pltpu.VMEM((tb, RWKV_WIDTH), f32)],
        compiler_params=_cparams("parallel"),
        name="rwkv_out",
    )(y_hm, g, bonus, ln_w.reshape(1, -1), ln_b.reshape(1, -1), bd)


def _mlstm_kernel(*refs, csz, n_alias):
    u_ref, ug_ref, gb_ref, ng_ref, c0_ref, n0_ref, m0_ref = refs[:7]
    y_ref, cout_ref, nout_ref, mout_ref, c_sc, n_sc, m_sc = refs[7 + n_alias:]
    ci = pl.program_id(1)
    dh = MLSTM_DH
    w_ = MLSTM_WIDTH

    @pl.when(ci == 0)
    def _():
        c_sc[...] = c0_ref[...]
        n_sc[...] = n0_ref[...]
        m_sc[...] = m0_ref[...]

    tril_incl, _, _ = _tri_consts(csz)
    causal = tril_incl > 0.0
    gates = ug_ref[...] + gb_ref[...]
    lane = lax.broadcasted_iota(jnp.int32, gates.shape, 1)
    logf = jax.nn.log_sigmoid(gates)
    fcum = _dot_exact_rhs(tril_incl.astype(bf16), logf)
    cols = jnp.where(lane < MLSTM_HEADS, gates, fcum)
    sel = (lax.broadcasted_iota(jnp.int32, (8, LANES), 0) == lax.broadcasted_iota(jnp.int32, (8, LANES), 1))
    hi, mid, lo = _split3(cols)
    selb = sel.astype(bf16)
    rows = _dot_nt(selb, hi) + _dot_nt(selb, mid) + _dot_nt(selb, lo)
    hs = range(MLSTM_HEADS)
    q = [u_ref[:, h * dh:(h + 1) * dh] for h in hs]
    k = [u_ref[:, w_ + h * dh:w_ + (h + 1) * dh] * (dh ** -0.5) for h in hs]
    v = [u_ref[:, 2 * w_ + h * dh:2 * w_ + (h + 1) * dh] for h in hs]
    qb = [q[h].astype(bf16) for h in hs]
    kb = [k[h].astype(bf16) for h in hs]
    vb = [v[h].astype(bf16) for h in hs]
    c_old = [c_sc[h] for h in hs]
    n_old = [n_sc[h:h + 1, :] for h in hs]
    m_old = [m_sc[h:h + 1, :1] for h in hs]
    ig_col = [cols[:, h:h + 1] for h in hs]
    f_col = [cols[:, MLSTM_HEADS + h:MLSTM_HEADS + h + 1] for h in hs]
    d_ts = [jnp.where(causal, f_col[h] - rows[MLSTM_HEADS + h:MLSTM_HEADS + h + 1, :] + rows[h:h + 1, :], -jnp.inf)
            for h in hs]
    inter = [m_old[h] + f_col[h] for h in hs]
    m_t = [jnp.maximum(inter[h], jnp.max(d_ts[h], axis=1, keepdims=True)) for h in hs]
    w_inter = [jnp.exp(inter[h] - m_t[h]) for h in hs]
    qk = [_dot_nt(qb[h], kb[h]) for h in hs]
    cq = [_dot_nt(qb[h], c_old[h].astype(bf16)) for h in hs]
    a = [jnp.exp(d_ts[h] - m_t[h]) * qk[h] for h in hs]
    av = [_dot(a[h].astype(bf16), vb[h]) for h in hs]
    m_new = [m_t[h][csz - 1:csz, :] for h in hs]
    f_last = [f_col[h][csz - 1:csz, :] for h in hs]
    carry = [jnp.exp(m_old[h] + f_last[h] - m_new[h]) for h in hs]
    w_write = [jnp.exp(f_last[h] - f_col[h] + ig_col[h] - m_new[h]) for h in hs]
    vk = [_dot_tn((v[h] * w_write[h]).astype(bf16), kb[h]) for h in hs]
    for h in hs:
        num = w_inter[h] * cq[h] + av[h]
        den = w_inter[h] * jnp.sum(q[h] * n_old[h], axis=1, keepdims=True) + jnp.sum(a[h], axis=1, keepdims=True)
        hh = num / jnp.maximum(jnp.abs(den), jnp.exp(-m_t[h]))
        hh = hh * lax.rsqrt(jnp.mean(hh * hh, axis=1, keepdims=True) + NORM_EPS)
        og = jax.nn.sigmoid(u_ref[:, 3 * w_ + h * dh:3 * w_ + (h + 1) * dh])
        y_ref[:, h * dh:(h + 1) * dh] = (og * hh * ng_ref[:, h * dh:(h + 1) * dh]).astype(y_ref.dtype)
        c_sc[h] = carry[h] * c_old[h] + vk[h]
        n_sc[h:h + 1, :] = carry[h] * n_old[h] + jnp.sum(k[h] * w_write[h], axis=0, keepdims=True)
        m_sc[h:h + 1, :] = jnp.broadcast_to(m_new[h], (1, LANES))

    @pl.when(ci == pl.num_programs(1) - 1)
    def _():
        cout_ref[...] = c_sc[...]
        nout_ref[...] = n_sc[...]
        mout_ref[...] = m_sc[...]


def _stacked_state_out(layer, stack, nseq, shape):
    spec = pl.BlockSpec((None, None) + shape, lambda s, c: (layer, s) + (0,) * len(shape))
    out_shape = jax.ShapeDtypeStruct((DEPTH, nseq) + shape, f32)
    assert stack.shape == out_shape.shape
    return spec, out_shape, [pl.BlockSpec(memory_space=pl.ANY)], [stack]


def mlstm(u, ug, gate_bias_row, norm_g, c0, n0, m0, row0, nseq, tlen, csz, layer=None, c_stack=None):
    cps = tlen // csz
    assert row0 % csz == 0
    off = row0 // csz

    def rowspec(cols):
        return pl.BlockSpec((csz, cols), lambda s, c: (off + s * cps + c, 0))

    def state(shape):
        return pl.BlockSpec((None,) + shape, lambda s, c: (s,) + (0,) * len(shape))

    c_shape = (MLSTM_HEADS, MLSTM_DH, MLSTM_DH)
    if layer is None:
        c_in, c_out, c_out_shape = state(c_shape), state(c_shape), jax.ShapeDtypeStruct((nseq,) + c_shape, f32)
        extra_specs, extra_args = [], []
    else:
        c_in = pl.BlockSpec((None, None) + c_shape, lambda s, c: (layer, s, 0, 0, 0))
        c_out, c_out_shape, extra_specs, extra_args = _stacked_state_out(layer, c_stack, nseq, c_shape)
    n_in = 7
    return pl.pallas_call(
        functools.partial(_mlstm_kernel, csz=csz, n_alias=len(extra_args)),
        grid=(nseq, cps),
        in_specs=[rowspec(4 * MLSTM_WIDTH), rowspec(LANES), pl.BlockSpec((1, LANES), lambda s, c: (0, 0)),
                  pl.BlockSpec((1, MLSTM_WIDTH), lambda s, c: (0, 0)),
                  c_in, state((8, MLSTM_DH)), state((8, LANES))] + extra_specs,
        out_specs=[pl.BlockSpec((csz, MLSTM_WIDTH), lambda s, c: (s * cps + c, 0)),
                   c_out, state((8, MLSTM_DH)), state((8, LANES))],
        out_shape=[jax.ShapeDtypeStruct((nseq * tlen, MLSTM_WIDTH), f32), c_out_shape,
                   jax.ShapeDtypeStruct((nseq, 8, MLSTM_DH), f32), jax.ShapeDtypeStruct((nseq, 8, LANES), f32)],
        scratch_shapes=[pltpu.VMEM((MLSTM_HEADS, MLSTM_DH, MLSTM_DH), f32), pltpu.VMEM((8, MLSTM_DH), f32),
                        pltpu.VMEM((8, LANES), f32)],
        input_output_aliases={n_in: 1} if extra_args else {},
        compiler_params=_cparams("parallel", "arbitrary"),
        name="mlstm",
    )(u, ug, gate_bias_row, norm_g.reshape(1, -1), c0, n0, m0, *extra_args)


def _column_plan():
    per_head = MLA_NOPE + MLA_ROPE
    q_nope = [h * per_head + d for h in range(MLA_HEADS) for d in range(MLA_NOPE)]
    q_rope = []
    for h in range(MLA_HEADS):
        base = h * per_head + MLA_NOPE
        x1 = [base + e for e in range(ROPE_HALF)]
        x2 = [base + ROPE_HALF + e for e in range(ROPE_HALF)]
        q_rope += x1 + x2 + x2 + x1
    kv_base = MLA_Q_COLS
    ckv = [kv_base + c for c in range(KV_RANK)]
    x1 = [kv_base + KV_RANK + e for e in range(ROPE_HALF)]
    x2 = [kv_base + KV_RANK + ROPE_HALF + e for e in range(ROPE_HALF)]
    return np.asarray(q_nope + q_rope, np.int32), np.asarray(ckv + x1 + x2 + x2 + x1, np.int32)


def _rope_tables():
    pos = jnp.concatenate([jnp.tile(jnp.arange(SEQ), BATCH), jnp.tile(PAST_LEN + jnp.arange(DEC_SEQ), DEC_BATCH)])
    inv = ROPE_THETA ** (-jnp.arange(ROPE_HALF, dtype=f32) / ROPE_HALF)
    ang = pos.astype(f32)[:, None] * inv[None, :]
    cos, sin = jnp.cos(ang), jnp.sin(ang)
    zero = jnp.zeros_like(cos)
    return jnp.concatenate([cos, cos, zero, zero], axis=1), jnp.concatenate([-sin, sin, zero, zero], axis=1)


def _pad8(x, rows_axis):
    pad = [(0, 0)] * x.ndim
    pad[rows_axis] = (0, 8 - x.shape[rows_axis])
    return jnp.pad(x, pad)


def kernel(x_prompt, x_sample, cache_ckv, cache_kpe, page_table, state_rwkv_shift, state_rwkv_S, state_mlstm_C, state_mlstm_n, state_mlstm_m, norm_gains, w_in, g_ckv, w_uk, w_uv, rwkv_mu, rwkv_w0, rwkv_w2, rwkv_a0, rwkv_a2, rwkv_g2, rwkv_kk, rwkv_ka, rwkv_rk, rwkv_ln_w, rwkv_ln_b, mlstm_gate_b, mlstm_norm, gate_b, w_br_mla, w_br_rwkv, w_br_mlstm, w_out, w_up, w_down):
    d = D_MODEL
    o0 = MLA_Q_COLS
    o1 = o0 + MLA_KV_COLS
    o2 = o1 + RWKV_COLS
    o3 = o2 + MLSTM_COLS
    q_cols, kv_cols = _column_plan()
    c4, s4 = _rope_tables()
    head_id = np.arange(MXU_DIM) // RWKV_HEAD
    bd = jnp.asarray(head_id[:, None] == head_id[None, :], bf16)
    zeros_s = jnp.zeros((BATCH, RWKV_HEADS, RWKV_HEAD, RWKV_HEAD), f32)
    zeros_c = jnp.zeros((BATCH, MLSTM_HEADS, MLSTM_DH, MLSTM_DH), f32)
    zeros_n = jnp.zeros((BATCH, 8, MLSTM_DH), f32)
    zeros_m = jnp.zeros((BATCH, 8, LANES), f32)
    tq = _pick(SEQ, (512, 256))
    tb_rw = 256
    n_starts = tb_rw // DEC_SEQ
    assert SEQ % tb_rw == 0 and M_SAMPLE % tb_rw == 0
    cache_kpe_t = jnp.swapaxes(cache_kpe, 2, 3)

    x = jnp.concatenate([x_prompt.reshape(M_PROMPT, d), x_sample.reshape(M_SAMPLE, d)], axis=0)
    h, h_lo = rmsnorm_bf16(x, norm_gains[0, 0])
    prompt_states, sample_states = [], []
    s_stack = jnp.zeros((DEPTH, DEC_BATCH, RWKV_HEADS, RWKV_HEAD, RWKV_HEAD), f32)
    c_stack = jnp.zeros((DEPTH, DEC_BATCH, MLSTM_HEADS, MLSTM_DH, MLSTM_DH), f32)
    for l in range(DEPTH):
        w = w_in[l]
        w_q = jnp.take(w, q_cols, axis=1).astype(bf16)
        w_kv = jnp.take(w, kv_cols, axis=1).astype(bf16)
        w_rw = w[:, o1:o2].astype(bf16)
        w_ml = w[:, o2:o2 + 4 * MLSTM_WIDTH].astype(bf16)
        w_mg = jnp.pad(w[:, o2 + 4 * MLSTM_WIDTH:o3], ((0, 0), (0, LANES - 2 * MLSTM_HEADS)))
        w_gt = w[:, o3:].astype(bf16)

        u_q = matmul(h, w_q)
        u_kv = matmul(h, w_kv)
        u_rw = matmul(h, w_rw)
        u_ml = matmul(h, w_ml)
        u_mg = matmul_split(h, h_lo, w_mg)
        gates = matmul(h, w_gt, out_dtype=bf16, bias=gate_b[l].reshape(-1))

        w_uk_t = jnp.transpose(w_uk[l], (1, 2, 0)).astype(bf16)
        w_uv_h = jnp.transpose(w_uv[l], (1, 0, 2)).astype(bf16)
        ql_p, qp_p, kc_p, kpe_p, ckv_p, kpef_p = mla_prep(u_q, u_kv, c4, s4, w_uk_t, g_ckv[l], 0, M_PROMPT, bf16)
        ql_s, qp_s, _, _, ckv_s, kpef_s = mla_prep(u_q, u_kv, c4, s4, w_uk_t, g_ckv[l], M_PROMPT, M_SAMPLE, f32)
        y_mla_p = attn_prompt(ql_p, qp_p, kc_p, kpe_p, w_uv_h, tq)
        y_mla_s = attn_sample(ql_s, qp_s, ckv_s, kpef_s, w_uv_h, cache_ckv, cache_kpe_t, page_table, l)
        y_mla = jnp.concatenate([y_mla_p, y_mla_s.astype(bf16)], axis=0)

        block_last = u_rw[tb_rw - 1:M_PROMPT:tb_rw].reshape(BATCH, SEQ // tb_rw, 1, RWKV_COLS)
        before = jnp.concatenate([jnp.zeros((BATCH, 1, 1, RWKV_COLS), f32), block_last[:, :-1]], axis=1)
        starts = jnp.concatenate([
            jnp.pad(before.reshape(M_PROMPT // tb_rw, 1, RWKV_COLS), ((0, 0), (0, n_starts - 1), (0, 0))),
            state_rwkv_shift[l].reshape(M_SAMPLE // tb_rw, n_starts, RWKV_COLS)], axis=0)
        zero_blk = jnp.zeros((W_LORA, RWKV_WIDTH), f32)
        w_wa = jnp.concatenate([jnp.concatenate([rwkv_w2[l], zero_blk], axis=1),
                                jnp.concatenate([zero_blk, rwkv_a2[l]], axis=1)], axis=0).astype(bf16)
        lp_rows = [rwkv_mu[l].reshape(1, -1), rwkv_w0[l].reshape(1, -1), rwkv_a0[l].reshape(1, -1),
                   rwkv_kk[l].reshape(1, -1), rwkv_ka[l].reshape(1, -1), rwkv_rk[l].reshape(1, -1)]
        ops = rwkv_prep(u_rw, starts, lp_rows, w_wa, rwkv_g2[l].astype(bf16), bd, tb_rw)
        hm_ops, g_, bonus = ops[:7], ops[7], ops[8]
        t_p = rwkv_phase1(*hm_ops, 0, M_PROMPT, RWKV_CHUNK, 16)
        t_s = rwkv_phase1(*hm_ops, M_PROMPT, M_SAMPLE, DEC_SEQ, 32)
        y_p, s_p = rwkv_phase2(zeros_s, *t_p, BATCH, SEQ, RWKV_CHUNK)
        y_s, s_stack = rwkv_phase2(state_rwkv_S, *t_s, DEC_BATCH, DEC_SEQ, DEC_SEQ, layer=l, s_stack=s_stack)
        y_rwkv = rwkv_out(jnp.concatenate([y_p, y_s], axis=1), g_, bonus, rwkv_ln_w[l], rwkv_ln_b[l], bd)

        gb_row = jnp.pad(mlstm_gate_b[l].reshape(1, -1), ((0, 0), (0, LANES - 2 * MLSTM_HEADS)))
        ym_p, c_p, n_p, m_p = mlstm(u_ml, u_mg, gb_row, mlstm_norm[l], zeros_c, zeros_n, zeros_m, 0, BATCH, SEQ, MLSTM_CHUNK)
        m0_s = jnp.broadcast_to(_pad8(state_mlstm_m[l], 1)[:, :, None], (DEC_BATCH, 8, LANES))
        ym_s, c_stack, n_s, m_s = mlstm(u_ml, u_mg, gb_row, mlstm_norm[l], state_mlstm_C, _pad8(state_mlstm_n[l], 1), m0_s,
                                        M_PROMPT, DEC_BATCH, DEC_SEQ, DEC_SEQ, layer=l, c_stack=c_stack)
        y_mlstm = jnp.concatenate([ym_p, ym_s], axis=0).astype(bf16)

        merged = merge_branches(y_mla, y_rwkv, y_mlstm, w_br_mla, w_br_rwkv, w_br_mlstm, gates, l)
        attn_out = matmul(merged, w_out, layer=l)
        x, hf = resnorm(x, attn_out, norm_gains[l, 1], norm_gains[l, 2], False)
        ff = matmul_ksplit(matmul(hf, w_up, out_dtype=bf16, act="relu2", layer=l), w_down, layer=l)
        g_next = norm_gains[l + 1, 0] if l + 1 < DEPTH else norm_gains[l, 3]
        x, h, h_lo = resnorm(x, ff, norm_gains[l, 3], g_next, True)

        shift_p = u_rw[SEQ - 1:M_PROMPT:SEQ]
        shift_s = u_rw[M_PROMPT + DEC_SEQ - 1::DEC_SEQ]
        prompt_states.append((ckv_p.reshape(BATCH, SEQ, KV_RANK), kpef_p.reshape(BATCH, SEQ, MLA_ROPE), shift_p, s_p,
                              c_p, n_p[:, :MLSTM_HEADS], m_p[:, :MLSTM_HEADS, 0]))
        sample_states.append((ckv_s.reshape(DEC_BATCH, DEC_SEQ, KV_RANK), kpef_s.reshape(DEC_BATCH, DEC_SEQ, MLA_ROPE),
                              shift_s, None, None, n_s[:, :MLSTM_HEADS], m_s[:, :MLSTM_HEADS, 0]))

    outs = [x[:M_PROMPT].reshape(BATCH, SEQ, d), x[M_PROMPT:].reshape(DEC_BATCH, DEC_SEQ, d)]
    for states, stacked in ((prompt_states, {}), (sample_states, {3: s_stack, 4: c_stack})):
        for i in range(7):
            outs.append(stacked[i] if i in stacked else jnp.stack([st[i] for st in states]))
    return tuple(outs)
```

```python
import functools

import jax
import jax.numpy as jnp
import numpy as np
from jax import lax
from jax.experimental import pallas as pl
from jax.experimental.pallas import tpu as pltpu

D_MODEL = 2048
BATCH = 2
SEQ = 4096
DEPTH = 2
DEC_BATCH = 128
DEC_SEQ = 8
PAST_LEN = 16384
PAGE_SIZE = 128

MLA_HEADS = 8
MLA_NOPE = 128
MLA_ROPE = 64
MLA_V = 128
KV_RANK = 256
ROPE_THETA = 10000.0
MLA_SCALE = (MLA_NOPE + MLA_ROPE) ** -0.5
RWKV_HEAD = 64
RWKV_WIDTH = 1024
RWKV_HEADS = RWKV_WIDTH // RWKV_HEAD
W_LORA = 64
A_LORA = 64
G_LORA = 128
RWKV_GN_EPS = 64e-5
MLSTM_HEADS = 4
MLSTM_DH = 256
MLSTM_WIDTH = MLSTM_HEADS * MLSTM_DH
MLSTM_CHUNK = 64
N_BRANCH = 3
D_FF = 4 * D_MODEL
NORM_EPS = 1e-6

MLA_Q_COLS = MLA_HEADS * (MLA_NOPE + MLA_ROPE)
MLA_KV_COLS = KV_RANK + MLA_ROPE
RWKV_COLS = 3 * RWKV_WIDTH + W_LORA + A_LORA + G_LORA
MLSTM_COLS = 4 * MLSTM_WIDTH + 2 * MLSTM_HEADS
GATE_COLS = N_BRANCH * D_MODEL

LANES = 128
MXU_DIM = 256
ROPE_HALF = MLA_ROPE // 2
QL_COLS = MLA_HEADS * KV_RANK
QP_COLS = MLA_HEADS * LANES
UQ_COLS = MLA_HEADS * MLA_NOPE + QP_COLS
UKV_COLS = KV_RANK + LANES
N_PAGES = PAST_LEN // PAGE_SIZE
PAGES_PER_STEP = 8
SEQS_PER_STEP = 4
RWKV_CHUNK = 64
M_PROMPT = BATCH * SEQ
M_SAMPLE = DEC_BATCH * DEC_SEQ
M_TOK = M_PROMPT + M_SAMPLE
VMEM_LIMIT = 56 * 1024 * 1024

f32 = jnp.float32
bf16 = jnp.bfloat16


def _cparams(*sem):
    return pltpu.CompilerParams(dimension_semantics=sem, vmem_limit_bytes=VMEM_LIMIT)


def _pick(n, cands):
    for c in cands:
        if n % c == 0:
            return c
    raise ValueError(f"no tile for {n} in {cands}")


def _dot(a, b):
    return jnp.dot(a, b, preferred_element_type=f32)


def _dot_nt(a, b):
    return lax.dot_general(a, b, (((1,), (1,)), ((), ())), preferred_element_type=f32)


def _dot_tn(a, b):
    return lax.dot_general(a, b, (((0,), (0,)), ((), ())), preferred_element_type=f32)


def _split3(x):
    hi = x.astype(bf16)
    r1 = x - hi.astype(f32)
    mid = r1.astype(bf16)
    lo = (r1 - mid.astype(f32)).astype(bf16)
    return hi, mid, lo


def _dot_exact_rhs(sel, x):
    hi, mid, lo = _split3(x)
    return _dot(sel, hi) + _dot(sel, mid) + _dot(sel, lo)


def _dot_exact_lhs(x, sel):
    hi, mid, lo = _split3(x)
    return _dot(hi, sel) + _dot(mid, sel) + _dot(lo, sel)


def _rms(x, g):
    return x * lax.rsqrt(jnp.mean(x * x, axis=-1, keepdims=True) + NORM_EPS) * g


def _emit_norm(hn, h_ref, lo_ref):
    hb = hn.astype(bf16)
    h_ref[...] = hb
    if lo_ref is not None:
        lo_ref[...] = (hn - hb.astype(f32)).astype(bf16)


def _norm_kernel(x_ref, g_ref, h_ref, lo_ref):
    _emit_norm(_rms(x_ref[...], g_ref[...]), h_ref, lo_ref)


def rmsnorm_bf16(x, g):
    m, d = x.shape
    tm = _pick(m, (512, 256))
    row = pl.BlockSpec((tm, d), lambda i: (i, 0))
    return pl.pallas_call(
        _norm_kernel,
        grid=(m // tm,),
        in_specs=[row, pl.BlockSpec((1, d), lambda i: (0, 0))],
        out_specs=[row, row],
        out_shape=[jax.ShapeDtypeStruct((m, d), bf16)] * 2,
        compiler_params=_cparams("parallel"),
        name="rmsnorm",
    )(x, g.reshape(1, d))


def _resnorm_kernel(x_ref, y_ref, gp_ref, gn_ref, xo_ref, h_ref, *lo_ref):
    xn = x_ref[...] + _rms(y_ref[...], gp_ref[...])
    xo_ref[...] = xn
    _emit_norm(_rms(xn, gn_ref[...]), h_ref, lo_ref[0] if lo_ref else None)


def resnorm(x, y, g_post, g_next, with_lo):
    m, d = x.shape
    tm = _pick(m, (512, 256))
    row = pl.BlockSpec((tm, d), lambda i: (i, 0))
    gain = pl.BlockSpec((1, d), lambda i: (0, 0))
    n_h = 2 if with_lo else 1
    return pl.pallas_call(
        _resnorm_kernel,
        grid=(m // tm,),
        in_specs=[row, row, gain, gain],
        out_specs=[row] * (1 + n_h),
        out_shape=[jax.ShapeDtypeStruct((m, d), f32)] + [jax.ShapeDtypeStruct((m, d), bf16)] * n_h,
        compiler_params=_cparams("parallel"),
        name="resnorm",
    )(x, y, g_post.reshape(1, d), g_next.reshape(1, d))


def _weights(b_ref):
    b = b_ref[...]
    return b if b.dtype == bf16 else b.astype(bf16)


def _weight_spec(layer, shape, index_map):
    if layer is None:
        return pl.BlockSpec(shape, index_map)
    return pl.BlockSpec((None,) + shape, lambda *g: (layer,) + index_map(*g))


def _mm(a_ref, b_ref, b_transposed):
    b = _weights(b_ref)
    if b_transposed:
        return _dot_nt(a_ref[...], b.reshape(b.shape[-2:]))
    return _dot(a_ref[...], b)


def _mm_kernel(a_ref, b_ref, o_ref, *, act, b_transposed):
    acc = _mm(a_ref, b_ref, b_transposed)
    if act == "relu2":
        acc = jnp.square(jnp.maximum(acc, 0.0))
    o_ref[...] = acc.astype(o_ref.dtype)


def _mm_bias_sigmoid_kernel(a_ref, b_ref, bias_ref, o_ref, *, b_transposed):
    acc = _mm(a_ref, b_ref, b_transposed) + bias_ref[...]
    o_ref[...] = jax.nn.sigmoid(acc).astype(o_ref.dtype)


def _mm_tiles(m, n, wide_weights):
    tn = _pick(n, (1024, 512, 384, 256, 128)) if n % 1664 else 1664
    tm = _pick(m, (512, 256)) if (wide_weights and tn == 1664) else _pick(m, (1024, 512, 256))
    return tm, tn


def matmul(a, b, out_dtype=f32, act=None, bias=None, layer=None, row0=None, n=None):
    m, k = a.shape
    b_transposed = row0 is not None
    if b_transposed:
        assert row0 % 8 == 0 and b.shape[-1] == k
    else:
        n = b.shape[-1]
    tm, tn = _mm_tiles(m, n, b.dtype == f32)
    if not b_transposed:
        b_spec = _weight_spec(layer, (k, tn), lambda i, j: (0, j))
    elif layer is None:
        assert row0 % tn == 0
        b_spec = pl.BlockSpec((tn, k), lambda i, j: (row0 // tn + j, 0))
    else:
        b_spec = pl.BlockSpec((pl.Element(1), pl.Element(tn), pl.Element(k)),
                              lambda i, j: (layer, 8 * (row0 // 8 + j * (tn // 8)), 0))
    in_specs = [pl.BlockSpec((tm, k), lambda i, j: (i, 0)), b_spec]
    args = [a, b]
    if bias is not None:
        kern = functools.partial(_mm_bias_sigmoid_kernel, b_transposed=b_transposed)
        in_specs.append(pl.BlockSpec((1, tn), lambda i, j: (0, j)))
        args.append(bias.reshape(1, n))
    else:
        kern = functools.partial(_mm_kernel, act=act, b_transposed=b_transposed)
    return pl.pallas_call(
        kern,
        grid=(m // tm, n // tn),
        in_specs=in_specs,
        out_specs=pl.BlockSpec((tm, tn), lambda i, j: (i, j)),
        out_shape=jax.ShapeDtypeStruct((m, n), out_dtype),
        compiler_params=_cparams("parallel", "parallel"),
        name="matmul",
    )(*args)


def _mm3_kernel(a_ref, alo_ref, b_ref, blo_ref, o_ref):
    a = a_ref[...]
    b = b_ref[...]
    o_ref[...] = _dot(a, b) + _dot(alo_ref[...], b) + _dot(a, blo_ref[...])


def matmul_split(a, a_lo, b):
    m, k = a.shape
    n = b.shape[1]
    b_hi = b.astype(bf16)
    b_lo = (b - b_hi.astype(f32)).astype(bf16)
    tm = _pick(m, (1024, 512, 256))
    row = pl.BlockSpec((tm, k), lambda i: (i, 0))
    col = pl.BlockSpec((k, n), lambda i: (0, 0))
    return pl.pallas_call(
        _mm3_kernel,
        grid=(m // tm,),
        in_specs=[row, row, col, col],
        out_specs=pl.BlockSpec((tm, n), lambda i: (i, 0)),
        out_shape=jax.ShapeDtypeStruct((m, n), f32),
        compiler_params=_cparams("parallel"),
        name="matmul_split",
    )(a, a_lo, b_hi, b_lo)


def _mm_acc_kernel(a_ref, b_ref, o_ref, acc_ref):
    kk = pl.program_id(2)

    @pl.when(kk == 0)
    def _():
        acc_ref[...] = jnp.zeros_like(acc_ref)

    acc_ref[...] += _dot(a_ref[...], _weights(b_ref))

    @pl.when(kk == pl.num_programs(2) - 1)
    def _():
        o_ref[...] = acc_ref[...]


def matmul_ksplit(a, b, layer=None):
    m, k = a.shape
    n = b.shape[-1]
    tm = _pick(m, (1024, 512, 256))
    tn = _pick(n, (1024, 512, 256))
    tk = _pick(k, (2048, 1024, 512, 256))
    return pl.pallas_call(
        _mm_acc_kernel,
        grid=(m // tm, n // tn, k // tk),
        in_specs=[pl.BlockSpec((tm, tk), lambda i, j, kk: (i, kk)),
                  _weight_spec(layer, (tk, tn), lambda i, j, kk: (kk, j))],
        out_specs=pl.BlockSpec((tm, tn), lambda i, j, kk: (i, j)),
        out_shape=jax.ShapeDtypeStruct((m, n), f32),
        scratch_shapes=[pltpu.VMEM((tm, tn), f32)],
        compiler_params=_cparams("parallel", "parallel", "arbitrary"),
        name="matmul_ksplit",
    )(a, b)


def _merge_kernel(ya_ref, yr_ref, ym_ref, wa_ref, wr_ref, wm_ref, ga_ref, gr_ref, gm_ref, o_ref):
    acc = ga_ref[...].astype(f32) * _dot(ya_ref[...], _weights(wa_ref))
    acc += gr_ref[...].astype(f32) * _dot(yr_ref[...], _weights(wr_ref))
    acc += gm_ref[...].astype(f32) * _dot(ym_ref[...], _weights(wm_ref))
    o_ref[...] = acc.astype(o_ref.dtype)


def merge_branches(y_mla, y_rwkv, y_mlstm, w_mla, w_rwkv, w_mlstm, gates, layer):
    m = y_mla.shape[0]
    d = w_mla.shape[-1]
    tm = _pick(m, (1024, 512, 256))
    tn = _pick(d, (512, 256))
    nj = d // tn
    ys = [pl.BlockSpec((tm, y.shape[1]), lambda i, j: (i, 0)) for y in (y_mla, y_rwkv, y_mlstm)]
    ws = [_weight_spec(layer, (w.shape[1], tn), lambda i, j: (0, j)) for w in (w_mla, w_rwkv, w_mlstm)]
    gs = [pl.BlockSpec((tm, tn), functools.partial(lambda i, j, b: (i, j + b * nj), b=b)) for b in range(N_BRANCH)]
    return pl.pallas_call(
        _merge_kernel,
        grid=(m // tm, nj),
        in_specs=ys + ws + gs,
        out_specs=pl.BlockSpec((tm, tn), lambda i, j: (i, j)),
        out_shape=jax.ShapeDtypeStruct((m, d), bf16),
        compiler_params=_cparams("parallel", "parallel"),
        name="merge_branches",
    )(y_mla, y_rwkv, y_mlstm, w_mla, w_rwkv, w_mlstm, gates, gates, gates)


def _rope_group(p, c4, s4):
    return p * c4 + pltpu.roll(p, 2 * ROPE_HALF, axis=1) * s4


def _mla_prep_kernel(uq_ref, ukv_ref, c4_ref, s4_ref, wuk_ref, g_ref, ql_ref, qp_ref, kc_ref, kpe_ref, ckv_ref, kpef_ref):
    c4 = c4_ref[...]
    s4 = s4_ref[...]
    nope = MLA_HEADS * MLA_NOPE
    for h in range(MLA_HEADS):
        qn = uq_ref[:, h * MLA_NOPE:(h + 1) * MLA_NOPE].astype(bf16)
        ql = _dot(qn, wuk_ref[h]) * MLA_SCALE
        ql_ref[:, h * KV_RANK:(h + 1) * KV_RANK] = ql.astype(ql_ref.dtype)
        qp = _rope_group(uq_ref[:, nope + h * LANES:nope + (h + 1) * LANES], c4, s4) * MLA_SCALE
        qp_ref[:, h * LANES:(h + 1) * LANES] = qp.astype(qp_ref.dtype)
    ckv = _rms(ukv_ref[:, :KV_RANK], g_ref[...])
    ckv_ref[...] = ckv
    kc_ref[...] = ckv.astype(bf16)
    kpe = _rope_group(ukv_ref[:, KV_RANK:], c4, s4)
    kpe_ref[...] = kpe.astype(bf16)
    kpef_ref[...] = kpe[:, :MLA_ROPE]


def mla_prep(u_q, u_kv, c4, s4, w_uk_t, g_ckv, row0, rows, q_dtype):
    tb = _pick(rows, (256, 128))
    assert row0 % tb == 0
    off = row0 // tb

    def rowspec(cols):
        return pl.BlockSpec((tb, cols), lambda i: (i + off, 0))

    def outspec(cols):
        return pl.BlockSpec((tb, cols), lambda i: (i, 0))

    return pl.pallas_call(
        _mla_prep_kernel,
        grid=(rows // tb,),
        in_specs=[rowspec(UQ_COLS), rowspec(UKV_COLS), rowspec(LANES), rowspec(LANES),
                  pl.BlockSpec((MLA_HEADS, MLA_NOPE, KV_RANK), lambda i: (0, 0, 0)),
                  pl.BlockSpec((1, KV_RANK), lambda i: (0, 0))],
        out_specs=[outspec(QL_COLS), outspec(QP_COLS), outspec(KV_RANK), outspec(LANES), outspec(KV_RANK), outspec(MLA_ROPE)],
        out_shape=[jax.ShapeDtypeStruct((rows, QL_COLS), q_dtype), jax.ShapeDtypeStruct((rows, QP_COLS), q_dtype),
                   jax.ShapeDtypeStruct((rows, KV_RANK), bf16), jax.ShapeDtypeStruct((rows, LANES), bf16),
                   jax.ShapeDtypeStruct((rows, KV_RANK), f32), jax.ShapeDtypeStruct((rows, MLA_ROPE), f32)],
        compiler_params=_cparams("parallel"),
        name="mla_prep",
    )(u_q, u_kv, c4, s4, w_uk_t, g_ckv.reshape(1, KV_RANK))


def _attn_prompt_kernel(ql_ref, qp_ref, kc_ref, kpe_ref, wuv_ref, o_ref, m_sc, l_sc, acc_sc, *, tq, tk):
    qi = pl.program_id(1)
    ki = pl.program_id(2)

    @pl.when(ki == 0)
    def _():
        m_sc[...] = jnp.full_like(m_sc, -jnp.inf)
        l_sc[...] = jnp.zeros_like(l_sc)
        acc_sc[...] = jnp.zeros_like(acc_sc)

    @pl.when(ki <= qi)
    def _():
        kc = kc_ref[...]
        kpe = kpe_ref[...]
        rel = lax.broadcasted_iota(jnp.int32, (tq, tk), 1) - lax.broadcasted_iota(jnp.int32, (tq, tk), 0)
        visible = rel <= (qi * tq - ki * tk)
        for h in range(MLA_HEADS):
            s = _dot_nt(ql_ref[:, h * KV_RANK:(h + 1) * KV_RANK], kc)
            s += _dot_nt(qp_ref[:, h * LANES:(h + 1) * LANES], kpe)
            s = jnp.where(visible, s, -jnp.inf)
            m_prev = m_sc[h]
            m_new = jnp.maximum(m_prev, jnp.max(s, axis=1, keepdims=True))
            alpha = jnp.exp(m_prev - m_new)
            p = jnp.exp(s - m_new[:, :1])
            l_sc[h] = alpha * l_sc[h] + jnp.sum(p, axis=1, keepdims=True)
            acc_sc[h] = acc_sc[h] * alpha[:, :1] + _dot(p.astype(bf16), kc)
            m_sc[h] = m_new

    @pl.when(ki == qi)
    def _():
        for h in range(MLA_HEADS):
            o = acc_sc[h] / l_sc[h][:, :1]
            o_ref[:, h * MLA_V:(h + 1) * MLA_V] = _dot(o.astype(bf16), wuv_ref[h]).astype(o_ref.dtype)


def attn_prompt(ql, qp, kc, kpe, w_uv_h, tq):
    tk = tq
    nq = SEQ // tq

    def qmap(b, qi, ki):
        return (b * nq + qi, 0)

    def kmap(b, qi, ki):
        return (b * nq + jnp.minimum(ki, qi), 0)

    return pl.pallas_call(
        functools.partial(_attn_prompt_kernel, tq=tq, tk=tk),
        grid=(BATCH, nq, nq),
        in_specs=[pl.BlockSpec((tq, QL_COLS), qmap), pl.BlockSpec((tq, QP_COLS), qmap),
                  pl.BlockSpec((tk, KV_RANK), kmap), pl.BlockSpec((tk, LANES), kmap),
                  pl.BlockSpec((MLA_HEADS, KV_RANK, MLA_V), lambda b, qi, ki: (0, 0, 0))],
        out_specs=pl.BlockSpec((tq, MLA_HEADS * MLA_V), qmap),
        out_shape=jax.ShapeDtypeStruct((M_PROMPT, MLA_HEADS * MLA_V), bf16),
        scratch_shapes=[pltpu.VMEM((MLA_HEADS, tq, LANES), f32), pltpu.VMEM((MLA_HEADS, tq, LANES), f32),
                        pltpu.VMEM((MLA_HEADS, tq, KV_RANK), f32)],
        compiler_params=_cparams("parallel", "parallel", "arbitrary"),
        name="attn_prompt",
    )(ql, qp, kc, kpe, w_uv_h)


def _attn_sample_kernel(pt_ref, ql_ref, qp_ref, cn_ref, pn_ref, wuv_ref, ckv_hbm, kpe_hbm, o_ref,
                        q_sc, qp_sc, kc_sc, kp_sc, m_sc, l_sc, acc_sc, ckv_buf, kpe_buf, sem_c, sem_p, *, layer):
    pps = PAGES_PER_STEP
    nsq = SEQS_PER_STEP
    n_pg = nsq * pps
    b = pl.program_id(0)
    j = pl.program_id(1)
    nchunk = pl.num_programs(1)
    rows = MLA_HEADS * DEC_SEQ
    seqs = range(nsq)

    def page_copies(slot, idx, page):
        return (pltpu.make_async_copy(ckv_hbm.at[layer, page], ckv_buf.at[slot, idx], sem_c.at[slot]),
                pltpu.make_async_copy(kpe_hbm.at[layer, page], kpe_buf.at[slot, idx], sem_p.at[slot]))

    def start_step(bb, jj, slot):
        for r in seqs:
            for i in range(pps):
                idx = r * pps + i
                for cp in page_copies(slot, idx, pt_ref[bb * nsq + r, jj * pps + i]):
                    cp.start(priority=idx % 2)

    t = b * nchunk + j
    slot = t % 2

    @pl.when(t == 0)
    def _():
        start_step(b, j, slot)

    @pl.when(t + 1 < pl.num_programs(0) * nchunk)
    def _():
        last_chunk = j == nchunk - 1
        start_step(jnp.where(last_chunk, b + 1, b), jnp.where(last_chunk, 0, j + 1), 1 - slot)

    for idx in range(n_pg):
        for cp in page_copies(slot, idx, 0):
            cp.wait()
    ckv_pages = [ckv_buf.at[slot, idx] for idx in range(n_pg)]
    kpe_pages = [kpe_buf.at[slot, idx] for idx in range(n_pg)]

    @pl.when(j == 0)
    def _():
        for r in seqs:
            tok = slice(r * DEC_SEQ, (r + 1) * DEC_SEQ)
            for h in range(MLA_HEADS):
                q_sc[r, h * DEC_SEQ:(h + 1) * DEC_SEQ, :] = ql_ref[tok, h * KV_RANK:(h + 1) * KV_RANK]
                qp_sc[r, h * DEC_SEQ:(h + 1) * DEC_SEQ, :] = qp_ref[tok, h * LANES:(h + 1) * LANES]
        m_sc[...] = jnp.full_like(m_sc, -jnp.inf)
        l_sc[...] = jnp.zeros_like(l_sc)
        acc_sc[...] = jnp.zeros_like(acc_sc)

    qb = [q_sc[r].astype(bf16) for r in seqs]
    qpb = [qp_sc[r][:, :MLA_ROPE].astype(bf16) for r in seqs]

    def online(r, s, vals):
        m_prev = m_sc[r]
        m_new = jnp.maximum(m_prev, jnp.max(s, axis=1, keepdims=True))
        alpha = jnp.exp(m_prev - m_new)
        p = jnp.exp(s - m_new[:, :1])
        l_sc[r] = alpha * l_sc[r] + jnp.sum(p, axis=1, keepdims=True)
        acc_sc[r] = acc_sc[r] * alpha[:, :1] + _dot(p.astype(bf16), vals)
        m_sc[r] = m_new

    for r in seqs:
        for i in range(pps):
            kc_sc[r, i * PAGE_SIZE:(i + 1) * PAGE_SIZE, :] = ckv_pages[r * pps + i][...].astype(bf16)
            kp_sc[r, :, i * PAGE_SIZE:(i + 1) * PAGE_SIZE] = kpe_pages[r * pps + i][...].astype(bf16)
    kcs = [kc_sc[r] for r in seqs]
    scores = [_dot_nt(qb[r], kcs[r]) + _dot(qpb[r], kp_sc[r]) for r in seqs]
    for r in seqs:
        online(r, scores[r], kcs[r])

    @pl.when(j == pl.num_programs(1) - 1)
    def _():
        pad = jnp.zeros((LANES - DEC_SEQ, KV_RANK), f32)
        key = lax.broadcasted_iota(jnp.int32, (rows, LANES), 1)
        tokid = lax.broadcasted_iota(jnp.int32, (rows, LANES), 0) % DEC_SEQ
        for r in seqs:
            tok = slice(r * DEC_SEQ, (r + 1) * DEC_SEQ)
            cn = jnp.concatenate([cn_ref[tok, :], pad], axis=0).astype(bf16)
            pn = jnp.concatenate([pn_ref[tok, :], pad[:, :MLA_ROPE]], axis=0).astype(bf16)
            s = _dot_nt(qb[r], cn) + _dot_nt(qpb[r], pn)
            online(r, jnp.where(key <= tokid, s, -jnp.inf), cn)
            o = acc_sc[r] / l_sc[r][:, :1]
            for h in range(MLA_HEADS):
                oh = o[h * DEC_SEQ:(h + 1) * DEC_SEQ, :].astype(bf16)
                o_ref[tok, h * MLA_V:(h + 1) * MLA_V] = _dot(oh, wuv_ref[h])


def attn_sample(ql, qp, c_new, p_new, w_uv_h, cache_ckv, cache_kpe_t, page_table, layer):
    pps = PAGES_PER_STEP
    nsq = SEQS_PER_STEP
    assert N_PAGES % pps == 0 and DEC_BATCH % nsq == 0
    nchunk = N_PAGES // pps
    rows = MLA_HEADS * DEC_SEQ
    keys = pps * PAGE_SIZE

    def seqspec(cols):
        return pl.BlockSpec((nsq * DEC_SEQ, cols), lambda b, j, pt: (b, 0))

    n_pg = nsq * pps
    in_specs = [seqspec(QL_COLS), seqspec(QP_COLS), seqspec(KV_RANK), seqspec(MLA_ROPE),
                pl.BlockSpec((MLA_HEADS, KV_RANK, MLA_V), lambda b, j, pt: (0, 0, 0)),
                pl.BlockSpec(memory_space=pl.ANY), pl.BlockSpec(memory_space=pl.ANY)]
    grid_spec = pltpu.PrefetchScalarGridSpec(
        num_scalar_prefetch=1,
        grid=(DEC_BATCH // nsq, nchunk),
        in_specs=in_specs,
        out_specs=pl.BlockSpec((nsq * DEC_SEQ, MLA_HEADS * MLA_V), lambda b, j, pt: (b, 0)),
        scratch_shapes=[pltpu.VMEM((nsq, rows, KV_RANK), f32), pltpu.VMEM((nsq, rows, LANES), f32),
                        pltpu.VMEM((nsq, keys, KV_RANK), bf16), pltpu.VMEM((nsq, MLA_ROPE, keys), bf16),
                        pltpu.VMEM((nsq, rows, LANES), f32), pltpu.VMEM((nsq, rows, LANES), f32),
                        pltpu.VMEM((nsq, rows, KV_RANK), f32),
                        pltpu.VMEM((2, n_pg, PAGE_SIZE, KV_RANK), f32), pltpu.VMEM((2, n_pg, MLA_ROPE, PAGE_SIZE), f32),
                        pltpu.SemaphoreType.DMA((2,)), pltpu.SemaphoreType.DMA((2,))],
    )
    return pl.pallas_call(
        functools.partial(_attn_sample_kernel, layer=layer),
        grid_spec=grid_spec,
        out_shape=jax.ShapeDtypeStruct((M_SAMPLE, MLA_HEADS * MLA_V), f32),
        compiler_params=_cparams("arbitrary", "arbitrary"),
        name="attn_sample",
    )(page_table, ql, qp, c_new, p_new, w_uv_h, cache_ckv, cache_kpe_t)


def _head_sums(x, bd):
    hi = x.astype(bf16)
    lo = (x - hi.astype(f32)).astype(bf16)
    parts = []
    for g in range(x.shape[1] // MXU_DIM):
        sl = slice(g * MXU_DIM, (g + 1) * MXU_DIM)
        parts.append(_dot(hi[:, sl], bd) + _dot(lo[:, sl], bd))
    return jnp.concatenate(parts, axis=1)


def _rwkv_prep_kernel(u_ref, st_ref, mu_ref, w0_ref, a0_ref, kkw_ref, ka_ref, rk_ref, wwa_ref, g2_ref, bd_ref,
                      r_ref, k_ref, v_ref, p_ref, q_ref, w_ref, gc_ref, g_ref, bonus_ref, *, tb, n_prompt_blocks):
    w_ = RWKV_WIDTH
    nst = st_ref.shape[0]
    is_sample = pl.program_id(0) >= n_prompt_blocks
    stride = jnp.where(is_sample, DEC_SEQ, tb)
    row_e = lax.broadcasted_iota(jnp.int32, (tb, nst), 0)
    col_e = lax.broadcasted_iota(jnp.int32, (tb, nst), 1)
    place = jnp.where(row_e == col_e * stride, 1.0, 0.0).astype(bf16)
    starts = _dot_exact_rhs(place, st_ref[...])
    u = u_ref[...]
    row = lax.broadcasted_iota(jnp.int32, (tb, 1), 0)
    is_start = jnp.where(is_sample, row % DEC_SEQ, row) == 0
    prev = jnp.where(is_start, starts, pltpu.roll(u, 1, axis=0))
    z = u + (prev - u) * mu_ref[...]
    r = z[:, :w_]
    k = z[:, w_:2 * w_]
    v = z[:, 2 * w_:3 * w_]
    wa = z[:, 3 * w_:3 * w_ + W_LORA + A_LORA]
    gd = z[:, 3 * w_ + W_LORA + A_LORA:]
    lane = lax.broadcasted_iota(jnp.int32, wa.shape, 1)
    wa = jnp.where(lane < W_LORA, jnp.tanh(wa), wa)
    lora = _dot(wa.astype(bf16), wwa_ref[...])
    w_log = -jax.nn.softplus(-(w0_ref[...] + lora[:, :w_])) - 0.5
    a = jax.nn.sigmoid(a0_ref[...] + lora[:, w_:])
    g = _dot(jax.nn.sigmoid(gd).astype(bf16), g2_ref[...])
    bd = bd_ref[...]
    kk = k * kkw_ref[...]
    kk = kk / jnp.maximum(jnp.sqrt(_head_sums(kk * kk, bd)), 1e-12)
    k = k * (1.0 + (a - 1.0) * ka_ref[...])
    bonus = _head_sums(r * k * rk_ref[...], bd) * v
    g_ref[...] = g
    bonus_ref[...] = bonus
    wdec = -jnp.exp(w_log)
    shift = jnp.where(is_sample, DEC_SEQ.bit_length() - 1, RWKV_CHUNK.bit_length() - 1)
    row_c = lax.broadcasted_iota(jnp.int32, (tb, tb), 0)
    col_c = lax.broadcasted_iota(jnp.int32, (tb, tb), 1)
    same_chunk = lax.shift_right_logical(row_c, shift) == lax.shift_right_logical(col_c, shift)
    tril_blk = jnp.where(same_chunk, jnp.where(col_c <= row_c, 1.0, 0.0), 0.0).astype(bf16)
    gcum = _dot_exact_rhs(tril_blk, wdec)
    for h in range(RWKV_HEADS):
        sl = slice(h * RWKV_HEAD, (h + 1) * RWKV_HEAD)
        r_ref[h] = r[:, sl]
        k_ref[h] = k[:, sl]
        v_ref[h] = v[:, sl]
        p_ref[h] = -kk[:, sl]
        q_ref[h] = (kk * a)[:, sl]
        w_ref[h] = wdec[:, sl]
        gc_ref[h] = gcum[:, sl]


def rwkv_prep(u, starts, lp_rows, w_wa, g2, bd, tb):
    m = u.shape[0]
    nst = starts.shape[1]
    rowc = pl.BlockSpec((tb, RWKV_COLS), lambda i: (i, 0))

    def vec(n):
        return pl.BlockSpec((1, n), lambda i: (0, 0))

    def full(a):
        return pl.BlockSpec(a.shape, lambda i: (0, 0))

    headmajor = pl.BlockSpec((RWKV_HEADS, tb, RWKV_HEAD), lambda i: (0, i, 0))
    tokmajor = pl.BlockSpec((tb, RWKV_WIDTH), lambda i: (i, 0))
    hm_shape = jax.ShapeDtypeStruct((RWKV_HEADS, m, RWKV_HEAD), f32)
    tm_shape = jax.ShapeDtypeStruct((m, RWKV_WIDTH), f32)
    return pl.pallas_call(
        functools.partial(_rwkv_prep_kernel, tb=tb, n_prompt_blocks=M_PROMPT // tb),
        grid=(m // tb,),
        in_specs=[rowc, pl.BlockSpec((None, nst, RWKV_COLS), lambda i: (i, 0, 0)), vec(RWKV_COLS)]
        + [vec(RWKV_WIDTH)] * 5 + [full(w_wa), full(g2), full(bd)],
        out_specs=[headmajor] * 7 + [tokmajor] * 2,
        out_shape=[hm_shape] * 7 + [tm_shape] * 2,
        compiler_params=_cparams("parallel"),
        name="rwkv_prep",
    )(u, starts, *lp_rows, w_wa, g2, bd)


def _tri_consts(csz):
    row = lax.broadcasted_iota(jnp.int32, (csz, csz), 0)
    col = lax.broadcasted_iota(jnp.int32, (csz, csz), 1)
    return (col <= row).astype(f32), (col < row).astype(f32), (col == row).astype(f32)


def _rwkv_phase1_kernel(r_ref, k_ref, v_ref, p_ref, q_ref, w_ref, g_ref, rhat_ref, yint_ref, mlow_ref, nn_ref, e_ref,
                        *, csz, nck):
    tril_incl, tril_strict, eye = _tri_consts(csz)
    cs = range(nck)
    sls = [slice(c * csz, (c + 1) * csz) for c in cs]

    def stack(a, b):
        return jnp.concatenate([a, b], axis=0).astype(bf16)

    pr, qt, kt, p0, r0, qh, kh, vb, e_last = [], [], [], [], [], [], [], [], []
    for sl in sls:
        r, k, v, p, q = r_ref[sl, :], k_ref[sl, :], v_ref[sl, :], p_ref[sl, :], q_ref[sl, :]
        g = g_ref[sl, :]
        gp = g - w_ref[sl, :]
        ref = g[csz // 2 - 1:csz // 2, :]
        gl = g[csz - 1:csz, :]
        e_in = jnp.exp(ref - g)
        e_out = jnp.exp(gl - g)
        pr.append(stack(p * jnp.exp(gp - ref), r * jnp.exp(g - ref)))
        qt.append((q * e_in).astype(bf16))
        kt.append((k * e_in).astype(bf16))
        p0.append((p * jnp.exp(gp)).astype(bf16))
        r0.append(r * jnp.exp(g))
        qh.append((q * e_out).astype(bf16))
        kh.append((k * e_out).astype(bf16))
        vb.append(v.astype(bf16))
        e_last.append(jnp.exp(gl))
    gq = [_dot_nt(pr[c], qt[c]) for c in cs]
    gk = [_dot_nt(pr[c], kt[c]) for c in cs]
    a_qp = [gq[c][:csz] * tril_strict for c in cs]
    a_kp = [(gk[c][:csz] * tril_strict).astype(bf16) for c in cs]
    a_qr = [(gq[c][csz:] * tril_incl).astype(bf16) for c in cs]
    a_kr = [(gk[c][csz:] * tril_incl).astype(bf16) for c in cs]
    tinv = [eye + a_qp[c] for c in cs]
    if csz > 2:
        apb = [a_qp[c].astype(bf16) for c in cs]
        apow = [_dot(apb[c], apb[c]) for c in cs]
        n = 2
        while n < csz:
            last = 2 * n >= csz
            rhs = [apow[c].astype(bf16) for c in cs]
            if last:
                prod = [_dot(tinv[c].astype(bf16), rhs[c]) for c in cs]
                tinv = [tinv[c] + prod[c] for c in cs]
            else:
                prod = [_dot(stack(tinv[c], apow[c]), rhs[c]) for c in cs]
                tinv = [tinv[c] + prod[c][:csz] for c in cs]
                apow = [prod[c][csz:] for c in cs]
            n *= 2
    tb_ = [tinv[c].astype(bf16) for c in cs]
    akpv = [_dot(a_kp[c], vb[c]).astype(bf16) for c in cs]
    phb = [_dot(tb_[c], p0[c]).astype(bf16) for c in cs]
    wvb = [_dot(tb_[c], akpv[c]).astype(bf16) for c in cs]
    rhat = [r0[c] + _dot(a_qr[c], phb[c]) for c in cs]
    yint = [_dot(a_qr[c], wvb[c]) + _dot(a_kr[c], vb[c]) for c in cs]
    mlow = [_dot_tn(phb[c], qh[c]) for c in cs]
    nn = [_dot_tn(wvb[c], qh[c]) + _dot_tn(vb[c], kh[c]) for c in cs]
    for c in cs:
        rhat_ref[sls[c], :] = rhat[c]
        yint_ref[sls[c], :] = yint[c]
        mlow_ref[c] = mlow[c]
        nn_ref[c] = nn[c]
        e_ref[c] = jnp.broadcast_to(e_last[c], (8, RWKV_HEAD))


def rwkv_phase1(r, k, v, p, q, w, g, row0, rows, csz, nck):
    tb = nck * csz
    assert rows % tb == 0 and row0 % tb == 0
    off = row0 // tb
    nchunks = rows // csz
    hm_in = pl.BlockSpec((None, tb, RWKV_HEAD), lambda h, i: (h, i + off, 0))
    hm_out = pl.BlockSpec((None, tb, RWKV_HEAD), lambda h, i: (h, i, 0))
    sq = pl.BlockSpec((None, nck, RWKV_HEAD, RWKV_HEAD), lambda h, i: (h, i, 0, 0))
    ev = pl.BlockSpec((None, nck, 8, RWKV_HEAD), lambda h, i: (h, i, 0, 0))
    return pl.pallas_call(
        functools.partial(_rwkv_phase1_kernel, csz=csz, nck=nck),
        grid=(RWKV_HEADS, rows // tb),
        in_specs=[hm_in] * 7,
        out_specs=[hm_out, hm_out, sq, sq, ev],
        out_shape=[jax.ShapeDtypeStruct((RWKV_HEADS, rows, RWKV_HEAD), f32)] * 2
        + [jax.ShapeDtypeStruct((RWKV_HEADS, nchunks, RWKV_HEAD, RWKV_HEAD), f32)] * 2
        + [jax.ShapeDtypeStruct((RWKV_HEADS, nchunks, 8, RWKV_HEAD), f32)],
        compiler_params=_cparams("parallel", "parallel"),
        name="rwkv_phase1",
    )(r, k, v, p, q, w, g)


def _rwkv_phase2_kernel(*refs, csz, nck, n_alias):
    s0_ref, rhat_ref, yint_ref, mlow_ref, nn_ref, e_ref = refs[:6]
    y_ref, sout_ref, s_sc = refs[6 + n_alias:]
    ci = pl.program_id(1)
    hs = range(RWKV_HEADS)

    @pl.when(ci == 0)
    def _():
        s_sc[...] = s0_ref[...]

    s = [s_sc[h] for h in hs]
    for c in range(nck):
        sl = slice(c * csz, (c + 1) * csz)
        sb = [s[h].astype(bf16) for h in hs]
        ys = [_dot_nt(rhat_ref[h, sl, :].astype(bf16), sb[h]) for h in hs]
        sm = [_dot(sb[h], mlow_ref[h, c].astype(bf16)) for h in hs]
        for h in hs:
            y_ref[h, sl, :] = ys[h] + yint_ref[h, sl, :]
        s = [s[h] * e_ref[h, c][:1, :] + sm[h] + nn_ref[h, c] for h in hs]
    for h in hs:
        s_sc[h] = s[h]

    @pl.when(ci == pl.num_programs(1) - 1)
    def _():
        sout_ref[...] = s_sc[...]


def rwkv_phase2(s0, rhat, yint, mlow, nn, e, nseq, tlen, csz, layer=None, s_stack=None):
    cps = tlen // csz
    nck = min(cps, 4)
    assert cps % nck == 0
    nsteps = cps // nck
    tb = nck * csz
    hm = pl.BlockSpec((RWKV_HEADS, tb, RWKV_HEAD), lambda s, c: (0, s * nsteps + c, 0))
    sq = pl.BlockSpec((RWKV_HEADS, nck, RWKV_HEAD, RWKV_HEAD), lambda s, c: (0, s * nsteps + c, 0, 0))
    ev = pl.BlockSpec((RWKV_HEADS, nck, 8, RWKV_HEAD), lambda s, c: (0, s * nsteps + c, 0, 0))
    s_shape = (RWKV_HEADS, RWKV_HEAD, RWKV_HEAD)
    if layer is None:
        st_in = st_out = pl.BlockSpec((None,) + s_shape, lambda s, c: (s, 0, 0, 0))
        st_out_shape = jax.ShapeDtypeStruct((nseq,) + s_shape, f32)
        extra_specs, extra_args = [], []
    else:
        st_in = pl.BlockSpec((None, None) + s_shape, lambda s, c: (layer, s, 0, 0, 0))
        st_out, st_out_shape, extra_specs, extra_args = _stacked_state_out(layer, s_stack, nseq, s_shape)
    n_in = 6
    return pl.pallas_call(
        functools.partial(_rwkv_phase2_kernel, csz=csz, nck=nck, n_alias=len(extra_args)),
        grid=(nseq, nsteps),
        in_specs=[st_in, hm, hm, sq, sq, ev] + extra_specs,
        out_specs=[hm, st_out],
        out_shape=[jax.ShapeDtypeStruct((RWKV_HEADS, nseq * tlen, RWKV_HEAD), f32), st_out_shape],
        scratch_shapes=[pltpu.VMEM(s_shape, f32)],
        input_output_aliases={n_in: 1} if extra_args else {},
        compiler_params=_cparams("parallel", "arbitrary"),
        name="rwkv_phase2",
    )(s0, rhat, yint, mlow, nn, e, *extra_args)


def _rwkv_out_kernel(y_ref, g_ref, bonus_ref, lnw_ref, lnb_ref, bd_ref, o_ref, y_sc):
    for h in range(RWKV_HEADS):
        y_sc[:, h * RWKV_HEAD:(h + 1) * RWKV_HEAD] = y_ref[h]
    y = y_sc[...]
    bd = bd_ref[...]
    mean = _head_sums(y, bd) * (1.0 / RWKV_HEAD)
    yc = y - mean
    var = _head_sums(yc * yc, bd) * (1.0 / RWKV_HEAD)
    yn = yc * lax.rsqrt(var + RWKV_GN_EPS) * lnw_ref[...] + lnb_ref[...]
    o_ref[...] = ((yn + bonus_ref[...]) * g_ref[...]).astype(o_ref.dtype)


def rwkv_out(y_hm, g, bonus, ln_w, ln_b, bd):
    m = g.shape[0]
    tb = _pick(m, (256, 128))
    tok = pl.BlockSpec((tb, RWKV_WIDTH), lambda i: (i, 0))
    vec = pl.BlockSpec((1, RWKV_WIDTH), lambda i: (0, 0))
    return pl.pallas_call(
        _rwkv_out_kernel,
        grid=(m // tb,),
        in_specs=[pl.BlockSpec((RWKV_HEADS, tb, RWKV_HEAD), lambda i: (0, i, 0)), tok, tok, vec, vec,
                  pl.BlockSpec(bd.shape, lambda i: (0, 0))],
        out_specs=tok,
        out_shape=jax.ShapeDtypeStruct((m, RWKV_WIDTH), bf16),
        scratch_shapes=[pltpu.VMEM((tb, RWKV_WIDTH), f32)],
        compiler_params=_cparams("parallel"),
        name="rwkv_out",
    )(y_hm, g, bonus, ln_w.reshape(1, -1), ln_b.reshape(1, -1), bd)


def _mlstm_kernel(*refs, csz, n_alias):
    u_ref, ug_ref, gb_ref, ng_ref, c0_ref, n0_ref, m0_ref = refs[:7]
    y_ref, cout_ref, nout_ref, mout_ref, c_sc, n_sc, m_sc = refs[7 + n_alias:]
    ci = pl.program_id(1)
    dh = MLSTM_DH
    w_ = MLSTM_WIDTH

    @pl.when(ci == 0)
    def _():
        c_sc[...] = c0_ref[...]
        n_sc[...] = n0_ref[...]
        m_sc[...] = m0_ref[...]

    tril_incl, _, _ = _tri_consts(csz)
    causal = tril_incl > 0.0
    gates = ug_ref[...] + gb_ref[...]
    lane = lax.broadcasted_iota(jnp.int32, gates.shape, 1)
    logf = jax.nn.log_sigmoid(gates)
    fcum = _dot_exact_rhs(tril_incl.astype(bf16), logf)
    cols = jnp.where(lane < MLSTM_HEADS, gates, fcum)
    sel = (lax.broadcasted_iota(jnp.int32, (8, LANES), 0) == lax.broadcasted_iota(jnp.int32, (8, LANES), 1))
    hi, mid, lo = _split3(cols)
    selb = sel.astype(bf16)
    rows = _dot_nt(selb, hi) + _dot_nt(selb, mid) + _dot_nt(selb, lo)
    hs = range(MLSTM_HEADS)
    q = [u_ref[:, h * dh:(h + 1) * dh] for h in hs]
    k = [u_ref[:, w_ + h * dh:w_ + (h + 1) * dh] * (dh ** -0.5) for h in hs]
    v = [u_ref[:, 2 * w_ + h * dh:2 * w_ + (h + 1) * dh] for h in hs]
    qb = [q[h].astype(bf16) for h in hs]
    kb = [k[h].astype(bf16) for h in hs]
    vb = [v[h].astype(bf16) for h in hs]
    c_old = [c_sc[h] for h in hs]
    n_old = [n_sc[h:h + 1, :] for h in hs]
    m_old = [m_sc[h:h + 1, :1] for h in hs]
    ig_col = [cols[:, h:h + 1] for h in hs]
    f_col = [cols[:, MLSTM_HEADS + h:MLSTM_HEADS + h + 1] for h in hs]
    d_ts = [jnp.where(causal, f_col[h] - rows[MLSTM_HEADS + h:MLSTM_HEADS + h + 1, :] + rows[h:h + 1, :], -jnp.inf)
            for h in hs]
    inter = [m_old[h] + f_col[h] for h in hs]
    m_t = [jnp.maximum(inter[h], jnp.max(d_ts[h], axis=1, keepdims=True)) for h in hs]
    w_inter = [jnp.exp(inter[h] - m_t[h]) for h in hs]
    qk = [_dot_nt(qb[h], kb[h]) for h in hs]
    cq = [_dot_nt(qb[h], c_old[h].astype(bf16)) for h in hs]
    a = [jnp.exp(d_ts[h] - m_t[h]) * qk[h] for h in hs]
    av = [_dot(a[h].astype(bf16), vb[h]) for h in hs]
    m_new = [m_t[h][csz - 1:csz, :] for h in hs]
    f_last = [f_col[h][csz - 1:csz, :] for h in hs]
    carry = [jnp.exp(m_old[h] + f_last[h] - m_new[h]) for h in hs]
    w_write = [jnp.exp(f_last[h] - f_col[h] + ig_col[h] - m_new[h]) for h in hs]
    vk = [_dot_tn((v[h] * w_write[h]).astype(bf16), kb[h]) for h in hs]
    for h in hs:
        num = w_inter[h] * cq[h] + av[h]
        den = w_inter[h] * jnp.sum(q[h] * n_old[h], axis=1, keepdims=True) + jnp.sum(a[h], axis=1, keepdims=True)
        hh = num / jnp.maximum(jnp.abs(den), jnp.exp(-m_t[h]))
        hh = hh * lax.rsqrt(jnp.mean(hh * hh, axis=1, keepdims=True) + NORM_EPS)
        og = jax.nn.sigmoid(u_ref[:, 3 * w_ + h * dh:3 * w_ + (h + 1) * dh])
        y_ref[:, h * dh:(h + 1) * dh] = (og * hh * ng_ref[:, h * dh:(h + 1) * dh]).astype(y_ref.dtype)
        c_sc[h] = carry[h] * c_old[h] + vk[h]
        n_sc[h:h + 1, :] = carry[h] * n_old[h] + jnp.sum(k[h] * w_write[h], axis=0, keepdims=True)
        m_sc[h:h + 1, :] = jnp.broadcast_to(m_new[h], (1, LANES))

    @pl.when(ci == pl.num_programs(1) - 1)
    def _():
        cout_ref[...] = c_sc[...]
        nout_ref[...] = n_sc[...]
        mout_ref[...] = m_sc[...]


def _stacked_state_out(layer, stack, nseq, shape):
    spec = pl.BlockSpec((None, None) + shape, lambda s, c: (layer, s) + (0,) * len(shape))
    out_shape = jax.ShapeDtypeStruct((DEPTH, nseq) + shape, f32)
    assert stack.shape == out_shape.shape
    return spec, out_shape, [pl.BlockSpec(memory_space=pl.ANY)], [stack]


def mlstm(u, ug, gate_bias_row, norm_g, c0, n0, m0, row0, nseq, tlen, csz, layer=None, c_stack=None):
    cps = tlen // csz
    assert row0 % csz == 0
    off = row0 // csz

    def rowspec(cols):
        return pl.BlockSpec((csz, cols), lambda s, c: (off + s * cps + c, 0))

    def state(shape):
        return pl.BlockSpec((None,) + shape, lambda s, c: (s,) + (0,) * len(shape))

    c_shape = (MLSTM_HEADS, MLSTM_DH, MLSTM_DH)
    if layer is None:
        c_in, c_out, c_out_shape = state(c_shape), state(c_shape), jax.ShapeDtypeStruct((nseq,) + c_shape, f32)
        extra_specs, extra_args = [], []
    else:
        c_in = pl.BlockSpec((None, None) + c_shape, lambda s, c: (layer, s, 0, 0, 0))
        c_out, c_out_shape, extra_specs, extra_args = _stacked_state_out(layer, c_stack, nseq, c_shape)
    n_in = 7
    return pl.pallas_call(
        functools.partial(_mlstm_kernel, csz=csz, n_alias=len(extra_args)),
        grid=(nseq, cps),
        in_specs=[rowspec(4 * MLSTM_WIDTH), rowspec(LANES), pl.BlockSpec((1, LANES), lambda s, c: (0, 0)),
                  pl.BlockSpec((1, MLSTM_WIDTH), lambda s, c: (0, 0)),
                  c_in, state((8, MLSTM_DH)), state((8, LANES))] + extra_specs,
        out_specs=[pl.BlockSpec((csz, MLSTM_WIDTH), lambda s, c: (s * cps + c, 0)),
                   c_out, state((8, MLSTM_DH)), state((8, LANES))],
        out_shape=[jax.ShapeDtypeStruct((nseq * tlen, MLSTM_WIDTH), f32), c_out_shape,
                   jax.ShapeDtypeStruct((nseq, 8, MLSTM_DH), f32), jax.ShapeDtypeStruct((nseq, 8, LANES), f32)],
        scratch_shapes=[pltpu.VMEM((MLSTM_HEADS, MLSTM_DH, MLSTM_DH), f32), pltpu.VMEM((8, MLSTM_DH), f32),
                        pltpu.VMEM((8, LANES), f32)],
        input_output_aliases={n_in: 1} if extra_args else {},
        compiler_params=_cparams("parallel", "arbitrary"),
        name="mlstm",
    )(u, ug, gate_bias_row, norm_g.reshape(1, -1), c0, n0, m0, *extra_args)


def _column_plan():
    per_head = MLA_NOPE + MLA_ROPE
    q_nope = [h * per_head + d for h in range(MLA_HEADS) for d in range(MLA_NOPE)]
    q_rope = []
    for h in range(MLA_HEADS):
        base = h * per_head + MLA_NOPE
        x1 = [base + e for e in range(ROPE_HALF)]
        x2 = [base + ROPE_HALF + e for e in range(ROPE_HALF)]
        q_rope += x1 + x2 + x2 + x1
    kv_base = MLA_Q_COLS
    ckv = [kv_base + c for c in range(KV_RANK)]
    x1 = [kv_base + KV_RANK + e for e in range(ROPE_HALF)]
    x2 = [kv_base + KV_RANK + ROPE_HALF + e for e in range(ROPE_HALF)]
    return np.asarray(q_nope + q_rope, np.int32), np.asarray(ckv + x1 + x2 + x2 + x1, np.int32)


def _rope_tables():
    pos = jnp.concatenate([jnp.tile(jnp.arange(SEQ), BATCH), jnp.tile(PAST_LEN + jnp.arange(DEC_SEQ), DEC_BATCH)])
    inv = ROPE_THETA ** (-jnp.arange(ROPE_HALF, dtype=f32) / ROPE_HALF)
    ang = pos.astype(f32)[:, None] * inv[None, :]
    cos, sin = jnp.cos(ang), jnp.sin(ang)
    zero = jnp.zeros_like(cos)
    return jnp.concatenate([cos, cos, zero, zero], axis=1), jnp.concatenate([-sin, sin, zero, zero], axis=1)


def _pad8(x, rows_axis):
    pad = [(0, 0)] * x.ndim
    pad[rows_axis] = (0, 8 - x.shape[rows_axis])
    return jnp.pad(x, pad)


def kernel(x_prompt, x_sample, cache_ckv, cache_kpe, page_table, state_rwkv_shift, state_rwkv_S, state_mlstm_C, state_mlstm_n, state_mlstm_m, norm_gains, w_in, g_ckv, w_uk, w_uv, rwkv_mu, rwkv_w0, rwkv_w2, rwkv_a0, rwkv_a2, rwkv_g2, rwkv_kk, rwkv_ka, rwkv_rk, rwkv_ln_w, rwkv_ln_b, mlstm_gate_b, mlstm_norm, gate_b, w_br_mla, w_br_rwkv, w_br_mlstm, w_out, w_up, w_down):
    d = D_MODEL
    o0 = MLA_Q_COLS
    o1 = o0 + MLA_KV_COLS
    o2 = o1 + RWKV_COLS
    o3 = o2 + MLSTM_COLS
    q_cols, kv_cols = _column_plan()
    c4, s4 = _rope_tables()
    head_id = np.arange(MXU_DIM) // RWKV_HEAD
    bd = jnp.asarray(head_id[:, None] == head_id[None, :], bf16)
    zeros_s = jnp.zeros((BATCH, RWKV_HEADS, RWKV_HEAD, RWKV_HEAD), f32)
    zeros_c = jnp.zeros((BATCH, MLSTM_HEADS, MLSTM_DH, MLSTM_DH), f32)
    zeros_n = jnp.zeros((BATCH, 8, MLSTM_DH), f32)
    zeros_m = jnp.zeros((BATCH, 8, LANES), f32)
    tq = _pick(SEQ, (512, 256))
    tb_rw = 256
    n_starts = tb_rw // DEC_SEQ
    assert SEQ % tb_rw == 0 and M_SAMPLE % tb_rw == 0
    cache_kpe_t = jnp.swapaxes(cache_kpe, 2, 3)
    w_in_t = jnp.swapaxes(w_in, 1, 2)

    x = jnp.concatenate([x_prompt.reshape(M_PROMPT, d), x_sample.reshape(M_SAMPLE, d)], axis=0)
    h, h_lo = rmsnorm_bf16(x, norm_gains[0, 0])
    prompt_states, sample_states = [], []
    s_stack = jnp.zeros((DEPTH, DEC_BATCH, RWKV_HEADS, RWKV_HEAD, RWKV_HEAD), f32)
    c_stack = jnp.zeros((DEPTH, DEC_BATCH, MLSTM_HEADS, MLSTM_DH, MLSTM_DH), f32)
    for l in range(DEPTH):
        w_q = jnp.take(w_in_t[l], q_cols, axis=0).astype(bf16)
        w_kv = jnp.take(w_in_t[l], kv_cols, axis=0).astype(bf16)
        w_mg = jnp.pad(w_in_t[l, o2 + 4 * MLSTM_WIDTH:o3].T, ((0, 0), (0, LANES - 2 * MLSTM_HEADS)))

        u_q = matmul(h, w_q, row0=0, n=UQ_COLS)
        u_kv = matmul(h, w_kv, row0=0, n=UKV_COLS)
        u_rw = matmul(h, w_in_t, layer=l, row0=o1, n=RWKV_COLS)
        u_ml = matmul(h, w_in_t, layer=l, row0=o2, n=4 * MLSTM_WIDTH)
        u_mg = matmul_split(h, h_lo, w_mg)
        gates = matmul(h, w_in_t, out_dtype=bf16, bias=gate_b[l].reshape(-1), layer=l, row0=o3, n=GATE_COLS)

        w_uk_t = jnp.transpose(w_uk[l], (1, 2, 0)).astype(bf16)
        w_uv_h = jnp.transpose(w_uv[l], (1, 0, 2)).astype(bf16)
        ql_p, qp_p, kc_p, kpe_p, ckv_p, kpef_p = mla_prep(u_q, u_kv, c4, s4, w_uk_t, g_ckv[l], 0, M_PROMPT, bf16)
        ql_s, qp_s, _, _, ckv_s, kpef_s = mla_prep(u_q, u_kv, c4, s4, w_uk_t, g_ckv[l], M_PROMPT, M_SAMPLE, f32)
        y_mla_p = attn_prompt(ql_p, qp_p, kc_p, kpe_p, w_uv_h, tq)
        y_mla_s = attn_sample(ql_s, qp_s, ckv_s, kpef_s, w_uv_h, cache_ckv, cache_kpe_t, page_table, l)
        y_mla = jnp.concatenate([y_mla_p, y_mla_s.astype(bf16)], axis=0)

        block_last = u_rw[tb_rw - 1:M_PROMPT:tb_rw].reshape(BATCH, SEQ // tb_rw, 1, RWKV_COLS)
        before = jnp.concatenate([jnp.zeros((BATCH, 1, 1, RWKV_COLS), f32), block_last[:, :-1]], axis=1)
        starts = jnp.concatenate([
            jnp.pad(before.reshape(M_PROMPT // tb_rw, 1, RWKV_COLS), ((0, 0), (0, n_starts - 1), (0, 0))),
            state_rwkv_shift[l].reshape(M_SAMPLE // tb_rw, n_starts, RWKV_COLS)], axis=0)
        zero_blk = jnp.zeros((W_LORA, RWKV_WIDTH), f32)
        w_wa = jnp.concatenate([jnp.concatenate([rwkv_w2[l], zero_blk], axis=1),
                                jnp.concatenate([zero_blk, rwkv_a2[l]], axis=1)], axis=0).astype(bf16)
        lp_rows = [rwkv_mu[l].reshape(1, -1), rwkv_w0[l].reshape(1, -1), rwkv_a0[l].reshape(1, -1),
                   rwkv_kk[l].reshape(1, -1), rwkv_ka[l].reshape(1, -1), rwkv_rk[l].reshape(1, -1)]
        ops = rwkv_prep(u_rw, starts, lp_rows, w_wa, rwkv_g2[l].astype(bf16), bd, tb_rw)
        hm_ops, g_, bonus = ops[:7], ops[7], ops[8]
        t_p = rwkv_phase1(*hm_ops, 0, M_PROMPT, RWKV_CHUNK, 16)
        t_s = rwkv_phase1(*hm_ops, M_PROMPT, M_SAMPLE, DEC_SEQ, 32)
        y_p, s_p = rwkv_phase2(zeros_s, *t_p, BATCH, SEQ, RWKV_CHUNK)
        y_s, s_stack = rwkv_phase2(state_rwkv_S, *t_s, DEC_BATCH, DEC_SEQ, DEC_SEQ, layer=l, s_stack=s_stack)
        y_rwkv = rwkv_out(jnp.concatenate([y_p, y_s], axis=1), g_, bonus, rwkv_ln_w[l], rwkv_ln_b[l], bd)

        gb_row = jnp.pad(mlstm_gate_b[l].reshape(1, -1), ((0, 0), (0, LANES - 2 * MLSTM_HEADS)))
        ym_p, c_p, n_p, m_p = mlstm(u_ml, u_mg, gb_row, mlstm_norm[l], zeros_c, zeros_n, zeros_m, 0, BATCH, SEQ, MLSTM_CHUNK)
        m0_s = jnp.broadcast_to(_pad8(state_mlstm_m[l], 1)[:, :, None], (DEC_BATCH, 8, LANES))
        ym_s, c_stack, n_s, m_s = mlstm(u_ml, u_mg, gb_row, mlstm_norm[l], state_mlstm_C, _pad8(state_mlstm_n[l], 1), m0_s,
                                        M_PROMPT, DEC_BATCH, DEC_SEQ, DEC_SEQ, layer=l, c_stack=c_stack)
        y_mlstm = jnp.concatenate([ym_p, ym_s], axis=0).astype(bf16)

        merged = merge_branches(y_mla, y_rwkv, y_mlstm, w_br_mla, w_br_rwkv, w_br_mlstm, gates, l)
        attn_out = matmul(merged, w_out, layer=l)
        x, hf = resnorm(x, attn_out, norm_gains[l, 1], norm_gains[l, 2], False)
        ff = matmul_ksplit(matmul(hf, w_up, out_dtype=bf16, act="relu2", layer=l), w_down, layer=l)
        g_next = norm_gains[l + 1, 0] if l + 1 < DEPTH else norm_gains[l, 3]
        x, h, h_lo = resnorm(x, ff, norm_gains[l, 3], g_next, True)

        shift_p = u_rw[SEQ - 1:M_PROMPT:SEQ]
        shift_s = u_rw[M_PROMPT + DEC_SEQ - 1::DEC_SEQ]
        prompt_states.append((ckv_p.reshape(BATCH, SEQ, KV_RANK), kpef_p.reshape(BATCH, SEQ, MLA_ROPE), shift_p, s_p,
                              c_p, n_p[:, :MLSTM_HEADS], m_p[:, :MLSTM_HEADS, 0]))
        sample_states.append((ckv_s.reshape(DEC_BATCH, DEC_SEQ, KV_RANK), kpef_s.reshape(DEC_BATCH, DEC_SEQ, MLA_ROPE),
                              shift_s, None, None, n_s[:, :MLSTM_HEADS], m_s[:, :MLSTM_HEADS, 0]))

    outs = [x[:M_PROMPT].reshape(BATCH, SEQ, d), x[M_PROMPT:].reshape(DEC_BATCH, DEC_SEQ, d)]
    for states, stacked in ((prompt_states, {}), (sample_states, {3: s_stack, 4: c_stack})):
        for i in range(7):
            outs.append(stacked[i] if i in stacked else jnp.stack([st[i] for st in states]))
    return tuple(outs)
```

```python
import functools

import jax
import jax.numpy as jnp
import numpy as np
from jax import lax
from jax.experimental import pallas as pl
from jax.experimental.pallas import tpu as pltpu

D_MODEL = 2048
BATCH = 2
SEQ = 4096
DEPTH = 2
DEC_BATCH = 128
DEC_SEQ = 8
PAST_LEN = 16384
PAGE_SIZE = 128

MLA_HEADS = 8
MLA_NOPE = 128
MLA_ROPE = 64
MLA_V = 128
KV_RANK = 256
ROPE_THETA = 10000.0
MLA_SCALE = (MLA_NOPE + MLA_ROPE) ** -0.5
RWKV_HEAD = 64
RWKV_WIDTH = 1024
RWKV_HEADS = RWKV_WIDTH // RWKV_HEAD
W_LORA = 64
A_LORA = 64
G_LORA = 128
RWKV_GN_EPS = 64e-5
MLSTM_HEADS = 4
MLSTM_DH = 256
MLSTM_WIDTH = MLSTM_HEADS * MLSTM_DH
MLSTM_CHUNK = 64
N_BRANCH = 3
D_FF = 4 * D_MODEL
NORM_EPS = 1e-6

MLA_Q_COLS = MLA_HEADS * (MLA_NOPE + MLA_ROPE)
MLA_KV_COLS = KV_RANK + MLA_ROPE
RWKV_COLS = 3 * RWKV_WIDTH + W_LORA + A_LORA + G_LORA
MLSTM_COLS = 4 * MLSTM_WIDTH + 2 * MLSTM_HEADS
GATE_COLS = N_BRANCH * D_MODEL

LANES = 128
MXU_DIM = 256
ROPE_HALF = MLA_ROPE // 2
QL_COLS = MLA_HEADS * KV_RANK
QP_COLS = MLA_HEADS * LANES
UQ_COLS = MLA_HEADS * MLA_NOPE + QP_COLS
UKV_COLS = KV_RANK + LANES
N_PAGES = PAST_LEN // PAGE_SIZE
PAGES_PER_STEP = 8
SEQS_PER_STEP = 4
RWKV_CHUNK = 64
RW_PAD = -(-RWKV_COLS // 896) * 896
M_PROMPT = BATCH * SEQ
M_SAMPLE = DEC_BATCH * DEC_SEQ
M_TOK = M_PROMPT + M_SAMPLE
VMEM_LIMIT = 56 * 1024 * 1024

f32 = jnp.float32
bf16 = jnp.bfloat16


def _cparams(*sem):
    return pltpu.CompilerParams(dimension_semantics=sem, vmem_limit_bytes=VMEM_LIMIT)


def _pick(n, cands):
    for c in cands:
        if n % c == 0:
            return c
    raise ValueError(f"no tile for {n} in {cands}")


def _dot(a, b):
    return jnp.dot(a, b, preferred_element_type=f32)


def _dot_nt(a, b):
    return lax.dot_general(a, b, (((1,), (1,)), ((), ())), preferred_element_type=f32)


def _dot_tn(a, b):
    return lax.dot_general(a, b, (((0,), (0,)), ((), ())), preferred_element_type=f32)


def _split3(x):
    hi = x.astype(bf16)
    r1 = x - hi.astype(f32)
    mid = r1.astype(bf16)
    lo = (r1 - mid.astype(f32)).astype(bf16)
    return hi, mid, lo


def _dot_exact_rhs(sel, x):
    hi, mid, lo = _split3(x)
    return _dot(sel, hi) + _dot(sel, mid) + _dot(sel, lo)


def _dot_exact_lhs(x, sel):
    hi, mid, lo = _split3(x)
    return _dot(hi, sel) + _dot(mid, sel) + _dot(lo, sel)


def _rms(x, g):
    return x * lax.rsqrt(jnp.mean(x * x, axis=-1, keepdims=True) + NORM_EPS) * g


def _emit_norm(hn, h_ref, lo_ref):
    hb = hn.astype(bf16)
    h_ref[...] = hb
    if lo_ref is not None:
        lo_ref[...] = (hn - hb.astype(f32)).astype(bf16)


def _norm_kernel(x_ref, g_ref, h_ref, lo_ref):
    _emit_norm(_rms(x_ref[...], g_ref[...]), h_ref, lo_ref)


def rmsnorm_bf16(x, g):
    m, d = x.shape
    tm = _pick(m, (512, 256))
    row = pl.BlockSpec((tm, d), lambda i: (i, 0))
    return pl.pallas_call(
        _norm_kernel,
        grid=(m // tm,),
        in_specs=[row, pl.BlockSpec((1, d), lambda i: (0, 0))],
        out_specs=[row, row],
        out_shape=[jax.ShapeDtypeStruct((m, d), bf16)] * 2,
        compiler_params=_cparams("parallel"),
        name="rmsnorm",
    )(x, g.reshape(1, d))


def _resnorm_kernel(x_ref, y_ref, gp_ref, gn_ref, xo_ref, h_ref, *lo_ref):
    xn = x_ref[...] + _rms(y_ref[...], gp_ref[...])
    xo_ref[...] = xn
    _emit_norm(_rms(xn, gn_ref[...]), h_ref, lo_ref[0] if lo_ref else None)


def resnorm(x, y, g_post, g_next, with_lo):
    m, d = x.shape
    tm = _pick(m, (512, 256))
    row = pl.BlockSpec((tm, d), lambda i: (i, 0))
    gain = pl.BlockSpec((1, d), lambda i: (0, 0))
    n_h = 2 if with_lo else 1
    return pl.pallas_call(
        _resnorm_kernel,
        grid=(m // tm,),
        in_specs=[row, row, gain, gain],
        out_specs=[row] * (1 + n_h),
        out_shape=[jax.ShapeDtypeStruct((m, d), f32)] + [jax.ShapeDtypeStruct((m, d), bf16)] * n_h,
        compiler_params=_cparams("parallel"),
        name="resnorm",
    )(x, y, g_post.reshape(1, d), g_next.reshape(1, d))


def _weights(b_ref):
    b = b_ref[...]
    return b if b.dtype == bf16 else b.astype(bf16)


def _weight_spec(layer, shape, index_map):
    if layer is None:
        return pl.BlockSpec(shape, index_map)
    return pl.BlockSpec((None,) + shape, lambda *g: (layer,) + index_map(*g))


def _mm(a_ref, b_ref, b_transposed):
    b = _weights(b_ref)
    if b_transposed:
        return _dot_nt(a_ref[...], b.reshape(b.shape[-2:]))
    return _dot(a_ref[...], b)


def _mm_kernel(a_ref, b_ref, o_ref, *, act, b_transposed):
    acc = _mm(a_ref, b_ref, b_transposed)
    if act == "relu2":
        acc = jnp.square(jnp.maximum(acc, 0.0))
    o_ref[...] = acc.astype(o_ref.dtype)


def _mm_bias_sigmoid_kernel(a_ref, b_ref, bias_ref, o_ref, *, b_transposed):
    acc = _mm(a_ref, b_ref, b_transposed) + bias_ref[...]
    o_ref[...] = jax.nn.sigmoid(acc).astype(o_ref.dtype)


def _mm_tiles(m, n):
    return _pick(m, (1024, 512, 256)), _pick(n, (1024, 896, 512, 384, 256, 128))


def matmul(a, b, out_dtype=f32, act=None, bias=None, layer=None, row0=None, n=None):
    m, k = a.shape
    b_transposed = row0 is not None
    if b_transposed:
        assert row0 % 8 == 0 and b.shape[-1] == k
    else:
        n = b.shape[-1]
    tm, tn = _mm_tiles(m, n)
    if not b_transposed:
        b_spec = _weight_spec(layer, (k, tn), lambda i, j: (0, j))
    elif layer is None:
        assert row0 % tn == 0
        b_spec = pl.BlockSpec((tn, k), lambda i, j: (row0 // tn + j, 0))
    else:
        b_spec = pl.BlockSpec((pl.Element(1), pl.Element(tn), pl.Element(k)),
                              lambda i, j: (layer, 8 * (row0 // 8 + j * (tn // 8)), 0))
    in_specs = [pl.BlockSpec((tm, k), lambda i, j: (i, 0)), b_spec]
    args = [a, b]
    if bias is not None:
        kern = functools.partial(_mm_bias_sigmoid_kernel, b_transposed=b_transposed)
        in_specs.append(pl.BlockSpec((1, tn), lambda i, j: (0, j)))
        args.append(bias.reshape(1, n))
    else:
        kern = functools.partial(_mm_kernel, act=act, b_transposed=b_transposed)
    return pl.pallas_call(
        kern,
        grid=(m // tm, n // tn),
        in_specs=in_specs,
        out_specs=pl.BlockSpec((tm, tn), lambda i, j: (i, j)),
        out_shape=jax.ShapeDtypeStruct((m, n), out_dtype),
        compiler_params=_cparams("parallel", "parallel"),
        name="matmul",
    )(*args)


def _mm3_kernel(a_ref, alo_ref, b_ref, blo_ref, o_ref):
    a = a_ref[...]
    b = b_ref[...]
    o_ref[...] = _dot(a, b) + _dot(alo_ref[...], b) + _dot(a, blo_ref[...])


def matmul_split(a, a_lo, b):
    m, k = a.shape
    n = b.shape[1]
    b_hi = b.astype(bf16)
    b_lo = (b - b_hi.astype(f32)).astype(bf16)
    tm = _pick(m, (1024, 512, 256))
    row = pl.BlockSpec((tm, k), lambda i: (i, 0))
    col = pl.BlockSpec((k, n), lambda i: (0, 0))
    return pl.pallas_call(
        _mm3_kernel,
        grid=(m // tm,),
        in_specs=[row, row, col, col],
        out_specs=pl.BlockSpec((tm, n), lambda i: (i, 0)),
        out_shape=jax.ShapeDtypeStruct((m, n), f32),
        compiler_params=_cparams("parallel"),
        name="matmul_split",
    )(a, a_lo, b_hi, b_lo)


def _mm_acc_kernel(a_ref, b_ref, o_ref, acc_ref):
    kk = pl.program_id(2)

    @pl.when(kk == 0)
    def _():
        acc_ref[...] = jnp.zeros_like(acc_ref)

    acc_ref[...] += _dot(a_ref[...], _weights(b_ref))

    @pl.when(kk == pl.num_programs(2) - 1)
    def _():
        o_ref[...] = acc_ref[...]


def matmul_ksplit(a, b, layer=None):
    m, k = a.shape
    n = b.shape[-1]
    tm = _pick(m, (1024, 512, 256))
    tn = _pick(n, (1024, 512, 256))
    tk = _pick(k, (2048, 1024, 512, 256))
    return pl.pallas_call(
        _mm_acc_kernel,
        grid=(m // tm, n // tn, k // tk),
        in_specs=[pl.BlockSpec((tm, tk), lambda i, j, kk: (i, kk)),
                  _weight_spec(layer, (tk, tn), lambda i, j, kk: (kk, j))],
        out_specs=pl.BlockSpec((tm, tn), lambda i, j, kk: (i, j)),
        out_shape=jax.ShapeDtypeStruct((m, n), f32),
        scratch_shapes=[pltpu.VMEM((tm, tn), f32)],
        compiler_params=_cparams("parallel", "parallel", "arbitrary"),
        name="matmul_ksplit",
    )(a, b)


def _merge_kernel(ya_ref, yr_ref, ym_ref, wa_ref, wr_ref, wm_ref, ga_ref, gr_ref, gm_ref, o_ref):
    acc = ga_ref[...].astype(f32) * _dot(ya_ref[...], _weights(wa_ref))
    acc += gr_ref[...].astype(f32) * _dot(yr_ref[...], _weights(wr_ref))
    acc += gm_ref[...].astype(f32) * _dot(ym_ref[...], _weights(wm_ref))
    o_ref[...] = acc.astype(o_ref.dtype)


def merge_branches(y_mla, y_rwkv, y_mlstm, w_mla, w_rwkv, w_mlstm, gates, layer):
    m = y_mla.shape[0]
    d = w_mla.shape[-1]
    tm = _pick(m, (1024, 512, 256))
    tn = _pick(d, (512, 256))
    nj = d // tn
    ys = [pl.BlockSpec((tm, y.shape[1]), lambda i, j: (i, 0)) for y in (y_mla, y_rwkv, y_mlstm)]
    ws = [_weight_spec(layer, (w.shape[1], tn), lambda i, j: (0, j)) for w in (w_mla, w_rwkv, w_mlstm)]
    gs = [pl.BlockSpec((tm, tn), functools.partial(lambda i, j, b: (i, j + b * nj), b=b)) for b in range(N_BRANCH)]
    return pl.pallas_call(
        _merge_kernel,
        grid=(m // tm, nj),
        in_specs=ys + ws + gs,
        out_specs=pl.BlockSpec((tm, tn), lambda i, j: (i, j)),
        out_shape=jax.ShapeDtypeStruct((m, d), bf16),
        compiler_params=_cparams("parallel", "parallel"),
        name="merge_branches",
    )(y_mla, y_rwkv, y_mlstm, w_mla, w_rwkv, w_mlstm, gates, gates, gates)


def _rope_group(p, c4, s4):
    return p * c4 + pltpu.roll(p, 2 * ROPE_HALF, axis=1) * s4


def _mla_prep_kernel(uq_ref, ukv_ref, c4_ref, s4_ref, wuk_ref, g_ref, ql_ref, qp_ref, kc_ref, kpe_ref, ckv_ref, kpef_ref):
    c4 = c4_ref[...]
    s4 = s4_ref[...]
    nope = MLA_HEADS * MLA_NOPE
    for h in range(MLA_HEADS):
        qn = uq_ref[:, h * MLA_NOPE:(h + 1) * MLA_NOPE].astype(bf16)
        ql = _dot(qn, wuk_ref[h]) * MLA_SCALE
        ql_ref[:, h * KV_RANK:(h + 1) * KV_RANK] = ql.astype(ql_ref.dtype)
        qp = _rope_group(uq_ref[:, nope + h * LANES:nope + (h + 1) * LANES], c4, s4) * MLA_SCALE
        qp_ref[:, h * LANES:(h + 1) * LANES] = qp.astype(qp_ref.dtype)
    ckv = _rms(ukv_ref[:, :KV_RANK], g_ref[...])
    ckv_ref[...] = ckv
    kc_ref[...] = ckv.astype(bf16)
    kpe = _rope_group(ukv_ref[:, KV_RANK:], c4, s4)
    kpe_ref[...] = kpe.astype(bf16)
    kpef_ref[...] = kpe[:, :MLA_ROPE]


def mla_prep(u_q, u_kv, c4, s4, w_uk_t, g_ckv, row0, rows, q_dtype):
    tb = _pick(rows, (256, 128))
    assert row0 % tb == 0
    off = row0 // tb

    def rowspec(cols):
        return pl.BlockSpec((tb, cols), lambda i: (i + off, 0))

    def outspec(cols):
        return pl.BlockSpec((tb, cols), lambda i: (i, 0))

    return pl.pallas_call(
        _mla_prep_kernel,
        grid=(rows // tb,),
        in_specs=[rowspec(UQ_COLS), rowspec(UKV_COLS), rowspec(LANES), rowspec(LANES),
                  pl.BlockSpec((MLA_HEADS, MLA_NOPE, KV_RANK), lambda i: (0, 0, 0)),
                  pl.BlockSpec((1, KV_RANK), lambda i: (0, 0))],
        out_specs=[outspec(QL_COLS), outspec(QP_COLS), outspec(KV_RANK), outspec(LANES), outspec(KV_RANK), outspec(MLA_ROPE)],
        out_shape=[jax.ShapeDtypeStruct((rows, QL_COLS), q_dtype), jax.ShapeDtypeStruct((rows, QP_COLS), q_dtype),
                   jax.ShapeDtypeStruct((rows, KV_RANK), bf16), jax.ShapeDtypeStruct((rows, LANES), bf16),
                   jax.ShapeDtypeStruct((rows, KV_RANK), f32), jax.ShapeDtypeStruct((rows, MLA_ROPE), f32)],
        compiler_params=_cparams("parallel"),
        name="mla_prep",
    )(u_q, u_kv, c4, s4, w_uk_t, g_ckv.reshape(1, KV_RANK))


def _attn_prompt_kernel(ql_ref, qp_ref, kc_ref, kpe_ref, wuv_ref, o_ref, m_sc, l_sc, acc_sc, *, tq, tk):
    qi = pl.program_id(1)
    ki = pl.program_id(2)

    @pl.when(ki == 0)
    def _():
        m_sc[...] = jnp.full_like(m_sc, -jnp.inf)
        l_sc[...] = jnp.zeros_like(l_sc)
        acc_sc[...] = jnp.zeros_like(acc_sc)

    @pl.when(ki <= qi)
    def _():
        kc = kc_ref[...]
        kpe = kpe_ref[...]
        rel = lax.broadcasted_iota(jnp.int32, (tq, tk), 1) - lax.broadcasted_iota(jnp.int32, (tq, tk), 0)
        visible = rel <= (qi * tq - ki * tk)
        for h in range(MLA_HEADS):
            s = _dot_nt(ql_ref[:, h * KV_RANK:(h + 1) * KV_RANK], kc)
            s += _dot_nt(qp_ref[:, h * LANES:(h + 1) * LANES], kpe)
            s = jnp.where(visible, s, -jnp.inf)
            m_prev = m_sc[h]
            m_new = jnp.maximum(m_prev, jnp.max(s, axis=1, keepdims=True))
            alpha = jnp.exp(m_prev - m_new)
            p = jnp.exp(s - m_new[:, :1])
            l_sc[h] = alpha * l_sc[h] + jnp.sum(p, axis=1, keepdims=True)
            acc_sc[h] = acc_sc[h] * alpha[:, :1] + _dot(p.astype(bf16), kc)
            m_sc[h] = m_new

    @pl.when(ki == qi)
    def _():
        for h in range(MLA_HEADS):
            o = acc_sc[h] / l_sc[h][:, :1]
            o_ref[:, h * MLA_V:(h + 1) * MLA_V] = _dot(o.astype(bf16), wuv_ref[h]).astype(o_ref.dtype)


def attn_prompt(ql, qp, kc, kpe, w_uv_h, tq):
    tk = tq
    nq = SEQ // tq

    def qmap(b, qi, ki):
        return (b * nq + qi, 0)

    def kmap(b, qi, ki):
        return (b * nq + jnp.minimum(ki, qi), 0)

    return pl.pallas_call(
        functools.partial(_attn_prompt_kernel, tq=tq, tk=tk),
        grid=(BATCH, nq, nq),
        in_specs=[pl.BlockSpec((tq, QL_COLS), qmap), pl.BlockSpec((tq, QP_COLS), qmap),
                  pl.BlockSpec((tk, KV_RANK), kmap), pl.BlockSpec((tk, LANES), kmap),
                  pl.BlockSpec((MLA_HEADS, KV_RANK, MLA_V), lambda b, qi, ki: (0, 0, 0))],
        out_specs=pl.BlockSpec((tq, MLA_HEADS * MLA_V), qmap),
        out_shape=jax.ShapeDtypeStruct((M_PROMPT, MLA_HEADS * MLA_V), bf16),
        scratch_shapes=[pltpu.VMEM((MLA_HEADS, tq, LANES), f32), pltpu.VMEM((MLA_HEADS, tq, LANES), f32),
                        pltpu.VMEM((MLA_HEADS, tq, KV_RANK), f32)],
        compiler_params=_cparams("parallel", "parallel", "arbitrary"),
        name="attn_prompt",
    )(ql, qp, kc, kpe, w_uv_h)


def _attn_sample_kernel(pt_ref, ql_ref, qp_ref, cn_ref, pn_ref, wuv_ref, ckv_hbm, kpe_hbm, o_ref,
                        q_sc, qp_sc, m_sc, l_sc, acc_sc, ckv_buf, kpe_buf, sem_c, sem_p, *, layer):
    pps = PAGES_PER_STEP
    nsq = SEQS_PER_STEP
    n_pg = nsq * pps
    b = pl.program_id(0)
    j = pl.program_id(1)
    nchunk = pl.num_programs(1)
    rows = MLA_HEADS * DEC_SEQ
    seqs = range(nsq)

    def page_copies(slot, idx, page):
        return (pltpu.make_async_copy(ckv_hbm.at[layer, page], ckv_buf.at[slot, idx], sem_c.at[slot]),
                pltpu.make_async_copy(kpe_hbm.at[layer, page], kpe_buf.at[slot, idx], sem_p.at[slot]))

    def start_seq(bb, jj, slot, r):
        for i in range(pps):
            idx = r * pps + i
            for cp in page_copies(slot, idx, pt_ref[bb * nsq + r, jj * pps + i]):
                cp.start(priority=idx % 2)

    def wait_slot(slot):
        for idx in range(n_pg):
            for cp in page_copies(slot, idx, 0):
                cp.wait()

    t = b * nchunk + j
    slot = t % 2
    is_last = t + 1 == pl.num_programs(0) * nchunk
    last_chunk = j == nchunk - 1
    next_b = jnp.where(is_last, b, jnp.where(last_chunk, b + 1, b))
    next_j = jnp.where(is_last, j, jnp.where(last_chunk, 0, j + 1))

    @pl.when(t == 0)
    def _():
        for r in seqs:
            start_seq(b, j, slot, r)

    wait_slot(slot)
    ckv_pages = [ckv_buf.at[slot, idx] for idx in range(n_pg)]
    kpe_pages = [kpe_buf.at[slot, idx] for idx in range(n_pg)]

    @pl.when(j == 0)
    def _():
        for r in seqs:
            tok = slice(r * DEC_SEQ, (r + 1) * DEC_SEQ)
            for h in range(MLA_HEADS):
                q_sc[r, h * DEC_SEQ:(h + 1) * DEC_SEQ, :] = ql_ref[tok, h * KV_RANK:(h + 1) * KV_RANK]
                qp_sc[r, h * DEC_SEQ:(h + 1) * DEC_SEQ, :] = qp_ref[tok, h * LANES:(h + 1) * LANES]
        m_sc[...] = jnp.full_like(m_sc, -jnp.inf)
        l_sc[...] = jnp.zeros_like(l_sc)
        acc_sc[...] = jnp.zeros_like(acc_sc)

    qb = [q_sc[r].astype(bf16) for r in seqs]
    qpb = [qp_sc[r][:, :MLA_ROPE].astype(bf16) for r in seqs]

    def online(r, s, vals):
        m_prev = m_sc[r]
        m_new = jnp.maximum(m_prev, jnp.max(s, axis=1, keepdims=True))
        alpha = jnp.exp(m_prev - m_new)
        p = jnp.exp(s - m_new[:, :1])
        l_sc[r] = alpha * l_sc[r] + jnp.sum(p, axis=1, keepdims=True)
        acc_sc[r] = acc_sc[r] * alpha[:, :1] + _dot(p.astype(bf16), vals)
        m_sc[r] = m_new

    kcs, scores = [], []
    for r in seqs:
        start_seq(next_b, next_j, 1 - slot, r)
        kcs.append(jnp.concatenate([ckv_pages[r * pps + i][...].astype(bf16) for i in range(pps)], axis=0))
        kp = jnp.concatenate([kpe_pages[r * pps + i][...].astype(bf16) for i in range(pps)], axis=1)
        scores.append(_dot_nt(qb[r], kcs[r]) + _dot(qpb[r], kp))
    for r in seqs:
        online(r, scores[r], kcs[r])

    @pl.when(is_last)
    def _():
        wait_slot(1 - slot)

    @pl.when(j == pl.num_programs(1) - 1)
    def _():
        pad = jnp.zeros((LANES - DEC_SEQ, KV_RANK), f32)
        key = lax.broadcasted_iota(jnp.int32, (rows, LANES), 1)
        tokid = lax.broadcasted_iota(jnp.int32, (rows, LANES), 0) % DEC_SEQ
        for r in seqs:
            tok = slice(r * DEC_SEQ, (r + 1) * DEC_SEQ)
            cn = jnp.concatenate([cn_ref[tok, :], pad], axis=0).astype(bf16)
            pn = jnp.concatenate([pn_ref[tok, :], pad[:, :MLA_ROPE]], axis=0).astype(bf16)
            s = _dot_nt(qb[r], cn) + _dot_nt(qpb[r], pn)
            online(r, jnp.where(key <= tokid, s, -jnp.inf), cn)
            o = acc_sc[r] / l_sc[r][:, :1]
            for h in range(MLA_HEADS):
                oh = o[h * DEC_SEQ:(h + 1) * DEC_SEQ, :].astype(bf16)
                o_ref[tok, h * MLA_V:(h + 1) * MLA_V] = _dot(oh, wuv_ref[h])


def attn_sample(ql, qp, c_new, p_new, w_uv_h, cache_ckv, cache_kpe_t, page_table, layer):
    pps = PAGES_PER_STEP
    nsq = SEQS_PER_STEP
    assert N_PAGES % pps == 0 and DEC_BATCH % nsq == 0
    nchunk = N_PAGES // pps
    rows = MLA_HEADS * DEC_SEQ

    def seqspec(cols):
        return pl.BlockSpec((nsq * DEC_SEQ, cols), lambda b, j, pt: (b, 0))

    n_pg = nsq * pps
    in_specs = [seqspec(QL_COLS), seqspec(QP_COLS), seqspec(KV_RANK), seqspec(MLA_ROPE),
                pl.BlockSpec((MLA_HEADS, KV_RANK, MLA_V), lambda b, j, pt: (0, 0, 0)),
                pl.BlockSpec(memory_space=pl.ANY), pl.BlockSpec(memory_space=pl.ANY)]
    grid_spec = pltpu.PrefetchScalarGridSpec(
        num_scalar_prefetch=1,
        grid=(DEC_BATCH // nsq, nchunk),
        in_specs=in_specs,
        out_specs=pl.BlockSpec((nsq * DEC_SEQ, MLA_HEADS * MLA_V), lambda b, j, pt: (b, 0)),
        scratch_shapes=[pltpu.VMEM((nsq, rows, KV_RANK), f32), pltpu.VMEM((nsq, rows, LANES), f32),
                        pltpu.VMEM((nsq, rows, LANES), f32), pltpu.VMEM((nsq, rows, LANES), f32),
                        pltpu.VMEM((nsq, rows, KV_RANK), f32),
                        pltpu.VMEM((2, n_pg, PAGE_SIZE, KV_RANK), f32), pltpu.VMEM((2, n_pg, MLA_ROPE, PAGE_SIZE), f32),
                        pltpu.SemaphoreType.DMA((2,)), pltpu.SemaphoreType.DMA((2,))],
    )
    return pl.pallas_call(
        functools.partial(_attn_sample_kernel, layer=layer),
        grid_spec=grid_spec,
        out_shape=jax.ShapeDtypeStruct((M_SAMPLE, MLA_HEADS * MLA_V), f32),
        compiler_params=_cparams("arbitrary", "arbitrary"),
        name="attn_sample",
    )(page_table, ql, qp, c_new, p_new, w_uv_h, cache_ckv, cache_kpe_t)


def _head_sums(x, bd):
    hi = x.astype(bf16)
    lo = (x - hi.astype(f32)).astype(bf16)
    parts = []
    for g in range(x.shape[1] // MXU_DIM):
        sl = slice(g * MXU_DIM, (g + 1) * MXU_DIM)
        parts.append(_dot(hi[:, sl], bd) + _dot(lo[:, sl], bd))
    return jnp.concatenate(parts, axis=1)


def _rwkv_prep_kernel(u_ref, st_ref, mu_ref, w0_ref, a0_ref, kkw_ref, ka_ref, rk_ref, wwa_ref, g2_ref, bd_ref,
                      r_ref, k_ref, v_ref, p_ref, q_ref, w_ref, gc_ref, g_ref, bonus_ref, *, tb, n_prompt_blocks):
    w_ = RWKV_WIDTH
    nst = st_ref.shape[0]
    is_sample = pl.program_id(0) >= n_prompt_blocks
    stride = jnp.where(is_sample, DEC_SEQ, tb)
    row_e = lax.broadcasted_iota(jnp.int32, (tb, nst), 0)
    col_e = lax.broadcasted_iota(jnp.int32, (tb, nst), 1)
    place = jnp.where(row_e == col_e * stride, 1.0, 0.0).astype(bf16)
    starts = _dot_exact_rhs(place, st_ref[...])
    u = u_ref[...]
    row = lax.broadcasted_iota(jnp.int32, (tb, 1), 0)
    is_start = jnp.where(is_sample, row % DEC_SEQ, row) == 0
    prev = jnp.where(is_start, starts, pltpu.roll(u, 1, axis=0))
    z = u + (prev - u) * mu_ref[...]
    r = z[:, :w_]
    k = z[:, w_:2 * w_]
    v = z[:, 2 * w_:3 * w_]
    wa = z[:, 3 * w_:3 * w_ + W_LORA + A_LORA]
    gd = z[:, 3 * w_ + W_LORA + A_LORA:]
    lane = lax.broadcasted_iota(jnp.int32, wa.shape, 1)
    wa = jnp.where(lane < W_LORA, jnp.tanh(wa), wa)
    lora = _dot(wa.astype(bf16), wwa_ref[...])
    w_log = -jax.nn.softplus(-(w0_ref[...] + lora[:, :w_])) - 0.5
    a = jax.nn.sigmoid(a0_ref[...] + lora[:, w_:])
    g = _dot(jax.nn.sigmoid(gd).astype(bf16), g2_ref[...])
    bd = bd_ref[...]
    kk = k * kkw_ref[...]
    kk = kk / jnp.maximum(jnp.sqrt(_head_sums(kk * kk, bd)), 1e-12)
    k = k * (1.0 + (a - 1.0) * ka_ref[...])
    bonus = _head_sums(r * k * rk_ref[...], bd) * v
    g_ref[...] = g
    bonus_ref[...] = bonus
    wdec = -jnp.exp(w_log)
    shift = jnp.where(is_sample, DEC_SEQ.bit_length() - 1, RWKV_CHUNK.bit_length() - 1)
    row_c = lax.broadcasted_iota(jnp.int32, (tb, tb), 0)
    col_c = lax.broadcasted_iota(jnp.int32, (tb, tb), 1)
    same_chunk = lax.shift_right_logical(row_c, shift) == lax.shift_right_logical(col_c, shift)
    tril_blk = jnp.where(same_chunk, jnp.where(col_c <= row_c, 1.0, 0.0), 0.0).astype(bf16)
    gcum = _dot_exact_rhs(tril_blk, wdec)
    for h in range(RWKV_HEADS):
        sl = slice(h * RWKV_HEAD, (h + 1) * RWKV_HEAD)
        r_ref[h] = r[:, sl]
        k_ref[h] = k[:, sl]
        v_ref[h] = v[:, sl]
        p_ref[h] = -kk[:, sl]
        q_ref[h] = (kk * a)[:, sl]
        w_ref[h] = wdec[:, sl]
        gc_ref[h] = gcum[:, sl]


def rwkv_prep(u, starts, lp_rows, w_wa, g2, bd, tb):
    m = u.shape[0]
    nst = starts.shape[1]
    rowc = pl.BlockSpec((tb, RWKV_COLS), lambda i: (i, 0))

    def vec(n):
        return pl.BlockSpec((1, n), lambda i: (0, 0))

    def full(a):
        return pl.BlockSpec(a.shape, lambda i: (0, 0))

    headmajor = pl.BlockSpec((RWKV_HEADS, tb, RWKV_HEAD), lambda i: (0, i, 0))
    tokmajor = pl.BlockSpec((tb, RWKV_WIDTH), lambda i: (i, 0))
    hm_shape = jax.ShapeDtypeStruct((RWKV_HEADS, m, RWKV_HEAD), f32)
    tm_shape = jax.ShapeDtypeStruct((m, RWKV_WIDTH), f32)
    return pl.pallas_call(
        functools.partial(_rwkv_prep_kernel, tb=tb, n_prompt_blocks=M_PROMPT // tb),
        grid=(m // tb,),
        in_specs=[rowc, pl.BlockSpec((None, nst, RWKV_COLS), lambda i: (i, 0, 0)), vec(RWKV_COLS)]
        + [vec(RWKV_WIDTH)] * 5 + [full(w_wa), full(g2), full(bd)],
        out_specs=[headmajor] * 7 + [tokmajor] * 2,
        out_shape=[hm_shape] * 7 + [tm_shape] * 2,
        compiler_params=_cparams("parallel"),
        name="rwkv_prep",
    )(u, starts, *lp_rows, w_wa, g2, bd)


def _tri_consts(csz):
    row = lax.broadcasted_iota(jnp.int32, (csz, csz), 0)
    col = lax.broadcasted_iota(jnp.int32, (csz, csz), 1)
    return (col <= row).astype(f32), (col < row).astype(f32), (col == row).astype(f32)


def _rwkv_phase1_kernel(r_ref, k_ref, v_ref, p_ref, q_ref, w_ref, g_ref, rhat_ref, yint_ref, mlow_ref, nn_ref, e_ref,
                        *, csz, nck):
    tril_incl, tril_strict, eye = _tri_consts(csz)
    cs = range(nck)
    sls = [slice(c * csz, (c + 1) * csz) for c in cs]

    def stack(a, b):
        return jnp.concatenate([a, b], axis=0).astype(bf16)

    pr, qt, kt, p0, r0, qh, kh, vb, e_last = [], [], [], [], [], [], [], [], []
    for sl in sls:
        r, k, v, p, q = r_ref[sl, :], k_ref[sl, :], v_ref[sl, :], p_ref[sl, :], q_ref[sl, :]
        g = g_ref[sl, :]
        gp = g - w_ref[sl, :]
        ref = g[csz // 2 - 1:csz // 2, :]
        gl = g[csz - 1:csz, :]
        e_in = jnp.exp(ref - g)
        e_out = jnp.exp(gl - g)
        pr.append(stack(p * jnp.exp(gp - ref), r * jnp.exp(g - ref)))
        qt.append((q * e_in).astype(bf16))
        kt.append((k * e_in).astype(bf16))
        p0.append((p * jnp.exp(gp)).astype(bf16))
        r0.append(r * jnp.exp(g))
        qh.append((q * e_out).astype(bf16))
        kh.append((k * e_out).astype(bf16))
        vb.append(v.astype(bf16))
        e_last.append(jnp.exp(gl))
    gq = [_dot_nt(pr[c], qt[c]) for c in cs]
    gk = [_dot_nt(pr[c], kt[c]) for c in cs]
    a_qp = [gq[c][:csz] * tril_strict for c in cs]
    a_kp = [(gk[c][:csz] * tril_strict).astype(bf16) for c in cs]
    a_qr = [(gq[c][csz:] * tril_incl).astype(bf16) for c in cs]
    a_kr = [(gk[c][csz:] * tril_incl).astype(bf16) for c in cs]
    tinv = [eye + a_qp[c] for c in cs]
    if csz > 2:
        apb = [a_qp[c].astype(bf16) for c in cs]
        apow = [_dot(apb[c], apb[c]) for c in cs]
        n = 2
        while n < csz:
            last = 2 * n >= csz
            rhs = [apow[c].astype(bf16) for c in cs]
            if last:
                prod = [_dot(tinv[c].astype(bf16), rhs[c]) for c in cs]
                tinv = [tinv[c] + prod[c] for c in cs]
            else:
                prod = [_dot(stack(tinv[c], apow[c]), rhs[c]) for c in cs]
                tinv = [tinv[c] + prod[c][:csz] for c in cs]
                apow = [prod[c][csz:] for c in cs]
            n *= 2
    tb_ = [tinv[c].astype(bf16) for c in cs]
    akpv = [_dot(a_kp[c], vb[c]).astype(bf16) for c in cs]
    phb = [_dot(tb_[c], p0[c]).astype(bf16) for c in cs]
    wvb = [_dot(tb_[c], akpv[c]).astype(bf16) for c in cs]
    rhat = [r0[c] + _dot(a_qr[c], phb[c]) for c in cs]
    yint = [_dot(a_qr[c], wvb[c]) + _dot(a_kr[c], vb[c]) for c in cs]
    mlow = [_dot_tn(phb[c], qh[c]) for c in cs]
    nn = [_dot_tn(wvb[c], qh[c]) + _dot_tn(vb[c], kh[c]) for c in cs]
    for c in cs:
        rhat_ref[sls[c], :] = rhat[c]
        yint_ref[sls[c], :] = yint[c]
        mlow_ref[c] = mlow[c]
        nn_ref[c] = nn[c]
        e_ref[c] = jnp.broadcast_to(e_last[c], (8, RWKV_HEAD))


def rwkv_phase1(r, k, v, p, q, w, g, row0, rows, csz, nck):
    tb = nck * csz
    assert rows % tb == 0 and row0 % tb == 0
    off = row0 // tb
    nchunks = rows // csz
    hm_in = pl.BlockSpec((None, tb, RWKV_HEAD), lambda h, i: (h, i + off, 0))
    hm_out = pl.BlockSpec((None, tb, RWKV_HEAD), lambda h, i: (h, i, 0))
    sq = pl.BlockSpec((None, nck, RWKV_HEAD, RWKV_HEAD), lambda h, i: (h, i, 0, 0))
    ev = pl.BlockSpec((None, nck, 8, RWKV_HEAD), lambda h, i: (h, i, 0, 0))
    return pl.pallas_call(
        functools.partial(_rwkv_phase1_kernel, csz=csz, nck=nck),
        grid=(RWKV_HEADS, rows // tb),
        in_specs=[hm_in] * 7,
        out_specs=[hm_out, hm_out, sq, sq, ev],
        out_shape=[jax.ShapeDtypeStruct((RWKV_HEADS, rows, RWKV_HEAD), f32)] * 2
        + [jax.ShapeDtypeStruct((RWKV_HEADS, nchunks, RWKV_HEAD, RWKV_HEAD), f32)] * 2
        + [jax.ShapeDtypeStruct((RWKV_HEADS, nchunks, 8, RWKV_HEAD), f32)],
        compiler_params=_cparams("parallel", "parallel"),
        name="rwkv_phase1",
    )(r, k, v, p, q, w, g)


def _rwkv_phase2_kernel(*refs, csz, nck, n_alias):
    s0_ref, rhat_ref, yint_ref, mlow_ref, nn_ref, e_ref = refs[:6]
    y_ref, sout_ref, s_sc = refs[6 + n_alias:]
    ci = pl.program_id(1)
    hs = range(RWKV_HEADS)

    @pl.when(ci == 0)
    def _():
        s_sc[...] = s0_ref[...]

    s = [s_sc[h] for h in hs]
    for c in range(nck):
        sl = slice(c * csz, (c + 1) * csz)
        sb = [s[h].astype(bf16) for h in hs]
        ys = [_dot_nt(rhat_ref[h, sl, :].astype(bf16), sb[h]) for h in hs]
        sm = [_dot(sb[h], mlow_ref[h, c].astype(bf16)) for h in hs]
        for h in hs:
            y_ref[h, sl, :] = ys[h] + yint_ref[h, sl, :]
        s = [s[h] * e_ref[h, c][:1, :] + sm[h] + nn_ref[h, c] for h in hs]
    for h in hs:
        s_sc[h] = s[h]

    @pl.when(ci == pl.num_programs(1) - 1)
    def _():
        sout_ref[...] = s_sc[...]


def rwkv_phase2(s0, rhat, yint, mlow, nn, e, nseq, tlen, csz, layer=None, s_stack=None):
    cps = tlen // csz
    nck = min(cps, 4)
    assert cps % nck == 0
    nsteps = cps // nck
    tb = nck * csz
    hm = pl.BlockSpec((RWKV_HEADS, tb, RWKV_HEAD), lambda s, c: (0, s * nsteps + c, 0))
    sq = pl.BlockSpec((RWKV_HEADS, nck, RWKV_HEAD, RWKV_HEAD), lambda s, c: (0, s * nsteps + c, 0, 0))
    ev = pl.BlockSpec((RWKV_HEADS, nck, 8, RWKV_HEAD), lambda s, c: (0, s * nsteps + c, 0, 0))
    s_shape = (RWKV_HEADS, RWKV_HEAD, RWKV_HEAD)
    if layer is None:
        st_in = st_out = pl.BlockSpec((None,) + s_shape, lambda s, c: (s, 0, 0, 0))
        st_out_shape = jax.ShapeDtypeStruct((nseq,) + s_shape, f32)
        extra_specs, extra_args = [], []
    else:
        st_in = pl.BlockSpec((None, None) + s_shape, lambda s, c: (layer, s, 0, 0, 0))
        st_out, st_out_shape, extra_specs, extra_args = _stacked_state_out(layer, s_stack, nseq, s_shape)
    n_in = 6
    return pl.pallas_call(
        functools.partial(_rwkv_phase2_kernel, csz=csz, nck=nck, n_alias=len(extra_args)),
        grid=(nseq, nsteps),
        in_specs=[st_in, hm, hm, sq, sq, ev] + extra_specs,
        out_specs=[hm, st_out],
        out_shape=[jax.ShapeDtypeStruct((RWKV_HEADS, nseq * tlen, RWKV_HEAD), f32), st_out_shape],
        scratch_shapes=[pltpu.VMEM(s_shape, f32)],
        input_output_aliases={n_in: 1} if extra_args else {},
        compiler_params=_cparams("parallel", "arbitrary"),
        name="rwkv_phase2",
    )(s0, rhat, yint, mlow, nn, e, *extra_args)


def _rwkv_out_kernel(y_ref, g_ref, bonus_ref, lnw_ref, lnb_ref, bd_ref, o_ref, y_sc):
    for h in range(RWKV_HEADS):
        y_sc[:, h * RWKV_HEAD:(h + 1) * RWKV_HEAD] = y_ref[h]
    y = y_sc[...]
    bd = bd_ref[...]
    mean = _head_sums(y, bd) * (1.0 / RWKV_HEAD)
    yc = y - mean
    var = _head_sums(yc * yc, bd) * (1.0 / RWKV_HEAD)
    yn = yc * lax.rsqrt(var + RWKV_GN_EPS) * lnw_ref[...] + lnb_ref[...]
    o_ref[...] = ((yn + bonus_ref[...]) * g_ref[...]).astype(o_ref.dtype)


def rwkv_out(y_hm, g, bonus, ln_w, ln_b, bd):
    m = g.shape[0]
    tb = _pick(m, (256, 128))
    tok = pl.BlockSpec((tb, RWKV_WIDTH), lambda i: (i, 0))
    vec = pl.BlockSpec((1, RWKV_WIDTH), lambda i: (0, 0))
    return pl.pallas_call(
        _rwkv_out_kernel,
        grid=(m // tb,),
        in_specs=[pl.BlockSpec((RWKV_HEADS, tb, RWKV_HEAD), lambda i: (0, i, 0)), tok, tok, vec, vec,
                  pl.BlockSpec(bd.shape, lambda i: (0, 0))],
        out_specs=tok,
        out_shape=jax.ShapeDtypeStruct((m, RWKV_WIDTH), bf16),
        scratch_shapes=[pltpu.VMEM((tb, RWKV_WIDTH), f32)],
        compiler_params=_cparams("parallel"),
        name="rwkv_out",
    )(y_hm, g, bonus, ln_w.reshape(1, -1), ln_b.reshape(1, -1), bd)


def _mlstm_kernel(*refs, csz, n_alias):
    u_ref, ug_ref, gb_ref, ng_ref, c0_ref, n0_ref, m0_ref = refs[:7]
    y_ref, cout_ref, nout_ref, mout_ref, c_sc, n_sc, m_sc = refs[7 + n_alias:]
    ci = pl.program_id(1)
    dh = MLSTM_DH
    w_ = MLSTM_WIDTH

    @pl.when(ci == 0)
    def _():
        c_sc[...] = c0_ref[...]
        n_sc[...] = n0_ref[...]
        m_sc[...] = m0_ref[...]

    tril_incl, _, _ = _tri_consts(csz)
    causal = tril_incl > 0.0
    gates = ug_ref[...] + gb_ref[...]
    lane = lax.broadcasted_iota(jnp.int32, gates.shape, 1)
    logf = jax.nn.log_sigmoid(gates)
    fcum = _dot_exact_rhs(tril_incl.astype(bf16), logf)
    cols = jnp.where(lane < MLSTM_HEADS, gates, fcum)
    sel = (lax.broadcasted_iota(jnp.int32, (8, LANES), 0) == lax.broadcasted_iota(jnp.int32, (8, LANES), 1))
    hi, mid, lo = _split3(cols)
    selb = sel.astype(bf16)
    rows = _dot_nt(selb, hi) + _dot_nt(selb, mid) + _dot_nt(selb, lo)
    hs = range(MLSTM_HEADS)
    q = [u_ref[:, h * dh:(h + 1) * dh] for h in hs]
    k = [u_ref[:, w_ + h * dh:w_ + (h + 1) * dh] * (dh ** -0.5) for h in hs]
    v = [u_ref[:, 2 * w_ + h * dh:2 * w_ + (h + 1) * dh] for h in hs]
    qb = [q[h].astype(bf16) for h in hs]
    kb = [k[h].astype(bf16) for h in hs]
    vb = [v[h].astype(bf16) for h in hs]
    c_old = [c_sc[h] for h in hs]
    n_old = [n_sc[h:h + 1, :] for h in hs]
    m_old = [m_sc[h:h + 1, :1] for h in hs]
    ig_col = [cols[:, h:h + 1] for h in hs]
    f_col = [cols[:, MLSTM_HEADS + h:MLSTM_HEADS + h + 1] for h in hs]
    d_ts = [jnp.where(causal, f_col[h] - rows[MLSTM_HEADS + h:MLSTM_HEADS + h + 1, :] + rows[h:h + 1, :], -jnp.inf)
            for h in hs]
    inter = [m_old[h] + f_col[h] for h in hs]
    m_t = [jnp.maximum(inter[h], jnp.max(d_ts[h], axis=1, keepdims=True)) for h in hs]
    w_inter = [jnp.exp(inter[h] - m_t[h]) for h in hs]
    qk = [_dot_nt(qb[h], kb[h]) for h in hs]
    cq = [_dot_nt(qb[h], c_old[h].astype(bf16)) for h in hs]
    a = [jnp.exp(d_ts[h] - m_t[h]) * qk[h] for h in hs]
    av = [_dot(a[h].astype(bf16), vb[h]) for h in hs]
    m_new = [m_t[h][csz - 1:csz, :] for h in hs]
    f_last = [f_col[h][csz - 1:csz, :] for h in hs]
    carry = [jnp.exp(m_old[h] + f_last[h] - m_new[h]) for h in hs]
    w_write = [jnp.exp(f_last[h] - f_col[h] + ig_col[h] - m_new[h]) for h in hs]
    vk = [_dot_tn((v[h] * w_write[h]).astype(bf16), kb[h]) for h in hs]
    for h in hs:
        num = w_inter[h] * cq[h] + av[h]
        den = w_inter[h] * jnp.sum(q[h] * n_old[h], axis=1, keepdims=True) + jnp.sum(a[h], axis=1, keepdims=True)
        hh = num / jnp.maximum(jnp.abs(den), jnp.exp(-m_t[h]))
        hh = hh * lax.rsqrt(jnp.mean(hh * hh, axis=1, keepdims=True) + NORM_EPS)
        og = jax.nn.sigmoid(u_ref[:, 3 * w_ + h * dh:3 * w_ + (h + 1) * dh])
        y_ref[:, h * dh:(h + 1) * dh] = (og * hh * ng_ref[:, h * dh:(h + 1) * dh]).astype(y_ref.dtype)
        c_sc[h] = carry[h] * c_old[h] + vk[h]
        n_sc[h:h + 1, :] = carry[h] * n_old[h] + jnp.sum(k[h] * w_write[h], axis=0, keepdims=True)
        m_sc[h:h + 1, :] = jnp.broadcast_to(m_new[h], (1, LANES))

    @pl.when(ci == pl.num_programs(1) - 1)
    def _():
        cout_ref[...] = c_sc[...]
        nout_ref[...] = n_sc[...]
        mout_ref[...] = m_sc[...]


def _stacked_state_out(layer, stack, nseq, shape):
    spec = pl.BlockSpec((None, None) + shape, lambda s, c: (layer, s) + (0,) * len(shape))
    out_shape = jax.ShapeDtypeStruct((DEPTH, nseq) + shape, f32)
    assert stack.shape == out_shape.shape
    return spec, out_shape, [pl.BlockSpec(memory_space=pl.ANY)], [stack]


def mlstm(u, ug, gate_bias_row, norm_g, c0, n0, m0, row0, nseq, tlen, csz, layer=None, c_stack=None):
    cps = tlen // csz
    assert row0 % csz == 0
    off = row0 // csz

    def rowspec(cols):
        return pl.BlockSpec((csz, cols), lambda s, c: (off + s * cps + c, 0))

    def state(shape):
        return pl.BlockSpec((None,) + shape, lambda s, c: (s,) + (0,) * len(shape))

    c_shape = (MLSTM_HEADS, MLSTM_DH, MLSTM_DH)
    if layer is None:
        c_in, c_out, c_out_shape = state(c_shape), state(c_shape), jax.ShapeDtypeStruct((nseq,) + c_shape, f32)
        extra_specs, extra_args = [], []
    else:
        c_in = pl.BlockSpec((None, None) + c_shape, lambda s, c: (layer, s, 0, 0, 0))
        c_out, c_out_shape, extra_specs, extra_args = _stacked_state_out(layer, c_stack, nseq, c_shape)
    n_in = 7
    return pl.pallas_call(
        functools.partial(_mlstm_kernel, csz=csz, n_alias=len(extra_args)),
        grid=(nseq, cps),
        in_specs=[rowspec(4 * MLSTM_WIDTH), rowspec(LANES), pl.BlockSpec((1, LANES), lambda s, c: (0, 0)),
                  pl.BlockSpec((1, MLSTM_WIDTH), lambda s, c: (0, 0)),
                  c_in, state((8, MLSTM_DH)), state((8, LANES))] + extra_specs,
        out_specs=[pl.BlockSpec((csz, MLSTM_WIDTH), lambda s, c: (s * cps + c, 0)),
                   c_out, state((8, MLSTM_DH)), state((8, LANES))],
        out_shape=[jax.ShapeDtypeStruct((nseq * tlen, MLSTM_WIDTH), f32), c_out_shape,
                   jax.ShapeDtypeStruct((nseq, 8, MLSTM_DH), f32), jax.ShapeDtypeStruct((nseq, 8, LANES), f32)],
        scratch_shapes=[pltpu.VMEM((MLSTM_HEADS, MLSTM_DH, MLSTM_DH), f32), pltpu.VMEM((8, MLSTM_DH), f32),
                        pltpu.VMEM((8, LANES), f32)],
        input_output_aliases={n_in: 1} if extra_args else {},
        compiler_params=_cparams("parallel", "arbitrary"),
        name="mlstm",
    )(u, ug, gate_bias_row, norm_g.reshape(1, -1), c0, n0, m0, *extra_args)


def _column_plan():
    per_head = MLA_NOPE + MLA_ROPE
    q_nope = [h * per_head + d for h in range(MLA_HEADS) for d in range(MLA_NOPE)]
    q_rope = []
    for h in range(MLA_HEADS):
        base = h * per_head + MLA_NOPE
        x1 = [base + e for e in range(ROPE_HALF)]
        x2 = [base + ROPE_HALF + e for e in range(ROPE_HALF)]
        q_rope += x1 + x2 + x2 + x1
    kv_base = MLA_Q_COLS
    ckv = [kv_base + c for c in range(KV_RANK)]
    x1 = [kv_base + KV_RANK + e for e in range(ROPE_HALF)]
    x2 = [kv_base + KV_RANK + ROPE_HALF + e for e in range(ROPE_HALF)]
    return np.asarray(q_nope + q_rope, np.int32), np.asarray(ckv + x1 + x2 + x2 + x1, np.int32)


def _rope_tables():
    pos = jnp.concatenate([jnp.tile(jnp.arange(SEQ), BATCH), jnp.tile(PAST_LEN + jnp.arange(DEC_SEQ), DEC_BATCH)])
    inv = ROPE_THETA ** (-jnp.arange(ROPE_HALF, dtype=f32) / ROPE_HALF)
    ang = pos.astype(f32)[:, None] * inv[None, :]
    cos, sin = jnp.cos(ang), jnp.sin(ang)
    zero = jnp.zeros_like(cos)
    return jnp.concatenate([cos, cos, zero, zero], axis=1), jnp.concatenate([-sin, sin, zero, zero], axis=1)


def _pad8(x, rows_axis):
    pad = [(0, 0)] * x.ndim
    pad[rows_axis] = (0, 8 - x.shape[rows_axis])
    return jnp.pad(x, pad)


def kernel(x_prompt, x_sample, cache_ckv, cache_kpe, page_table, state_rwkv_shift, state_rwkv_S, state_mlstm_C, state_mlstm_n, state_mlstm_m, norm_gains, w_in, g_ckv, w_uk, w_uv, rwkv_mu, rwkv_w0, rwkv_w2, rwkv_a0, rwkv_a2, rwkv_g2, rwkv_kk, rwkv_ka, rwkv_rk, rwkv_ln_w, rwkv_ln_b, mlstm_gate_b, mlstm_norm, gate_b, w_br_mla, w_br_rwkv, w_br_mlstm, w_out, w_up, w_down):
    d = D_MODEL
    o0 = MLA_Q_COLS
    o1 = o0 + MLA_KV_COLS
    o2 = o1 + RWKV_COLS
    o3 = o2 + MLSTM_COLS
    q_cols, kv_cols = _column_plan()
    c4, s4 = _rope_tables()
    head_id = np.arange(MXU_DIM) // RWKV_HEAD
    bd = jnp.asarray(head_id[:, None] == head_id[None, :], bf16)
    zeros_s = jnp.zeros((BATCH, RWKV_HEADS, RWKV_HEAD, RWKV_HEAD), f32)
    zeros_c = jnp.zeros((BATCH, MLSTM_HEADS, MLSTM_DH, MLSTM_DH), f32)
    zeros_n = jnp.zeros((BATCH, 8, MLSTM_DH), f32)
    zeros_m = jnp.zeros((BATCH, 8, LANES), f32)
    tq = _pick(SEQ, (512, 256))
    tb_rw = 256
    n_starts = tb_rw // DEC_SEQ
    assert SEQ % tb_rw == 0 and M_SAMPLE % tb_rw == 0
    cache_kpe_t = jnp.swapaxes(cache_kpe, 2, 3)
    w_in_t = jnp.swapaxes(w_in, 1, 2)
    w_in_rows = w_in_t.reshape(DEPTH * w_in.shape[2], d)

    x = jnp.concatenate([x_prompt.reshape(M_PROMPT, d), x_sample.reshape(M_SAMPLE, d)], axis=0)
    h, h_lo = rmsnorm_bf16(x, norm_gains[0, 0])
    prompt_states, sample_states = [], []
    s_stack = jnp.zeros((DEPTH, DEC_BATCH, RWKV_HEADS, RWKV_HEAD, RWKV_HEAD), f32)
    c_stack = jnp.zeros((DEPTH, DEC_BATCH, MLSTM_HEADS, MLSTM_DH, MLSTM_DH), f32)
    for l in range(DEPTH):
        w_q = jnp.take(w_in_rows, l * w_in.shape[2] + q_cols, axis=0)
        w_kv = jnp.take(w_in_rows, l * w_in.shape[2] + kv_cols, axis=0)
        w_mg = jnp.pad(w_in_t[l, o2 + 4 * MLSTM_WIDTH:o3].T, ((0, 0), (0, LANES - 2 * MLSTM_HEADS)))

        u_q = matmul(h, w_q, row0=0, n=UQ_COLS)
        u_kv = matmul(h, w_kv, row0=0, n=UKV_COLS)
        u_rw = matmul(h, w_in_t, layer=l, row0=o1, n=RW_PAD)
        u_ml = matmul(h, w_in_t, layer=l, row0=o2, n=4 * MLSTM_WIDTH)
        u_mg = matmul_split(h, h_lo, w_mg)
        gates = matmul(h, w_in_t, out_dtype=bf16, bias=gate_b[l].reshape(-1), layer=l, row0=o3, n=GATE_COLS)

        w_uk_t = jnp.transpose(w_uk[l], (1, 2, 0)).astype(bf16)
        w_uv_h = jnp.transpose(w_uv[l], (1, 0, 2)).astype(bf16)
        ql_p, qp_p, kc_p, kpe_p, ckv_p, kpef_p = mla_prep(u_q, u_kv, c4, s4, w_uk_t, g_ckv[l], 0, M_PROMPT, bf16)
        ql_s, qp_s, _, _, ckv_s, kpef_s = mla_prep(u_q, u_kv, c4, s4, w_uk_t, g_ckv[l], M_PROMPT, M_SAMPLE, f32)
        y_mla_p = attn_prompt(ql_p, qp_p, kc_p, kpe_p, w_uv_h, tq)
        y_mla_s = attn_sample(ql_s, qp_s, ckv_s, kpef_s, w_uv_h, cache_ckv, cache_kpe_t, page_table, l)
        y_mla = jnp.concatenate([y_mla_p, y_mla_s.astype(bf16)], axis=0)

        block_last = u_rw[tb_rw - 1:M_PROMPT:tb_rw, :RWKV_COLS].reshape(BATCH, SEQ // tb_rw, 1, RWKV_COLS)
        before = jnp.concatenate([jnp.zeros((BATCH, 1, 1, RWKV_COLS), f32), block_last[:, :-1]], axis=1)
        starts = jnp.concatenate([
            jnp.pad(before.reshape(M_PROMPT // tb_rw, 1, RWKV_COLS), ((0, 0), (0, n_starts - 1), (0, 0))),
            state_rwkv_shift[l].reshape(M_SAMPLE // tb_rw, n_starts, RWKV_COLS)], axis=0)
        zero_blk = jnp.zeros((W_LORA, RWKV_WIDTH), f32)
        w_wa = jnp.concatenate([jnp.concatenate([rwkv_w2[l], zero_blk], axis=1),
                                jnp.concatenate([zero_blk, rwkv_a2[l]], axis=1)], axis=0).astype(bf16)
        lp_rows = [rwkv_mu[l].reshape(1, -1), rwkv_w0[l].reshape(1, -1), rwkv_a0[l].reshape(1, -1),
                   rwkv_kk[l].reshape(1, -1), rwkv_ka[l].reshape(1, -1), rwkv_rk[l].reshape(1, -1)]
        ops = rwkv_prep(u_rw, starts, lp_rows, w_wa, rwkv_g2[l].astype(bf16), bd, tb_rw)
        hm_ops, g_, bonus = ops[:7], ops[7], ops[8]
        t_p = rwkv_phase1(*hm_ops, 0, M_PROMPT, RWKV_CHUNK, 16)
        t_s = rwkv_phase1(*hm_ops, M_PROMPT, M_SAMPLE, DEC_SEQ, 32)
        y_p, s_p = rwkv_phase2(zeros_s, *t_p, BATCH, SEQ, RWKV_CHUNK)
        y_s, s_stack = rwkv_phase2(state_rwkv_S, *t_s, DEC_BATCH, DEC_SEQ, DEC_SEQ, layer=l, s_stack=s_stack)
        y_rwkv = rwkv_out(jnp.concatenate([y_p, y_s], axis=1), g_, bonus, rwkv_ln_w[l], rwkv_ln_b[l], bd)

        gb_row = jnp.pad(mlstm_gate_b[l].reshape(1, -1), ((0, 0), (0, LANES - 2 * MLSTM_HEADS)))
        ym_p, c_p, n_p, m_p = mlstm(u_ml, u_mg, gb_row, mlstm_norm[l], zeros_c, zeros_n, zeros_m, 0, BATCH, SEQ, MLSTM_CHUNK)
        m0_s = jnp.broadcast_to(_pad8(state_mlstm_m[l], 1)[:, :, None], (DEC_BATCH, 8, LANES))
        ym_s, c_stack, n_s, m_s = mlstm(u_ml, u_mg, gb_row, mlstm_norm[l], state_mlstm_C, _pad8(state_mlstm_n[l], 1), m0_s,
                                        M_PROMPT, DEC_BATCH, DEC_SEQ, DEC_SEQ, layer=l, c_stack=c_stack)
        y_mlstm = jnp.concatenate([ym_p, ym_s], axis=0).astype(bf16)

        merged = merge_branches(y_mla, y_rwkv, y_mlstm, w_br_mla, w_br_rwkv, w_br_mlstm, gates, l)
        attn_out = matmul(merged, w_out, layer=l)
        x, hf = resnorm(x, attn_out, norm_gains[l, 1], norm_gains[l, 2], False)
        ff = matmul_ksplit(matmul(hf, w_up, out_dtype=bf16, act="relu2", layer=l), w_down, layer=l)
        g_next = norm_gains[l + 1, 0] if l + 1 < DEPTH else norm_gains[l, 3]
        x, h, h_lo = resnorm(x, ff, norm_gains[l, 3], g_next, True)

        shift_p = u_rw[SEQ - 1:M_PROMPT:SEQ, :RWKV_COLS]
        shift_s = u_rw[M_PROMPT + DEC_SEQ - 1::DEC_SEQ, :RWKV_COLS]
        prompt_states.append((ckv_p.reshape(BATCH, SEQ, KV_RANK), kpef_p.reshape(BATCH, SEQ, MLA_ROPE), shift_p, s_p,
                              c_p, n_p[:, :MLSTM_HEADS], m_p[:, :MLSTM_HEADS, 0]))
        sample_states.append((ckv_s.reshape(DEC_BATCH, DEC_SEQ, KV_RANK), kpef_s.reshape(DEC_BATCH, DEC_SEQ, MLA_ROPE),
                              shift_s, None, None, n_s[:, :MLSTM_HEADS], m_s[:, :MLSTM_HEADS, 0]))

    outs = [x[:M_PROMPT].reshape(BATCH, SEQ, d), x[M_PROMPT:].reshape(DEC_BATCH, DEC_SEQ, d)]
    for states, stacked in ((prompt_states, {}), (sample_states, {3: s_stack, 4: c_stack})):
        for i in range(7):
            outs.append(stacked[i] if i in stacked else jnp.stack([st[i] for st in states]))
    return tuple(outs)
```

```python
import functools

import jax
import jax.numpy as jnp
import numpy as np
from jax import lax
from jax.experimental import pallas as pl
from jax.experimental.pallas import tpu as pltpu

D_MODEL = 2048
BATCH = 2
SEQ = 4096
DEPTH = 2
DEC_BATCH = 128
DEC_SEQ = 8
PAST_LEN = 16384
PAGE_SIZE = 128

MLA_HEADS = 8
MLA_NOPE = 128
MLA_ROPE = 64
MLA_V = 128
KV_RANK = 256
ROPE_THETA = 10000.0
MLA_SCALE = (MLA_NOPE + MLA_ROPE) ** -0.5
RWKV_HEAD = 64
RWKV_WIDTH = 1024
RWKV_HEADS = RWKV_WIDTH // RWKV_HEAD
W_LORA = 64
A_LORA = 64
G_LORA = 128
RWKV_GN_EPS = 64e-5
MLSTM_HEADS = 4
MLSTM_DH = 256
MLSTM_WIDTH = MLSTM_HEADS * MLSTM_DH
MLSTM_CHUNK = 64
N_BRANCH = 3
D_FF = 4 * D_MODEL
NORM_EPS = 1e-6

MLA_Q_COLS = MLA_HEADS * (MLA_NOPE + MLA_ROPE)
MLA_KV_COLS = KV_RANK + MLA_ROPE
RWKV_COLS = 3 * RWKV_WIDTH + W_LORA + A_LORA + G_LORA
MLSTM_COLS = 4 * MLSTM_WIDTH + 2 * MLSTM_HEADS
GATE_COLS = N_BRANCH * D_MODEL

LANES = 128
MXU_DIM = 256
ROPE_HALF = MLA_ROPE // 2
QL_COLS = MLA_HEADS * KV_RANK
QP_COLS = MLA_HEADS * LANES
UQ_COLS = MLA_HEADS * MLA_NOPE + QP_COLS
UKV_COLS = KV_RANK + LANES
N_PAGES = PAST_LEN // PAGE_SIZE
PAGES_PER_STEP = 16
SEQS_PER_STEP = 4
RWKV_CHUNK = 64
RW_PAD = -(-RWKV_COLS // 896) * 896
M_PROMPT = BATCH * SEQ
M_SAMPLE = DEC_BATCH * DEC_SEQ
M_TOK = M_PROMPT + M_SAMPLE
VMEM_LIMIT = 56 * 1024 * 1024

f32 = jnp.float32
bf16 = jnp.bfloat16


def _cparams(*sem):
    return pltpu.CompilerParams(dimension_semantics=sem, vmem_limit_bytes=VMEM_LIMIT)


def _pick(n, cands):
    for c in cands:
        if n % c == 0:
            return c
    raise ValueError(f"no tile for {n} in {cands}")


def _dot(a, b):
    return jnp.dot(a, b, preferred_element_type=f32)


def _dot_nt(a, b):
    return lax.dot_general(a, b, (((1,), (1,)), ((), ())), preferred_element_type=f32)


def _dot_tn(a, b):
    return lax.dot_general(a, b, (((0,), (0,)), ((), ())), preferred_element_type=f32)


def _split3(x):
    hi = x.astype(bf16)
    r1 = x - hi.astype(f32)
    mid = r1.astype(bf16)
    lo = (r1 - mid.astype(f32)).astype(bf16)
    return hi, mid, lo


def _dot_exact_rhs(sel, x):
    hi, mid, lo = _split3(x)
    return _dot(sel, hi) + _dot(sel, mid) + _dot(sel, lo)


def _dot_exact_lhs(x, sel):
    hi, mid, lo = _split3(x)
    return _dot(hi, sel) + _dot(mid, sel) + _dot(lo, sel)


def _rms(x, g):
    return x * lax.rsqrt(jnp.mean(x * x, axis=-1, keepdims=True) + NORM_EPS) * g


def _emit_norm(hn, h_ref, lo_ref):
    hb = hn.astype(bf16)
    h_ref[...] = hb
    if lo_ref is not None:
        lo_ref[...] = (hn - hb.astype(f32)).astype(bf16)


def _norm_kernel(x_ref, g_ref, h_ref, lo_ref):
    _emit_norm(_rms(x_ref[...], g_ref[...]), h_ref, lo_ref)


def rmsnorm_bf16(x, g):
    m, d = x.shape
    tm = _pick(m, (512, 256))
    row = pl.BlockSpec((tm, d), lambda i: (i, 0))
    return pl.pallas_call(
        _norm_kernel,
        grid=(m // tm,),
        in_specs=[row, pl.BlockSpec((1, d), lambda i: (0, 0))],
        out_specs=[row, row],
        out_shape=[jax.ShapeDtypeStruct((m, d), bf16)] * 2,
        compiler_params=_cparams("parallel"),
        name="rmsnorm",
    )(x, g.reshape(1, d))


def _resnorm_kernel(x_ref, y_ref, gp_ref, gn_ref, xo_ref, h_ref, *lo_ref):
    xn = x_ref[...] + _rms(y_ref[...], gp_ref[...])
    xo_ref[...] = xn
    _emit_norm(_rms(xn, gn_ref[...]), h_ref, lo_ref[0] if lo_ref else None)


def resnorm(x, y, g_post, g_next, with_lo):
    m, d = x.shape
    tm = _pick(m, (512, 256))
    row = pl.BlockSpec((tm, d), lambda i: (i, 0))
    gain = pl.BlockSpec((1, d), lambda i: (0, 0))
    n_h = 2 if with_lo else 1
    return pl.pallas_call(
        _resnorm_kernel,
        grid=(m // tm,),
        in_specs=[row, row, gain, gain],
        out_specs=[row] * (1 + n_h),
        out_shape=[jax.ShapeDtypeStruct((m, d), f32)] + [jax.ShapeDtypeStruct((m, d), bf16)] * n_h,
        compiler_params=_cparams("parallel"),
        name="resnorm",
    )(x, y, g_post.reshape(1, d), g_next.reshape(1, d))


def _weights(b_ref):
    b = b_ref[...]
    return b if b.dtype == bf16 else b.astype(bf16)


def _weight_spec(layer, shape, index_map):
    if layer is None:
        return pl.BlockSpec(shape, index_map)
    return pl.BlockSpec((None,) + shape, lambda *g: (layer,) + index_map(*g))


def _mm(a_ref, b_ref, b_transposed):
    b = _weights(b_ref)
    if b_transposed:
        return _dot_nt(a_ref[...], b.reshape(b.shape[-2:]))
    return _dot(a_ref[...], b)


def _mm_kernel(a_ref, b_ref, o_ref, *, act, b_transposed):
    acc = _mm(a_ref, b_ref, b_transposed)
    if act == "relu2":
        acc = jnp.square(jnp.maximum(acc, 0.0))
    o_ref[...] = acc.astype(o_ref.dtype)


def _mm_bias_sigmoid_kernel(a_ref, b_ref, bias_ref, o_ref, *, b_transposed):
    acc = _mm(a_ref, b_ref, b_transposed) + bias_ref[...]
    o_ref[...] = jax.nn.sigmoid(acc).astype(o_ref.dtype)


def _mm_tiles(m, n):
    return _pick(m, (1024, 512, 256)), _pick(n, (1024, 896, 512, 384, 256, 128))


def matmul(a, b, out_dtype=f32, act=None, bias=None, layer=None, row0=None, n=None):
    m, k = a.shape
    b_transposed = row0 is not None
    if b_transposed:
        assert row0 % 8 == 0 and b.shape[-1] == k
    else:
        n = b.shape[-1]
    tm, tn = _mm_tiles(m, n)
    if not b_transposed:
        b_spec = _weight_spec(layer, (k, tn), lambda i, j: (0, j))
    elif layer is None:
        assert row0 % tn == 0
        b_spec = pl.BlockSpec((tn, k), lambda i, j: (row0 // tn + j, 0))
    else:
        b_spec = pl.BlockSpec((pl.Element(1), pl.Element(tn), pl.Element(k)),
                              lambda i, j: (layer, 8 * (row0 // 8 + j * (tn // 8)), 0))
    in_specs = [pl.BlockSpec((tm, k), lambda i, j: (i, 0)), b_spec]
    args = [a, b]
    if bias is not None:
        kern = functools.partial(_mm_bias_sigmoid_kernel, b_transposed=b_transposed)
        in_specs.append(pl.BlockSpec((1, tn), lambda i, j: (0, j)))
        args.append(bias.reshape(1, n))
    else:
        kern = functools.partial(_mm_kernel, act=act, b_transposed=b_transposed)
    return pl.pallas_call(
        kern,
        grid=(m // tm, n // tn),
        in_specs=in_specs,
        out_specs=pl.BlockSpec((tm, tn), lambda i, j: (i, j)),
        out_shape=jax.ShapeDtypeStruct((m, n), out_dtype),
        compiler_params=_cparams("parallel", "parallel"),
        name="matmul",
    )(*args)


def _mm3_kernel(a_ref, alo_ref, b_ref, blo_ref, o_ref):
    a = a_ref[...]
    b = b_ref[...]
    o_ref[...] = _dot(a, b) + _dot(alo_ref[...], b) + _dot(a, blo_ref[...])


def matmul_split(a, a_lo, b):
    m, k = a.shape
    n = b.shape[1]
    b_hi = b.astype(bf16)
    b_lo = (b - b_hi.astype(f32)).astype(bf16)
    tm = _pick(m, (1024, 512, 256))
    row = pl.BlockSpec((tm, k), lambda i: (i, 0))
    col = pl.BlockSpec((k, n), lambda i: (0, 0))
    return pl.pallas_call(
        _mm3_kernel,
        grid=(m // tm,),
        in_specs=[row, row, col, col],
        out_specs=pl.BlockSpec((tm, n), lambda i: (i, 0)),
        out_shape=jax.ShapeDtypeStruct((m, n), f32),
        compiler_params=_cparams("parallel"),
        name="matmul_split",
    )(a, a_lo, b_hi, b_lo)


def _mm_acc_kernel(a_ref, b_ref, o_ref, acc_ref):
    kk = pl.program_id(2)

    @pl.when(kk == 0)
    def _():
        acc_ref[...] = jnp.zeros_like(acc_ref)

    acc_ref[...] += _dot(a_ref[...], _weights(b_ref))

    @pl.when(kk == pl.num_programs(2) - 1)
    def _():
        o_ref[...] = acc_ref[...]


def matmul_ksplit(a, b, layer=None):
    m, k = a.shape
    n = b.shape[-1]
    tm = _pick(m, (1024, 512, 256))
    tn = _pick(n, (1024, 512, 256))
    tk = _pick(k, (2048, 1024, 512, 256))
    return pl.pallas_call(
        _mm_acc_kernel,
        grid=(m // tm, n // tn, k // tk),
        in_specs=[pl.BlockSpec((tm, tk), lambda i, j, kk: (i, kk)),
                  _weight_spec(layer, (tk, tn), lambda i, j, kk: (kk, j))],
        out_specs=pl.BlockSpec((tm, tn), lambda i, j, kk: (i, j)),
        out_shape=jax.ShapeDtypeStruct((m, n), f32),
        scratch_shapes=[pltpu.VMEM((tm, tn), f32)],
        compiler_params=_cparams("parallel", "parallel", "arbitrary"),
        name="matmul_ksplit",
    )(a, b)


def _merge_kernel(ya_ref, yr_ref, ym_ref, wa_ref, wr_ref, wm_ref, ga_ref, gr_ref, gm_ref, o_ref):
    acc = ga_ref[...].astype(f32) * _dot(ya_ref[...], _weights(wa_ref))
    acc += gr_ref[...].astype(f32) * _dot(yr_ref[...], _weights(wr_ref))
    acc += gm_ref[...].astype(f32) * _dot(ym_ref[...], _weights(wm_ref))
    o_ref[...] = acc.astype(o_ref.dtype)


def merge_branches(y_mla, y_rwkv, y_mlstm, w_mla, w_rwkv, w_mlstm, gates, layer):
    m = y_mla.shape[0]
    d = w_mla.shape[-1]
    tm = _pick(m, (1024, 512, 256))
    tn = _pick(d, (512, 256))
    nj = d // tn
    ys = [pl.BlockSpec((tm, y.shape[1]), lambda i, j: (i, 0)) for y in (y_mla, y_rwkv, y_mlstm)]
    ws = [_weight_spec(layer, (w.shape[1], tn), lambda i, j: (0, j)) for w in (w_mla, w_rwkv, w_mlstm)]
    gs = [pl.BlockSpec((tm, tn), functools.partial(lambda i, j, b: (i, j + b * nj), b=b)) for b in range(N_BRANCH)]
    return pl.pallas_call(
        _merge_kernel,
        grid=(m // tm, nj),
        in_specs=ys + ws + gs,
        out_specs=pl.BlockSpec((tm, tn), lambda i, j: (i, j)),
        out_shape=jax.ShapeDtypeStruct((m, d), bf16),
        compiler_params=_cparams("parallel", "parallel"),
        name="merge_branches",
    )(y_mla, y_rwkv, y_mlstm, w_mla, w_rwkv, w_mlstm, gates, gates, gates)


def _rope_group(p, c4, s4):
    return p * c4 + pltpu.roll(p, 2 * ROPE_HALF, axis=1) * s4


def _mla_prep_kernel(uq_ref, ukv_ref, c4_ref, s4_ref, wuk_ref, g_ref, ql_ref, qp_ref, kc_ref, kpe_ref, ckv_ref, kpef_ref):
    c4 = c4_ref[...]
    s4 = s4_ref[...]
    nope = MLA_HEADS * MLA_NOPE
    for h in range(MLA_HEADS):
        qn = uq_ref[:, h * MLA_NOPE:(h + 1) * MLA_NOPE].astype(bf16)
        ql = _dot(qn, wuk_ref[h]) * MLA_SCALE
        ql_ref[:, h * KV_RANK:(h + 1) * KV_RANK] = ql.astype(ql_ref.dtype)
        qp = _rope_group(uq_ref[:, nope + h * LANES:nope + (h + 1) * LANES], c4, s4) * MLA_SCALE
        qp_ref[:, h * LANES:(h + 1) * LANES] = qp.astype(qp_ref.dtype)
    ckv = _rms(ukv_ref[:, :KV_RANK], g_ref[...])
    ckv_ref[...] = ckv
    kc_ref[...] = ckv.astype(bf16)
    kpe = _rope_group(ukv_ref[:, KV_RANK:], c4, s4)
    kpe_ref[...] = kpe.astype(bf16)
    kpef_ref[...] = kpe[:, :MLA_ROPE]


def mla_prep(u_q, u_kv, c4, s4, w_uk_t, g_ckv, row0, rows, q_dtype):
    tb = _pick(rows, (256, 128))
    assert row0 % tb == 0
    off = row0 // tb

    def rowspec(cols):
        return pl.BlockSpec((tb, cols), lambda i: (i + off, 0))

    def outspec(cols):
        return pl.BlockSpec((tb, cols), lambda i: (i, 0))

    return pl.pallas_call(
        _mla_prep_kernel,
        grid=(rows // tb,),
        in_specs=[rowspec(UQ_COLS), rowspec(UKV_COLS), rowspec(LANES), rowspec(LANES),
                  pl.BlockSpec((MLA_HEADS, MLA_NOPE, KV_RANK), lambda i: (0, 0, 0)),
                  pl.BlockSpec((1, KV_RANK), lambda i: (0, 0))],
        out_specs=[outspec(QL_COLS), outspec(QP_COLS), outspec(KV_RANK), outspec(LANES), outspec(KV_RANK), outspec(MLA_ROPE)],
        out_shape=[jax.ShapeDtypeStruct((rows, QL_COLS), q_dtype), jax.ShapeDtypeStruct((rows, QP_COLS), q_dtype),
                   jax.ShapeDtypeStruct((rows, KV_RANK), bf16), jax.ShapeDtypeStruct((rows, LANES), bf16),
                   jax.ShapeDtypeStruct((rows, KV_RANK), f32), jax.ShapeDtypeStruct((rows, MLA_ROPE), f32)],
        compiler_params=_cparams("parallel"),
        name="mla_prep",
    )(u_q, u_kv, c4, s4, w_uk_t, g_ckv.reshape(1, KV_RANK))


def _attn_prompt_kernel(ql_ref, qp_ref, kc_ref, kpe_ref, wuv_ref, o_ref, m_sc, l_sc, acc_sc, *, tq, tk):
    qi = pl.program_id(1)
    ki = pl.program_id(2)

    @pl.when(ki == 0)
    def _():
        m_sc[...] = jnp.full_like(m_sc, -jnp.inf)
        l_sc[...] = jnp.zeros_like(l_sc)
        acc_sc[...] = jnp.zeros_like(acc_sc)

    @pl.when(ki <= qi)
    def _():
        kc = kc_ref[...]
        kpe = kpe_ref[...]
        rel = lax.broadcasted_iota(jnp.int32, (tq, tk), 1) - lax.broadcasted_iota(jnp.int32, (tq, tk), 0)
        visible = rel <= (qi * tq - ki * tk)
        for h in range(MLA_HEADS):
            s = _dot_nt(ql_ref[:, h * KV_RANK:(h + 1) * KV_RANK], kc)
            s += _dot_nt(qp_ref[:, h * LANES:(h + 1) * LANES], kpe)
            s = jnp.where(visible, s, -jnp.inf)
            m_prev = m_sc[h]
            m_new = jnp.maximum(m_prev, jnp.max(s, axis=1, keepdims=True))
            alpha = jnp.exp(m_prev - m_new)
            p = jnp.exp(s - m_new[:, :1])
            l_sc[h] = alpha * l_sc[h] + jnp.sum(p, axis=1, keepdims=True)
            acc_sc[h] = acc_sc[h] * alpha[:, :1] + _dot(p.astype(bf16), kc)
            m_sc[h] = m_new

    @pl.when(ki == qi)
    def _():
        for h in range(MLA_HEADS):
            o = acc_sc[h] / l_sc[h][:, :1]
            o_ref[:, h * MLA_V:(h + 1) * MLA_V] = _dot(o.astype(bf16), wuv_ref[h]).astype(o_ref.dtype)


def attn_prompt(ql, qp, kc, kpe, w_uv_h, tq):
    tk = tq
    nq = SEQ // tq

    def qmap(b, qi, ki):
        return (b * nq + qi, 0)

    def kmap(b, qi, ki):
        return (b * nq + jnp.minimum(ki, qi), 0)

    return pl.pallas_call(
        functools.partial(_attn_prompt_kernel, tq=tq, tk=tk),
        grid=(BATCH, nq, nq),
        in_specs=[pl.BlockSpec((tq, QL_COLS), qmap), pl.BlockSpec((tq, QP_COLS), qmap),
                  pl.BlockSpec((tk, KV_RANK), kmap), pl.BlockSpec((tk, LANES), kmap),
                  pl.BlockSpec((MLA_HEADS, KV_RANK, MLA_V), lambda b, qi, ki: (0, 0, 0))],
        out_specs=pl.BlockSpec((tq, MLA_HEADS * MLA_V), qmap),
        out_shape=jax.ShapeDtypeStruct((M_PROMPT, MLA_HEADS * MLA_V), bf16),
        scratch_shapes=[pltpu.VMEM((MLA_HEADS, tq, LANES), f32), pltpu.VMEM((MLA_HEADS, tq, LANES), f32),
                        pltpu.VMEM((MLA_HEADS, tq, KV_RANK), f32)],
        compiler_params=_cparams("parallel", "parallel", "arbitrary"),
        name="attn_prompt",
    )(ql, qp, kc, kpe, w_uv_h)


def _attn_sample_kernel(pt_ref, ql_ref, qp_ref, cn_ref, pn_ref, wuv_ref, ckv_hbm, kpe_hbm, o_ref,
                        q_sc, qp_sc, m_sc, l_sc, acc_sc, ckv_buf, kpe_buf, sem_c, sem_p, *, layer):
    pps = PAGES_PER_STEP
    nsq = SEQS_PER_STEP
    n_pg = nsq * pps
    b = pl.program_id(0)
    j = pl.program_id(1)
    nchunk = pl.num_programs(1)
    rows = MLA_HEADS * DEC_SEQ
    seqs = range(nsq)

    def page_copies(slot, idx, page):
        return (pltpu.make_async_copy(ckv_hbm.at[layer, page], ckv_buf.at[slot, idx], sem_c.at[slot]),
                pltpu.make_async_copy(kpe_hbm.at[layer, page], kpe_buf.at[slot, idx], sem_p.at[slot]))

    def start_step(bb, jj, slot):
        for r in seqs:
            for i in range(pps):
                for cp in page_copies(slot, r * pps + i, pt_ref[bb * nsq + r, jj * pps + i]):
                    cp.start()

    t = b * nchunk + j
    slot = t % 2

    @pl.when(t == 0)
    def _():
        start_step(b, j, slot)

    @pl.when(t + 1 < pl.num_programs(0) * nchunk)
    def _():
        last_chunk = j == nchunk - 1
        start_step(jnp.where(last_chunk, b + 1, b), jnp.where(last_chunk, 0, j + 1), 1 - slot)

    for idx in range(n_pg):
        for cp in page_copies(slot, idx, 0):
            cp.wait()
    ckv_pages = [ckv_buf.at[slot, idx] for idx in range(n_pg)]
    kpe_pages = [kpe_buf.at[slot, idx] for idx in range(n_pg)]

    @pl.when(j == 0)
    def _():
        for r in seqs:
            tok = slice(r * DEC_SEQ, (r + 1) * DEC_SEQ)
            for h in range(MLA_HEADS):
                q_sc[r, h * DEC_SEQ:(h + 1) * DEC_SEQ, :] = ql_ref[tok, h * KV_RANK:(h + 1) * KV_RANK]
                qp_sc[r, h * DEC_SEQ:(h + 1) * DEC_SEQ, :] = qp_ref[tok, h * LANES:(h + 1) * LANES]
        m_sc[...] = jnp.full_like(m_sc, -jnp.inf)
        l_sc[...] = jnp.zeros_like(l_sc)
        acc_sc[...] = jnp.zeros_like(acc_sc)

    qb = [q_sc[r].astype(bf16) for r in seqs]
    qpb = [qp_sc[r][:, :MLA_ROPE].astype(bf16) for r in seqs]

    def online(r, s, vals):
        m_prev = m_sc[r]
        m_new = jnp.maximum(m_prev, jnp.max(s, axis=1, keepdims=True))
        alpha = jnp.exp(m_prev - m_new)
        p = jnp.exp(s - m_new[:, :1])
        l_sc[r] = alpha * l_sc[r] + jnp.sum(p, axis=1, keepdims=True)
        acc_sc[r] = acc_sc[r] * alpha[:, :1] + _dot(p.astype(bf16), vals)
        m_sc[r] = m_new

    kcs, scores = [], []
    for r in seqs:
        kcs.append(jnp.concatenate([ckv_pages[r * pps + i][...].astype(bf16) for i in range(pps)], axis=0))
        kp = jnp.concatenate([kpe_pages[r * pps + i][...].astype(bf16) for i in range(pps)], axis=1)
        scores.append(_dot_nt(qb[r], kcs[r]) + _dot(qpb[r], kp))
    for r in seqs:
        online(r, scores[r], kcs[r])

    @pl.when(j == pl.num_programs(1) - 1)
    def _():
        pad = jnp.zeros((LANES - DEC_SEQ, KV_RANK), f32)
        key = lax.broadcasted_iota(jnp.int32, (rows, LANES), 1)
        tokid = lax.broadcasted_iota(jnp.int32, (rows, LANES), 0) % DEC_SEQ
        for r in seqs:
            tok = slice(r * DEC_SEQ, (r + 1) * DEC_SEQ)
            cn = jnp.concatenate([cn_ref[tok, :], pad], axis=0).astype(bf16)
            pn = jnp.concatenate([pn_ref[tok, :], pad[:, :MLA_ROPE]], axis=0).astype(bf16)
            s = _dot_nt(qb[r], cn) + _dot_nt(qpb[r], pn)
            online(r, jnp.where(key <= tokid, s, -jnp.inf), cn)
            o = acc_sc[r] / l_sc[r][:, :1]
            for h in range(MLA_HEADS):
                oh = o[h * DEC_SEQ:(h + 1) * DEC_SEQ, :].astype(bf16)
                o_ref[tok, h * MLA_V:(h + 1) * MLA_V] = _dot(oh, wuv_ref[h])


def attn_sample(ql, qp, c_new, p_new, w_uv_h, cache_ckv, cache_kpe_t, page_table, layer):
    pps = PAGES_PER_STEP
    nsq = SEQS_PER_STEP
    assert N_PAGES % pps == 0 and DEC_BATCH % nsq == 0
    nchunk = N_PAGES // pps
    rows = MLA_HEADS * DEC_SEQ

    def seqspec(cols):
        return pl.BlockSpec((nsq * DEC_SEQ, cols), lambda b, j, pt: (b, 0))

    n_pg = nsq * pps
    in_specs = [seqspec(QL_COLS), seqspec(QP_COLS), seqspec(KV_RANK), seqspec(MLA_ROPE),
                pl.BlockSpec((MLA_HEADS, KV_RANK, MLA_V), lambda b, j, pt: (0, 0, 0)),
                pl.BlockSpec(memory_space=pl.ANY), pl.BlockSpec(memory_space=pl.ANY)]
    grid_spec = pltpu.PrefetchScalarGridSpec(
        num_scalar_prefetch=1,
        grid=(DEC_BATCH // nsq, nchunk),
        in_specs=in_specs,
        out_specs=pl.BlockSpec((nsq * DEC_SEQ, MLA_HEADS * MLA_V), lambda b, j, pt: (b, 0)),
        scratch_shapes=[pltpu.VMEM((nsq, rows, KV_RANK), f32), pltpu.VMEM((nsq, rows, LANES), f32),
                        pltpu.VMEM((nsq, rows, LANES), f32), pltpu.VMEM((nsq, rows, LANES), f32),
                        pltpu.VMEM((nsq, rows, KV_RANK), f32),
                        pltpu.VMEM((2, n_pg, PAGE_SIZE, KV_RANK), f32), pltpu.VMEM((2, n_pg, MLA_ROPE, PAGE_SIZE), f32),
                        pltpu.SemaphoreType.DMA((2,)), pltpu.SemaphoreType.DMA((2,))],
    )
    return pl.pallas_call(
        functools.partial(_attn_sample_kernel, layer=layer),
        grid_spec=grid_spec,
        out_shape=jax.ShapeDtypeStruct((M_SAMPLE, MLA_HEADS * MLA_V), f32),
        compiler_params=_cparams("arbitrary", "arbitrary"),
        name="attn_sample",
    )(page_table, ql, qp, c_new, p_new, w_uv_h, cache_ckv, cache_kpe_t)


def _head_sums(x, bd):
    hi = x.astype(bf16)
    lo = (x - hi.astype(f32)).astype(bf16)
    parts = []
    for g in range(x.shape[1] // MXU_DIM):
        sl = slice(g * MXU_DIM, (g + 1) * MXU_DIM)
        parts.append(_dot(hi[:, sl], bd) + _dot(lo[:, sl], bd))
    return jnp.concatenate(parts, axis=1)


def _rwkv_prep_kernel(u_ref, st_ref, mu_ref, w0_ref, a0_ref, kkw_ref, ka_ref, rk_ref, wwa_ref, g2_ref, bd_ref,
                      r_ref, k_ref, v_ref, p_ref, q_ref, w_ref, gc_ref, g_ref, bonus_ref, *, tb, n_prompt_blocks):
    w_ = RWKV_WIDTH
    nst = st_ref.shape[0]
    is_sample = pl.program_id(0) >= n_prompt_blocks
    stride = jnp.where(is_sample, DEC_SEQ, tb)
    row_e = lax.broadcasted_iota(jnp.int32, (tb, nst), 0)
    col_e = lax.broadcasted_iota(jnp.int32, (tb, nst), 1)
    place = jnp.where(row_e == col_e * stride, 1.0, 0.0).astype(bf16)
    starts = _dot_exact_rhs(place, st_ref[...])
    u = u_ref[...]
    row = lax.broadcasted_iota(jnp.int32, (tb, 1), 0)
    is_start = jnp.where(is_sample, row % DEC_SEQ, row) == 0
    prev = jnp.where(is_start, starts, pltpu.roll(u, 1, axis=0))
    z = u + (prev - u) * mu_ref[...]
    r = z[:, :w_]
    k = z[:, w_:2 * w_]
    v = z[:, 2 * w_:3 * w_]
    wa = z[:, 3 * w_:3 * w_ + W_LORA + A_LORA]
    gd = z[:, 3 * w_ + W_LORA + A_LORA:]
    lane = lax.broadcasted_iota(jnp.int32, wa.shape, 1)
    wa = jnp.where(lane < W_LORA, jnp.tanh(wa), wa)
    lora = _dot(wa.astype(bf16), wwa_ref[...])
    w_log = -jax.nn.softplus(-(w0_ref[...] + lora[:, :w_])) - 0.5
    a = jax.nn.sigmoid(a0_ref[...] + lora[:, w_:])
    g = _dot(jax.nn.sigmoid(gd).astype(bf16), g2_ref[...])
    bd = bd_ref[...]
    kk = k * kkw_ref[...]
    kk = kk / jnp.maximum(jnp.sqrt(_head_sums(kk * kk, bd)), 1e-12)
    k = k * (1.0 + (a - 1.0) * ka_ref[...])
    bonus = _head_sums(r * k * rk_ref[...], bd) * v
    g_ref[...] = g
    bonus_ref[...] = bonus
    wdec = -jnp.exp(w_log)
    shift = jnp.where(is_sample, DEC_SEQ.bit_length() - 1, RWKV_CHUNK.bit_length() - 1)
    row_c = lax.broadcasted_iota(jnp.int32, (tb, tb), 0)
    col_c = lax.broadcasted_iota(jnp.int32, (tb, tb), 1)
    same_chunk = lax.shift_right_logical(row_c, shift) == lax.shift_right_logical(col_c, shift)
    tril_blk = jnp.where(same_chunk, jnp.where(col_c <= row_c, 1.0, 0.0), 0.0).astype(bf16)
    gcum = _dot_exact_rhs(tril_blk, wdec)
    for h in range(RWKV_HEADS):
        sl = slice(h * RWKV_HEAD, (h + 1) * RWKV_HEAD)
        r_ref[h] = r[:, sl]
        k_ref[h] = k[:, sl]
        v_ref[h] = v[:, sl]
        p_ref[h] = -kk[:, sl]
        q_ref[h] = (kk * a)[:, sl]
        w_ref[h] = wdec[:, sl]
        gc_ref[h] = gcum[:, sl]


def rwkv_prep(u, starts, lp_rows, w_wa, g2, bd, tb):
    m = u.shape[0]
    nst = starts.shape[1]
    rowc = pl.BlockSpec((tb, RWKV_COLS), lambda i: (i, 0))

    def vec(n):
        return pl.BlockSpec((1, n), lambda i: (0, 0))

    def full(a):
        return pl.BlockSpec(a.shape, lambda i: (0, 0))

    headmajor = pl.BlockSpec((RWKV_HEADS, tb, RWKV_HEAD), lambda i: (0, i, 0))
    tokmajor = pl.BlockSpec((tb, RWKV_WIDTH), lambda i: (i, 0))
    hm_shape = jax.ShapeDtypeStruct((RWKV_HEADS, m, RWKV_HEAD), f32)
    tm_shape = jax.ShapeDtypeStruct((m, RWKV_WIDTH), f32)
    return pl.pallas_call(
        functools.partial(_rwkv_prep_kernel, tb=tb, n_prompt_blocks=M_PROMPT // tb),
        grid=(m // tb,),
        in_specs=[rowc, pl.BlockSpec((None, nst, RWKV_COLS), lambda i: (i, 0, 0)), vec(RWKV_COLS)]
        + [vec(RWKV_WIDTH)] * 5 + [full(w_wa), full(g2), full(bd)],
        out_specs=[headmajor] * 7 + [tokmajor] * 2,
        out_shape=[hm_shape] * 7 + [tm_shape] * 2,
        compiler_params=_cparams("parallel"),
        name="rwkv_prep",
    )(u, starts, *lp_rows, w_wa, g2, bd)


def _tri_consts(csz):
    row = lax.broadcasted_iota(jnp.int32, (csz, csz), 0)
    col = lax.broadcasted_iota(jnp.int32, (csz, csz), 1)
    return (col <= row).astype(f32), (col < row).astype(f32), (col == row).astype(f32)


def _rwkv_phase1_kernel(r_ref, k_ref, v_ref, p_ref, q_ref, w_ref, g_ref, rhat_ref, yint_ref, mlow_ref, nn_ref, e_ref,
                        *, csz, nck):
    tril_incl, tril_strict, eye = _tri_consts(csz)
    cs = range(nck)
    sls = [slice(c * csz, (c + 1) * csz) for c in cs]

    def stack(a, b):
        return jnp.concatenate([a, b], axis=0).astype(bf16)

    pr, qt, kt, p0, r0, qh, kh, vb, e_last = [], [], [], [], [], [], [], [], []
    for sl in sls:
        r, k, v, p, q = r_ref[sl, :], k_ref[sl, :], v_ref[sl, :], p_ref[sl, :], q_ref[sl, :]
        g = g_ref[sl, :]
        gp = g - w_ref[sl, :]
        ref = g[csz // 2 - 1:csz // 2, :]
        gl = g[csz - 1:csz, :]
        e_in = jnp.exp(ref - g)
        e_out = jnp.exp(gl - g)
        pr.append(stack(p * jnp.exp(gp - ref), r * jnp.exp(g - ref)))
        qt.append((q * e_in).astype(bf16))
        kt.append((k * e_in).astype(bf16))
        p0.append((p * jnp.exp(gp)).astype(bf16))
        r0.append(r * jnp.exp(g))
        qh.append((q * e_out).astype(bf16))
        kh.append((k * e_out).astype(bf16))
        vb.append(v.astype(bf16))
        e_last.append(jnp.exp(gl))
    gq = [_dot_nt(pr[c], qt[c]) for c in cs]
    gk = [_dot_nt(pr[c], kt[c]) for c in cs]
    a_qp = [gq[c][:csz] * tril_strict for c in cs]
    a_kp = [(gk[c][:csz] * tril_strict).astype(bf16) for c in cs]
    a_qr = [(gq[c][csz:] * tril_incl).astype(bf16) for c in cs]
    a_kr = [(gk[c][csz:] * tril_incl).astype(bf16) for c in cs]
    tinv = [eye + a_qp[c] for c in cs]
    if csz > 2:
        apb = [a_qp[c].astype(bf16) for c in cs]
        apow = [_dot(apb[c], apb[c]) for c in cs]
        n = 2
        while n < csz:
            last = 2 * n >= csz
            rhs = [apow[c].astype(bf16) for c in cs]
            if last:
                prod = [_dot(tinv[c].astype(bf16), rhs[c]) for c in cs]
                tinv = [tinv[c] + prod[c] for c in cs]
            else:
                prod = [_dot(stack(tinv[c], apow[c]), rhs[c]) for c in cs]
                tinv = [tinv[c] + prod[c][:csz] for c in cs]
                apow = [prod[c][csz:] for c in cs]
            n *= 2
    tb_ = [tinv[c].astype(bf16) for c in cs]
    akpv = [_dot(a_kp[c], vb[c]).astype(bf16) for c in cs]
    phb = [_dot(tb_[c], p0[c]).astype(bf16) for c in cs]
    wvb = [_dot(tb_[c], akpv[c]).astype(bf16) for c in cs]
    rhat = [r0[c] + _dot(a_qr[c], phb[c]) for c in cs]
    yint = [_dot(a_qr[c], wvb[c]) + _dot(a_kr[c], vb[c]) for c in cs]
    mlow = [_dot_tn(phb[c], qh[c]) for c in cs]
    nn = [_dot_tn(wvb[c], qh[c]) + _dot_tn(vb[c], kh[c]) for c in cs]
    for c in cs:
        rhat_ref[sls[c], :] = rhat[c]
        yint_ref[sls[c], :] = yint[c]
        mlow_ref[c] = mlow[c]
        nn_ref[c] = nn[c]
        e_ref[c] = jnp.broadcast_to(e_last[c], (8, RWKV_HEAD))


def rwkv_phase1(r, k, v, p, q, w, g, row0, rows, csz, nck):
    tb = nck * csz
    assert rows % tb == 0 and row0 % tb == 0
    off = row0 // tb
    nchunks = rows // csz
    hm_in = pl.BlockSpec((None, tb, RWKV_HEAD), lambda h, i: (h, i + off, 0))
    hm_out = pl.BlockSpec((None, tb, RWKV_HEAD), lambda h, i: (h, i, 0))
    sq = pl.BlockSpec((None, nck, RWKV_HEAD, RWKV_HEAD), lambda h, i: (h, i, 0, 0))
    ev = pl.BlockSpec((None, nck, 8, RWKV_HEAD), lambda h, i: (h, i, 0, 0))
    return pl.pallas_call(
        functools.partial(_rwkv_phase1_kernel, csz=csz, nck=nck),
        grid=(RWKV_HEADS, rows // tb),
        in_specs=[hm_in] * 7,
        out_specs=[hm_out, hm_out, sq, sq, ev],
        out_shape=[jax.ShapeDtypeStruct((RWKV_HEADS, rows, RWKV_HEAD), f32)] * 2
        + [jax.ShapeDtypeStruct((RWKV_HEADS, nchunks, RWKV_HEAD, RWKV_HEAD), f32)] * 2
        + [jax.ShapeDtypeStruct((RWKV_HEADS, nchunks, 8, RWKV_HEAD), f32)],
        compiler_params=_cparams("parallel", "parallel"),
        name="rwkv_phase1",
    )(r, k, v, p, q, w, g)


def _rwkv_phase2_kernel(*refs, csz, nck, n_alias):
    s0_ref, rhat_ref, yint_ref, mlow_ref, nn_ref, e_ref = refs[:6]
    y_ref, sout_ref, s_sc = refs[6 + n_alias:]
    ci = pl.program_id(1)
    hs = range(RWKV_HEADS)

    @pl.when(ci == 0)
    def _():
        s_sc[...] = s0_ref[...]

    s = [s_sc[h] for h in hs]
    for c in range(nck):
        sl = slice(c * csz, (c + 1) * csz)
        sb = [s[h].astype(bf16) for h in hs]
        ys = [_dot_nt(rhat_ref[h, sl, :].astype(bf16), sb[h]) for h in hs]
        sm = [_dot(sb[h], mlow_ref[h, c].astype(bf16)) for h in hs]
        for h in hs:
            y_ref[h, sl, :] = ys[h] + yint_ref[h, sl, :]
        s = [s[h] * e_ref[h, c][:1, :] + sm[h] + nn_ref[h, c] for h in hs]
    for h in hs:
        s_sc[h] = s[h]

    @pl.when(ci == pl.num_programs(1) - 1)
    def _():
        sout_ref[...] = s_sc[...]


def rwkv_phase2(s0, rhat, yint, mlow, nn, e, nseq, tlen, csz, layer=None, s_stack=None):
    cps = tlen // csz
    nck = min(cps, 4)
    assert cps % nck == 0
    nsteps = cps // nck
    tb = nck * csz
    hm = pl.BlockSpec((RWKV_HEADS, tb, RWKV_HEAD), lambda s, c: (0, s * nsteps + c, 0))
    sq = pl.BlockSpec((RWKV_HEADS, nck, RWKV_HEAD, RWKV_HEAD), lambda s, c: (0, s * nsteps + c, 0, 0))
    ev = pl.BlockSpec((RWKV_HEADS, nck, 8, RWKV_HEAD), lambda s, c: (0, s * nsteps + c, 0, 0))
    s_shape = (RWKV_HEADS, RWKV_HEAD, RWKV_HEAD)
    if layer is None:
        st_in = st_out = pl.BlockSpec((None,) + s_shape, lambda s, c: (s, 0, 0, 0))
        st_out_shape = jax.ShapeDtypeStruct((nseq,) + s_shape, f32)
        extra_specs, extra_args = [], []
    else:
        st_in = pl.BlockSpec((None, None) + s_shape, lambda s, c: (layer, s, 0, 0, 0))
        st_out, st_out_shape, extra_specs, extra_args = _stacked_state_out(layer, s_stack, nseq, s_shape)
    n_in = 6
    return pl.pallas_call(
        functools.partial(_rwkv_phase2_kernel, csz=csz, nck=nck, n_alias=len(extra_args)),
        grid=(nseq, nsteps),
        in_specs=[st_in, hm, hm, sq, sq, ev] + extra_specs,
        out_specs=[hm, st_out],
        out_shape=[jax.ShapeDtypeStruct((RWKV_HEADS, nseq * tlen, RWKV_HEAD), f32), st_out_shape],
        scratch_shapes=[pltpu.VMEM(s_shape, f32)],
        input_output_aliases={n_in: 1} if extra_args else {},
        compiler_params=_cparams("parallel", "arbitrary"),
        name="rwkv_phase2",
    )(s0, rhat, yint, mlow, nn, e, *extra_args)


def _rwkv_out_kernel(y_ref, g_ref, bonus_ref, lnw_ref, lnb_ref, bd_ref, o_ref, y_sc):
    for h in range(RWKV_HEADS):
        y_sc[:, h * RWKV_HEAD:(h + 1) * RWKV_HEAD] = y_ref[h]
    y = y_sc[...]
    bd = bd_ref[...]
    mean = _head_sums(y, bd) * (1.0 / RWKV_HEAD)
    yc = y - mean
    var = _head_sums(yc * yc, bd) * (1.0 / RWKV_HEAD)
    yn = yc * lax.rsqrt(var + RWKV_GN_EPS) * lnw_ref[...] + lnb_ref[...]
    o_ref[...] = ((yn + bonus_ref[...]) * g_ref[...]).astype(o_ref.dtype)


def rwkv_out(y_hm, g, bonus, ln_w, ln_b, bd):
    m = g.shape[0]
    tb = _pick(m, (256, 128))
    tok = pl.BlockSpec((tb, RWKV_WIDTH), lambda i: (i, 0))
    vec = pl.BlockSpec((1, RWKV_WIDTH), lambda i: (0, 0))
    return pl.pallas_call(
        _rwkv_out_kernel,
        grid=(m // tb,),
        in_specs=[pl.BlockSpec((RWKV_HEADS, tb, RWKV_HEAD), lambda i: (0, i, 0)), tok, tok, vec, vec,
                  pl.BlockSpec(bd.shape, lambda i: (0, 0))],
        out_specs=tok,
        out_shape=jax.ShapeDtypeStruct((m, RWKV_WIDTH), bf16),
        scratch_shapes=[pltpu.VMEM((tb, RWKV_WIDTH), f32)],
        compiler_params=_cparams("parallel"),
        name="rwkv_out",
    )(y_hm, g, bonus, ln_w.reshape(1, -1), ln_b.reshape(1, -1), bd)


def _mlstm_kernel(*refs, csz, n_alias):
    u_ref, ug_ref, gb_ref, ng_ref, c0_ref, n0_ref, m0_ref = refs[:7]
    y_ref, cout_ref, nout_ref, mout_ref, c_sc, n_sc, m_sc = refs[7 + n_alias:]
    ci = pl.program_id(1)
    dh = MLSTM_DH
    w_ = MLSTM_WIDTH

    @pl.when(ci == 0)
    def _():
        c_sc[...] = c0_ref[...]
        n_sc[...] = n0_ref[...]
        m_sc[...] = m0_ref[...]

    tril_incl, _, _ = _tri_consts(csz)
    causal = tril_incl > 0.0
    gates = ug_ref[...] + gb_ref[...]
    lane = lax.broadcasted_iota(jnp.int32, gates.shape, 1)
    logf = jax.nn.log_sigmoid(gates)
    fcum = _dot_exact_rhs(tril_incl.astype(bf16), logf)
    cols = jnp.where(lane < MLSTM_HEADS, gates, fcum)
    sel = (lax.broadcasted_iota(jnp.int32, (8, LANES), 0) == lax.broadcasted_iota(jnp.int32, (8, LANES), 1))
    hi, mid, lo = _split3(cols)
    selb = sel.astype(bf16)
    rows = _dot_nt(selb, hi) + _dot_nt(selb, mid) + _dot_nt(selb, lo)
    hs = range(MLSTM_HEADS)
    q = [u_ref[:, h * dh:(h + 1) * dh] for h in hs]
    k = [u_ref[:, w_ + h * dh:w_ + (h + 1) * dh] * (dh ** -0.5) for h in hs]
    v = [u_ref[:, 2 * w_ + h * dh:2 * w_ + (h + 1) * dh] for h in hs]
    qb = [q[h].astype(bf16) for h in hs]
    kb = [k[h].astype(bf16) for h in hs]
    vb = [v[h].astype(bf16) for h in hs]
    c_old = [c_sc[h] for h in hs]
    n_old = [n_sc[h:h + 1, :] for h in hs]
    m_old = [m_sc[h:h + 1, :1] for h in hs]
    ig_col = [cols[:, h:h + 1] for h in hs]
    f_col = [cols[:, MLSTM_HEADS + h:MLSTM_HEADS + h + 1] for h in hs]
    d_ts = [jnp.where(causal, f_col[h] - rows[MLSTM_HEADS + h:MLSTM_HEADS + h + 1, :] + rows[h:h + 1, :], -jnp.inf)
            for h in hs]
    inter = [m_old[h] + f_col[h] for h in hs]
    m_t = [jnp.maximum(inter[h], jnp.max(d_ts[h], axis=1, keepdims=True)) for h in hs]
    w_inter = [jnp.exp(inter[h] - m_t[h]) for h in hs]
    qk = [_dot_nt(qb[h], kb[h]) for h in hs]
    cq = [_dot_nt(qb[h], c_old[h].astype(bf16)) for h in hs]
    a = [jnp.exp(d_ts[h] - m_t[h]) * qk[h] for h in hs]
    av = [_dot(a[h].astype(bf16), vb[h]) for h in hs]
    m_new = [m_t[h][csz - 1:csz, :] for h in hs]
    f_last = [f_col[h][csz - 1:csz, :] for h in hs]
    carry = [jnp.exp(m_old[h] + f_last[h] - m_new[h]) for h in hs]
    w_write = [jnp.exp(f_last[h] - f_col[h] + ig_col[h] - m_new[h]) for h in hs]
    vk = [_dot_tn((v[h] * w_write[h]).astype(bf16), kb[h]) for h in hs]
    for h in hs:
        num = w_inter[h] * cq[h] + av[h]
        den = w_inter[h] * jnp.sum(q[h] * n_old[h], axis=1, keepdims=True) + jnp.sum(a[h], axis=1, keepdims=True)
        hh = num / jnp.maximum(jnp.abs(den), jnp.exp(-m_t[h]))
        hh = hh * lax.rsqrt(jnp.mean(hh * hh, axis=1, keepdims=True) + NORM_EPS)
        og = jax.nn.sigmoid(u_ref[:, 3 * w_ + h * dh:3 * w_ + (h + 1) * dh])
        y_ref[:, h * dh:(h + 1) * dh] = (og * hh * ng_ref[:, h * dh:(h + 1) * dh]).astype(y_ref.dtype)
        c_sc[h] = carry[h] * c_old[h] + vk[h]
        n_sc[h:h + 1, :] = carry[h] * n_old[h] + jnp.sum(k[h] * w_write[h], axis=0, keepdims=True)
        m_sc[h:h + 1, :] = jnp.broadcast_to(m_new[h], (1, LANES))

    @pl.when(ci == pl.num_programs(1) - 1)
    def _():
        cout_ref[...] = c_sc[...]
        nout_ref[...] = n_sc[...]
        mout_ref[...] = m_sc[...]


def _stacked_state_out(layer, stack, nseq, shape):
    spec = pl.BlockSpec((None, None) + shape, lambda s, c: (layer, s) + (0,) * len(shape))
    out_shape = jax.ShapeDtypeStruct((DEPTH, nseq) + shape, f32)
    assert stack.shape == out_shape.shape
    return spec, out_shape, [pl.BlockSpec(memory_space=pl.ANY)], [stack]


def mlstm(u, ug, gate_bias_row, norm_g, c0, n0, m0, row0, nseq, tlen, csz, layer=None, c_stack=None):
    cps = tlen // csz
    assert row0 % csz == 0
    off = row0 // csz

    def rowspec(cols):
        return pl.BlockSpec((csz, cols), lambda s, c: (off + s * cps + c, 0))

    def state(shape):
        return pl.BlockSpec((None,) + shape, lambda s, c: (s,) + (0,) * len(shape))

    c_shape = (MLSTM_HEADS, MLSTM_DH, MLSTM_DH)
    if layer is None:
        c_in, c_out, c_out_shape = state(c_shape), state(c_shape), jax.ShapeDtypeStruct((nseq,) + c_shape, f32)
        extra_specs, extra_args = [], []
    else:
        c_in = pl.BlockSpec((None, None) + c_shape, lambda s, c: (layer, s, 0, 0, 0))
        c_out, c_out_shape, extra_specs, extra_args = _stacked_state_out(layer, c_stack, nseq, c_shape)
    n_in = 7
    return pl.pallas_call(
        functools.partial(_mlstm_kernel, csz=csz, n_alias=len(extra_args)),
        grid=(nseq, cps),
        in_specs=[rowspec(4 * MLSTM_WIDTH), rowspec(LANES), pl.BlockSpec((1, LANES), lambda s, c: (0, 0)),
                  pl.BlockSpec((1, MLSTM_WIDTH), lambda s, c: (0, 0)),
                  c_in, state((8, MLSTM_DH)), state((8, LANES))] + extra_specs,
        out_specs=[pl.BlockSpec((csz, MLSTM_WIDTH), lambda s, c: (s * cps + c, 0)),
                   c_out, state((8, MLSTM_DH)), state((8, LANES))],
        out_shape=[jax.ShapeDtypeStruct((nseq * tlen, MLSTM_WIDTH), f32), c_out_shape,
                   jax.ShapeDtypeStruct((nseq, 8, MLSTM_DH), f32), jax.ShapeDtypeStruct((nseq, 8, LANES), f32)],
        scratch_shapes=[pltpu.VMEM((MLSTM_HEADS, MLSTM_DH, MLSTM_DH), f32), pltpu.VMEM((8, MLSTM_DH), f32),
                        pltpu.VMEM((8, LANES), f32)],
        input_output_aliases={n_in: 1} if extra_args else {},
        compiler_params=_cparams("parallel", "arbitrary"),
        name="mlstm",
    )(u, ug, gate_bias_row, norm_g.reshape(1, -1), c0, n0, m0, *extra_args)


def _column_plan():
    per_head = MLA_NOPE + MLA_ROPE
    q_nope = [h * per_head + d for h in range(MLA_HEADS) for d in range(MLA_NOPE)]
    q_rope = []
    for h in range(MLA_HEADS):
        base = h * per_head + MLA_NOPE
        x1 = [base + e for e in range(ROPE_HALF)]
        x2 = [base + ROPE_HALF + e for e in range(ROPE_HALF)]
        q_rope += x1 + x2 + x2 + x1
    kv_base = MLA_Q_COLS
    ckv = [kv_base + c for c in range(KV_RANK)]
    x1 = [kv_base + KV_RANK + e for e in range(ROPE_HALF)]
    x2 = [kv_base + KV_RANK + ROPE_HALF + e for e in range(ROPE_HALF)]
    return np.asarray(q_nope + q_rope, np.int32), np.asarray(ckv + x1 + x2 + x2 + x1, np.int32)


def _rope_tables():
    pos = jnp.concatenate([jnp.tile(jnp.arange(SEQ), BATCH), jnp.tile(PAST_LEN + jnp.arange(DEC_SEQ), DEC_BATCH)])
    inv = ROPE_THETA ** (-jnp.arange(ROPE_HALF, dtype=f32) / ROPE_HALF)
    ang = pos.astype(f32)[:, None] * inv[None, :]
    cos, sin = jnp.cos(ang), jnp.sin(ang)
    zero = jnp.zeros_like(cos)
    return jnp.concatenate([cos, cos, zero, zero], axis=1), jnp.concatenate([-sin, sin, zero, zero], axis=1)


def _pad8(x, rows_axis):
    pad = [(0, 0)] * x.ndim
    pad[rows_axis] = (0, 8 - x.shape[rows_axis])
    return jnp.pad(x, pad)


def kernel(x_prompt, x_sample, cache_ckv, cache_kpe, page_table, state_rwkv_shift, state_rwkv_S, state_mlstm_C, state_mlstm_n, state_mlstm_m, norm_gains, w_in, g_ckv, w_uk, w_uv, rwkv_mu, rwkv_w0, rwkv_w2, rwkv_a0, rwkv_a2, rwkv_g2, rwkv_kk, rwkv_ka, rwkv_rk, rwkv_ln_w, rwkv_ln_b, mlstm_gate_b, mlstm_norm, gate_b, w_br_mla, w_br_rwkv, w_br_mlstm, w_out, w_up, w_down):
    d = D_MODEL
    o0 = MLA_Q_COLS
    o1 = o0 + MLA_KV_COLS
    o2 = o1 + RWKV_COLS
    o3 = o2 + MLSTM_COLS
    q_cols, kv_cols = _column_plan()
    c4, s4 = _rope_tables()
    head_id = np.arange(MXU_DIM) // RWKV_HEAD
    bd = jnp.asarray(head_id[:, None] == head_id[None, :], bf16)
    zeros_s = jnp.zeros((BATCH, RWKV_HEADS, RWKV_HEAD, RWKV_HEAD), f32)
    zeros_c = jnp.zeros((BATCH, MLSTM_HEADS, MLSTM_DH, MLSTM_DH), f32)
    zeros_n = jnp.zeros((BATCH, 8, MLSTM_DH), f32)
    zeros_m = jnp.zeros((BATCH, 8, LANES), f32)
    tq = _pick(SEQ, (512, 256))
    tb_rw = 256
    n_starts = tb_rw // DEC_SEQ
    assert SEQ % tb_rw == 0 and M_SAMPLE % tb_rw == 0
    cache_kpe_t = jnp.swapaxes(cache_kpe, 2, 3)
    w_in_t = jnp.swapaxes(w_in, 1, 2)
    w_in_rows = w_in_t.reshape(DEPTH * w_in.shape[2], d)

    x = jnp.concatenate([x_prompt.reshape(M_PROMPT, d), x_sample.reshape(M_SAMPLE, d)], axis=0)
    h, h_lo = rmsnorm_bf16(x, norm_gains[0, 0])
    prompt_states, sample_states = [], []
    s_stack = jnp.zeros((DEPTH, DEC_BATCH, RWKV_HEADS, RWKV_HEAD, RWKV_HEAD), f32)
    c_stack = jnp.zeros((DEPTH, DEC_BATCH, MLSTM_HEADS, MLSTM_DH, MLSTM_DH), f32)
    for l in range(DEPTH):
        w_q = jnp.take(w_in_rows, l * w_in.shape[2] + q_cols, axis=0)
        w_kv = jnp.take(w_in_rows, l * w_in.shape[2] + kv_cols, axis=0)
        w_mg = jnp.pad(w_in_t[l, o2 + 4 * MLSTM_WIDTH:o3].T, ((0, 0), (0, LANES - 2 * MLSTM_HEADS)))

        u_q = matmul(h, w_q, row0=0, n=UQ_COLS)
        u_kv = matmul(h, w_kv, row0=0, n=UKV_COLS)
        u_rw = matmul(h, w_in_t, layer=l, row0=o1, n=RW_PAD)
        u_ml = matmul(h, w_in_t, layer=l, row0=o2, n=4 * MLSTM_WIDTH)
        u_mg = matmul_split(h, h_lo, w_mg)
        gates = matmul(h, w_in_t, out_dtype=bf16, bias=gate_b[l].reshape(-1), layer=l, row0=o3, n=GATE_COLS)

        w_uk_t = jnp.transpose(w_uk[l], (1, 2, 0)).astype(bf16)
        w_uv_h = jnp.transpose(w_uv[l], (1, 0, 2)).astype(bf16)
        ql_p, qp_p, kc_p, kpe_p, ckv_p, kpef_p = mla_prep(u_q, u_kv, c4, s4, w_uk_t, g_ckv[l], 0, M_PROMPT, bf16)
        ql_s, qp_s, _, _, ckv_s, kpef_s = mla_prep(u_q, u_kv, c4, s4, w_uk_t, g_ckv[l], M_PROMPT, M_SAMPLE, f32)
        y_mla_p = attn_prompt(ql_p, qp_p, kc_p, kpe_p, w_uv_h, tq)
        y_mla_s = attn_sample(ql_s, qp_s, ckv_s, kpef_s, w_uv_h, cache_ckv, cache_kpe_t, page_table, l)
        y_mla = jnp.concatenate([y_mla_p, y_mla_s.astype(bf16)], axis=0)

        block_last = u_rw[tb_rw - 1:M_PROMPT:tb_rw, :RWKV_COLS].reshape(BATCH, SEQ // tb_rw, 1, RWKV_COLS)
        before = jnp.concatenate([jnp.zeros((BATCH, 1, 1, RWKV_COLS), f32), block_last[:, :-1]], axis=1)
        starts = jnp.concatenate([
            jnp.pad(before.reshape(M_PROMPT // tb_rw, 1, RWKV_COLS), ((0, 0), (0, n_starts - 1), (0, 0))),
            state_rwkv_shift[l].reshape(M_SAMPLE // tb_rw, n_starts, RWKV_COLS)], axis=0)
        zero_blk = jnp.zeros((W_LORA, RWKV_WIDTH), f32)
        w_wa = jnp.concatenate([jnp.concatenate([rwkv_w2[l], zero_blk], axis=1),
                                jnp.concatenate([zero_blk, rwkv_a2[l]], axis=1)], axis=0).astype(bf16)
        lp_rows = [rwkv_mu[l].reshape(1, -1), rwkv_w0[l].reshape(1, -1), rwkv_a0[l].reshape(1, -1),
                   rwkv_kk[l].reshape(1, -1), rwkv_ka[l].reshape(1, -1), rwkv_rk[l].reshape(1, -1)]
        ops = rwkv_prep(u_rw, starts, lp_rows, w_wa, rwkv_g2[l].astype(bf16), bd, tb_rw)
        hm_ops, g_, bonus = ops[:7], ops[7], ops[8]
        t_p = rwkv_phase1(*hm_ops, 0, M_PROMPT, RWKV_CHUNK, 16)
        t_s = rwkv_phase1(*hm_ops, M_PROMPT, M_SAMPLE, DEC_SEQ, 32)
        y_p, s_p = rwkv_phase2(zeros_s, *t_p, BATCH, SEQ, RWKV_CHUNK)
        y_s, s_stack = rwkv_phase2(state_rwkv_S, *t_s, DEC_BATCH, DEC_SEQ, DEC_SEQ, layer=l, s_stack=s_stack)
        y_rwkv = rwkv_out(jnp.concatenate([y_p, y_s], axis=1), g_, bonus, rwkv_ln_w[l], rwkv_ln_b[l], bd)

        gb_row = jnp.pad(mlstm_gate_b[l].reshape(1, -1), ((0, 0), (0, LANES - 2 * MLSTM_HEADS)))
        ym_p, c_p, n_p, m_p = mlstm(u_ml, u_mg, gb_row, mlstm_norm[l], zeros_c, zeros_n, zeros_m, 0, BATCH, SEQ, MLSTM_CHUNK)
        m0_s = jnp.broadcast_to(_pad8(state_mlstm_m[l], 1)[:, :, None], (DEC_BATCH, 8, LANES))
        ym_s, c_stack, n_s, m_s = mlstm(u_ml, u_mg, gb_row, mlstm_norm[l], state_mlstm_C, _pad8(state_mlstm_n[l], 1), m0_s,
                                        M_PROMPT, DEC_BATCH, DEC_SEQ, DEC_SEQ, layer=l, c_stack=c_stack)
        y_mlstm = jnp.concatenate([ym_p, ym_s], axis=0).astype(bf16)

        merged = merge_branches(y_mla, y_rwkv, y_mlstm, w_br_mla, w_br_rwkv, w_br_mlstm, gates, l)
        attn_out = matmul(merged, w_out, layer=l)
        x, hf = resnorm(x, attn_out, norm_gains[l, 1], norm_gains[l, 2], False)
        ff = matmul_ksplit(matmul(hf, w_up, out_dtype=bf16, act="relu2", layer=l), w_down, layer=l)
        g_next = norm_gains[l + 1, 0] if l + 1 < DEPTH else norm_gains[l, 3]
        x, h, h_lo = resnorm(x, ff, norm_gains[l, 3], g_next, True)

        shift_p = u_rw[SEQ - 1:M_PROMPT:SEQ, :RWKV_COLS]
        shift_s = u_rw[M_PROMPT + DEC_SEQ - 1::DEC_SEQ, :RWKV_COLS]
        prompt_states.append((ckv_p.reshape(BATCH, SEQ, KV_RANK), kpef_p.reshape(BATCH, SEQ, MLA_ROPE), shift_p, s_p,
                              c_p, n_p[:, :MLSTM_HEADS], m_p[:, :MLSTM_HEADS, 0]))
        sample_states.append((ckv_s.reshape(DEC_BATCH, DEC_SEQ, KV_RANK), kpef_s.reshape(DEC_BATCH, DEC_SEQ, MLA_ROPE),
                              shift_s, None, None, n_s[:, :MLSTM_HEADS], m_s[:, :MLSTM_HEADS, 0]))

    outs = [x[:M_PROMPT].reshape(BATCH, SEQ, d), x[M_PROMPT:].reshape(DEC_BATCH, DEC_SEQ, d)]
    for states, stacked in ((prompt_states, {}), (sample_states, {3: s_stack, 4: c_stack})):
        for i in range(7):
            outs.append(stacked[i] if i in stacked else jnp.stack([st[i] for st in states]))
    return tuple(outs)
```
